```python
import jax
import jax.numpy as jnp
from jax import lax
import numpy as np

D_MODEL = 2048
BATCH = 8
SEQ = 2048
DEPTH = 1

N_META = 16
ATTN_BLOCK = 128
WINDOW = 128
N_Q_HEADS = 16
N_KV_HEADS = 2
HEAD_DIM = 64
ROPE_THETA = 10000.0
N_REC_HEADS = 8
REC_KEY_DIM = 128
REC_VAL_DIM = 128
REC_CHUNK = 16
N_EXPERTS = 32
TOP_K = 4
D_EXPERT = D_MODEL
SWIGLU_ALPHA = 1.702
SWIGLU_LIMIT = 7.0
MOE_BLOCK = 128
NORM_EPS = 1e-5
NEG_INF = -1e30

Q_WIDTH = N_Q_HEADS * HEAD_DIM
KV_WIDTH = N_KV_HEADS * HEAD_DIM
REC_K_WIDTH = N_REC_HEADS * REC_KEY_DIM
REC_V_WIDTH = N_REC_HEADS * REC_VAL_DIM
IN_SIZES = (Q_WIDTH, KV_WIDTH, KV_WIDTH, REC_K_WIDTH, REC_K_WIDTH, REC_V_WIDTH, REC_V_WIDTH, D_MODEL, D_MODEL)
IN_WIDTH = sum(IN_SIZES)
IN_SPLITS = tuple(int(s) for s in np.cumsum(IN_SIZES)[:-1])

kernel_name = 'hybrid_swa_hgrn2_moe_block'


def rms_norm(x, g):
    xf = x.astype(jnp.float32)
    y = xf * lax.rsqrt(jnp.mean(xf * xf, axis=-1, keepdims=True) + NORM_EPS)
    return (y * g.astype(jnp.float32)).astype(x.dtype)


def apply_rope(x, pos):
    half = HEAD_DIM // 2
    inv_freq = ROPE_THETA ** (-jnp.arange(half, dtype=jnp.float32) / half)
    ang = pos.astype(jnp.float32)[:, None] * inv_freq[None, :]
    cos = jnp.cos(ang)[None, :, None, :]
    sin = jnp.sin(ang)[None, :, None, :]
    xf = x.astype(jnp.float32)
    x1, x2 = xf[..., :half], xf[..., half:]
    return jnp.concatenate([x1 * cos - x2 * sin, x2 * cos + x1 * sin], axis=-1).astype(x.dtype)


def sliding_window_attention(q, k, v, sinks):
    B, T = q.shape[0], q.shape[1]
    G = N_Q_HEADS // N_KV_HEADS
    pos = jnp.arange(T)
    q = apply_rope(q, pos)
    k = apply_rope(k, pos)
    meta_k, meta_v = k[:, :N_META], v[:, :N_META]
    pad = ATTN_BLOCK - N_META
    padw = ((0, 0), (pad, 0), (0, 0), (0, 0))
    qp, kp, vp = jnp.pad(q, padw), jnp.pad(k, padw), jnp.pad(v, padw)
    L = T + pad
    nb = L // ATTN_BLOCK
    qb = qp.reshape(B, nb, ATTN_BLOCK, N_KV_HEADS, G, HEAD_DIM)
    kb = kp.reshape(B, nb, ATTN_BLOCK, N_KV_HEADS, HEAD_DIM)
    vb = vp.reshape(B, nb, ATTN_BLOCK, N_KV_HEADS, HEAD_DIM)
    shift = ((0, 0), (1, 0), (0, 0), (0, 0), (0, 0))
    k_band = jnp.concatenate([jnp.pad(kb, shift)[:, :nb], kb], axis=2)
    v_band = jnp.concatenate([jnp.pad(vb, shift)[:, :nb], vb], axis=2)
    scale = HEAD_DIM ** -0.5
    s_band = jnp.einsum('bnqhgd,bnkhd->bnhgqk', qb, k_band).astype(jnp.float32) * scale
    s_meta = jnp.einsum('bnqhgd,bmhd->bnhgqm', qb, meta_k).astype(jnp.float32) * scale
    q_idx = jnp.arange(nb)[:, None] * ATTN_BLOCK + jnp.arange(ATTN_BLOCK)[None, :]
    k_idx = (jnp.arange(nb)[:, None] - 1) * ATTN_BLOCK + jnp.arange(2 * ATTN_BLOCK)[None, :]
    diff = q_idx[:, :, None] - k_idx[:, None, :]
    band_ok = (diff >= 0) & (diff < WINDOW) & (k_idx[:, None, :] >= ATTN_BLOCK)
    meta_ok = (q_idx[:, :, None] - pad) >= jnp.arange(N_META)[None, None, :]
    sink = jnp.broadcast_to(sinks.astype(jnp.float32).reshape(N_KV_HEADS, G)[None, None, :, :, None, None],
                            (B, nb, N_KV_HEADS, G, ATTN_BLOCK, 1))
    logits = jnp.concatenate([jnp.where(meta_ok[None, :, None, None], s_meta, NEG_INF),
                              jnp.where(band_ok[None, :, None, None], s_band, NEG_INF),
                              sink], axis=-1)
    p = jax.nn.softmax(logits, axis=-1).astype(v.dtype)
    p_meta = p[..., :N_META]
    p_band = p[..., N_META:N_META + 2 * ATTN_BLOCK]
    o = (jnp.einsum('bnhgqm,bmhd->bnqhgd', p_meta, meta_v)
         + jnp.einsum('bnhgqk,bnkhd->bnqhgd', p_band, v_band))
    return o.reshape(B, L, Q_WIDTH)[:, pad:]


def hgrn2_mixer(rq, rf, ri, rg, lower_bound, norm_g):
    B, T = rq.shape[0], rq.shape[1]
    f32 = jnp.float32
    nc = T // REC_CHUNK
    shp_k = (B, nc, REC_CHUNK, N_REC_HEADS, REC_KEY_DIM)
    shp_v = (B, nc, REC_CHUNK, N_REC_HEADS, REC_VAL_DIM)
    f = lower_bound + (1.0 - lower_bound) * jax.nn.sigmoid(rf.astype(f32))
    log_f = jnp.log(f).reshape(shp_k)
    k = (1.0 - f).reshape(shp_k)
    q = rq.astype(f32).reshape(shp_k)
    v = ri.astype(f32).reshape(shp_v)
    b = jnp.cumsum(log_f, axis=2)
    b_last = b[:, :, -1:]
    q_dec = q * jnp.exp(b)
    k_dec = k * jnp.exp(-b)
    k_to_end = k * jnp.exp(b_last - b)
    causal = jnp.tril(jnp.ones((REC_CHUNK, REC_CHUNK), dtype=bool))
    scores = jnp.where(causal, jnp.einsum('bnchk,bndhk->bnhcd', q_dec, k_dec), 0.0)
    o_intra = jnp.einsum('bnhcd,bndhv->bnchv', scores, v)
    chunk_decay = jnp.exp(b_last[:, :, 0])

    def step(state, inp):
        q_c, k_c, v_c, d_c = inp
        o_c = jnp.einsum('bchk,bhkv->bchv', q_c, state)
        state = d_c[..., None] * state + jnp.einsum('bchk,bchv->bhkv', k_c, v_c)
        return state, o_c

    state0 = jnp.zeros((B, N_REC_HEADS, REC_KEY_DIM, REC_VAL_DIM), f32)
    xs = (jnp.moveaxis(q_dec, 1, 0), jnp.moveaxis(k_to_end, 1, 0),
          jnp.moveaxis(v, 1, 0), jnp.moveaxis(chunk_decay, 1, 0))
    _, o_inter = lax.scan(step, state0, xs)
    o = (o_intra + jnp.moveaxis(o_inter, 0, 1)).reshape(B, T, N_REC_HEADS, REC_VAL_DIM)
    o = o * lax.rsqrt(jnp.mean(o * o, axis=-1, keepdims=True) + NORM_EPS) * norm_g.astype(f32)
    g = rg.astype(f32).reshape(B, T, N_REC_HEADS, REC_VAL_DIM)
    return (o * jax.nn.silu(g)).reshape(B, T, REC_V_WIDTH).astype(rq.dtype)


def clamped_swiglu(gate, up):
    gate = jnp.minimum(gate, SWIGLU_LIMIT)
    up = jnp.clip(up, -SWIGLU_LIMIT, SWIGLU_LIMIT)
    return gate * jax.nn.sigmoid(SWIGLU_ALPHA * gate) * (up + 1.0)


def moe_ffn(h, w_router, b_router, w_gate, b_gate, w_up, b_up, w_down, b_down):
    B, T, D = h.shape
    N = B * T
    A = N * TOP_K
    x = h.reshape(N, D)
    logits = (x @ w_router).astype(jnp.float32) + b_router.astype(jnp.float32)
    top_val, top_idx = lax.top_k(logits, TOP_K)
    gates = jax.nn.softmax(top_val, axis=-1).astype(h.dtype)
    e_flat = top_idx.reshape(A)
    tok_flat = jnp.arange(A, dtype=jnp.int32) // TOP_K
    g_flat = gates.reshape(A)
    order = jnp.argsort(e_flat)
    e_sorted = e_flat[order]
    counts = jnp.bincount(e_flat, length=N_EXPERTS)
    padded = ((counts + MOE_BLOCK - 1) // MOE_BLOCK) * MOE_BLOCK
    pad_end = jnp.cumsum(padded)
    pad_start = pad_end - padded
    start = jnp.cumsum(counts) - counts
    dest = pad_start[e_sorted] + jnp.arange(A) - start[e_sorted]
    n_blk = -(-(A + N_EXPERTS * (MOE_BLOCK - 1)) // MOE_BLOCK)
    P = n_blk * MOE_BLOCK
    slot_tok = jnp.full((P,), N, dtype=jnp.int32).at[dest].set(tok_flat[order])
    slot_gate = jnp.zeros((P,), h.dtype).at[dest].set(g_flat[order])
    blk_exp = jnp.minimum(jnp.searchsorted(pad_end, jnp.arange(n_blk) * MOE_BLOCK, side='right'),
                          N_EXPERTS - 1)
    xp = jnp.concatenate([x, jnp.zeros((1, D), x.dtype)], axis=0)

    def expert_block(args):
        tok, e = args
        xb = xp[tok]
        gt = xb @ w_gate[e] + b_gate[e]
        up = xb @ w_up[e] + b_up[e]
        return clamped_swiglu(gt, up) @ w_down[e] + b_down[e]

    yb = lax.map(expert_block, (slot_tok.reshape(n_blk, MOE_BLOCK), blk_exp))
    y = yb.reshape(P, D) * slot_gate[:, None]
    out = jax.ops.segment_sum(y, slot_tok, num_segments=N + 1)[:N]
    return out.reshape(B, T, D)


def setup_inputs(seed: int = 0) -> dict:
    key = jax.random.key(seed)
    ks = jax.random.split(key, 20)
    f32 = jnp.float32

    def nrm(k, shape, scale):
        return jax.random.normal(k, shape, f32) * scale

    def gain(k, shape):
        return 1.0 + 0.02 * jax.random.normal(k, shape, f32)

    L, D, E, F = DEPTH, D_MODEL, N_EXPERTS, D_EXPERT
    return {
        'x': nrm(ks[0], (BATCH, SEQ, D), 1.0),
        'meta_tokens': nrm(ks[1], (N_META, D), 1.0),
        'attn_norm_g': gain(ks[2], (L, D)),
        'w_in': nrm(ks[3], (L, D, IN_WIDTH), D ** -0.5),
        'attn_sinks': nrm(ks[4], (L, N_Q_HEADS), 1.0),
        'lower_bound_logits': nrm(ks[5], (L + 1, REC_K_WIDTH), 1.0),
        'rec_norm_g': gain(ks[6], (L, REC_VAL_DIM)),
        'w_attn_proj': nrm(ks[7], (L, Q_WIDTH, D), Q_WIDTH ** -0.5),
        'w_rec_proj': nrm(ks[8], (L, REC_V_WIDTH, D), REC_V_WIDTH ** -0.5),
        'w_out': nrm(ks[9], (L, D, D), D ** -0.5),
        'ffn_norm_g': gain(ks[10], (L, D)),
        'w_router': nrm(ks[11], (L, D, E), D ** -0.5),
        'b_router': nrm(ks[12], (L, E), 0.01),
        'w_gate': nrm(ks[13], (L, E, D, F), D ** -0.5),
        'b_gate': nrm(ks[14], (L, E, F), 0.01),
        'w_up': nrm(ks[15], (L, E, D, F), D ** -0.5),
        'b_up': nrm(ks[16], (L, E, F), 0.01),
        'w_down': nrm(ks[17], (L, E, F, D), F ** -0.5),
        'b_down': nrm(ks[18], (L, E, D), 0.01),
        'final_norm_g': gain(ks[19], (D,)),
    }


def reference(x, meta_tokens, attn_norm_g, w_in, attn_sinks, lower_bound_logits, rec_norm_g,
              w_attn_proj, w_rec_proj, w_out, ffn_norm_g, w_router, b_router, w_gate, b_gate,
              w_up, b_up, w_down, b_down, final_norm_g):
    B = x.shape[0]
    meta = jnp.broadcast_to(meta_tokens.astype(x.dtype)[None], (B, N_META, D_MODEL))
    h = jnp.concatenate([meta, x], axis=1)
    T = h.shape[1]
    lb_all = jnp.cumsum(jax.nn.softmax(lower_bound_logits.astype(jnp.float32), axis=0), axis=0)
    for l in range(DEPTH):
        u = rms_norm(h, attn_norm_g[l])
        proj = u @ w_in[l]
        q, k, v, rq, rf, ri, rg, ga, gr = jnp.split(proj, IN_SPLITS, axis=-1)
        attn = sliding_window_attention(q.reshape(B, T, N_Q_HEADS, HEAD_DIM),
                                        k.reshape(B, T, N_KV_HEADS, HEAD_DIM),
                                        v.reshape(B, T, N_KV_HEADS, HEAD_DIM), attn_sinks[l])
        rec = hgrn2_mixer(rq, rf, ri, rg, lb_all[l], rec_norm_g[l])
        mixed = jax.nn.sigmoid(ga) * (attn @ w_attn_proj[l]) + jax.nn.sigmoid(gr) * (rec @ w_rec_proj[l])
        h = h + mixed @ w_out[l]
        h = h + moe_ffn(rms_norm(h, ffn_norm_g[l]), w_router[l], b_router[l], w_gate[l], b_gate[l],
                        w_up[l], b_up[l], w_down[l], b_down[l])
    return rms_norm(h, final_norm_g)[:, N_META:]
```

```python
import functools

import jax
import jax.numpy as jnp
import numpy as np
from jax import lax
from jax.experimental import pallas as pl
from jax.experimental.pallas import tpu as pltpu

F32 = jnp.float32
BF16 = jnp.bfloat16
I32 = jnp.int32

N_META_TOK = 16
HEAD_DIM = 64
N_Q_HEADS = 16
N_KV_HEADS = 2
ATTN_BLK = 128
ROPE_THETA = 10000.0
N_REC_HEADS = 8
REC_DIM = 128
REC_CHUNK = 16
N_EXPERTS = 32
TOP_K = 4
SWIGLU_ALPHA = 1.702
SWIGLU_LIMIT = 7.0
NORM_EPS = 1e-5
NEG_INF = -1e30

LANES = 128
ROW_TILE = 512
MIX_TILE = 256
SEG = 256
MOE_BLK = 256
MOE_FC = 1024
TOK_TILE = 128
VMEM_LIMIT = 56 * 1024 * 1024

Q_W = N_Q_HEADS * HEAD_DIM
REC_W = N_REC_HEADS * REC_DIM


def _cparams(sem, vmem=None):
    return pltpu.CompilerParams(dimension_semantics=sem, vmem_limit_bytes=vmem)


def _rms(xf, g):
    return xf * lax.rsqrt(jnp.mean(xf * xf, axis=-1, keepdims=True) + NORM_EPS) * g


def _norm_qkv_kernel(h_ref, g_ref, w_ref, cos_ref, sin_ref, u_ref, qkv_ref):
    u = _rms(h_ref[...], g_ref[...]).astype(BF16)
    u_ref[...] = u
    p = jnp.dot(u, w_ref[...], preferred_element_type=F32)
    cos = cos_ref[...]
    sin = sin_ref[...]
    lane = lax.broadcasted_iota(I32, cos.shape, 1)
    first_half = (lane % HEAD_DIM) < (HEAD_DIM // 2)
    n_q = Q_W // LANES
    for c in range(n_q + 2):
        xs = p[:, c * LANES:(c + 1) * LANES]
        swapped = jnp.where(first_half, pltpu.roll(xs, LANES - HEAD_DIM // 2, 1),
                            pltpu.roll(xs, HEAD_DIM // 2, 1))
        r = xs * cos + swapped * sin
        if c < n_q:
            r = r * (HEAD_DIM ** -0.5)
        qkv_ref[:, c * LANES:(c + 1) * LANES] = r.astype(BF16)
    v0 = (n_q + 2) * LANES
    qkv_ref[:, v0:] = p[:, v0:].astype(BF16)


def _norm_qkv(h_rows, g, w_qkv, cos_t, sin_t, n_seq_tiles, tiles_per_seq):
    R, D = h_rows.shape
    W = w_qkv.shape[1]

    def tab_map(i):
        return (jnp.where(i < n_seq_tiles, i % tiles_per_seq, tiles_per_seq), 0)

    return pl.pallas_call(
        _norm_qkv_kernel,
        grid=(R // ROW_TILE,),
        in_specs=[
            pl.BlockSpec((ROW_TILE, D), lambda i: (i, 0)),
            pl.BlockSpec((1, D), lambda i: (0, 0)),
            pl.BlockSpec((D, W), lambda i: (0, 0)),
            pl.BlockSpec((ROW_TILE, LANES), tab_map),
            pl.BlockSpec((ROW_TILE, LANES), tab_map),
        ],
        out_specs=[
            pl.BlockSpec((ROW_TILE, D), lambda i: (i, 0)),
            pl.BlockSpec((ROW_TILE, W), lambda i: (i, 0)),
        ],
        out_shape=[jax.ShapeDtypeStruct((R, D), BF16), jax.ShapeDtypeStruct((R, W), BF16)],
        compiler_params=_cparams(("parallel",), VMEM_LIMIT),
        name="norm_qkv",
    )(h_rows, g, w_qkv, cos_t, sin_t)


def _matmul_kernel(x_ref, w_ref, o_ref):
    o_ref[...] = jnp.dot(x_ref[...], w_ref[...], preferred_element_type=F32).astype(o_ref.dtype)


def _proj_rest(u, w):
    R, D = u.shape
    N = w.shape[1]
    tn = 1024
    return pl.pallas_call(
        _matmul_kernel,
        grid=(R // ROW_TILE, N // tn),
        in_specs=[pl.BlockSpec((ROW_TILE, D), lambda i, j: (i, 0)),
                  pl.BlockSpec((D, tn), lambda i, j: (0, j))],
        out_specs=pl.BlockSpec((ROW_TILE, tn), lambda i, j: (i, j)),
        out_shape=jax.ShapeDtypeStruct((R, N), F32),
        compiler_params=_cparams(("parallel", "parallel"), VMEM_LIMIT),
        name="proj_rest",
    )(u, w)


def _attn_core(q_ref, k_groups, v_groups, mask, sink_ref, write):
    nkeys = k_groups[0].shape[0]
    lo_k = lax.broadcasted_iota(I32, (nkeys, LANES), 1) < HEAD_DIM
    rows = q_ref.shape[0]
    lo_q = lax.broadcasted_iota(I32, (rows, LANES), 1) < HEAD_DIM
    pairs_per_group = (N_Q_HEADS // N_KV_HEADS) // 2
    for g in range(N_KV_HEADS):
        kk = k_groups[g]
        vv = v_groups[g]
        zero_v = jnp.zeros_like(vv)
        v_half = (jnp.where(lo_k, vv, zero_v), jnp.where(lo_k, zero_v, vv))
        for j in range(pairs_per_group):
            pr = pairs_per_group * g + j
            qp = q_ref[:, pr * LANES:(pr + 1) * LANES]
            zero_q = jnp.zeros_like(qp)
            q_half = (jnp.where(lo_q, qp, zero_q), jnp.where(lo_q, zero_q, qp))
            acc = None
            for half in range(2):
                s = lax.dot_general(q_half[half], kk, (((1,), (1,)), ((), ())),
                                    preferred_element_type=F32)
                s = jnp.where(mask, s, NEG_INF)
                sk = sink_ref[2 * pr + half]
                m = jnp.maximum(jnp.max(s, axis=-1, keepdims=True), sk)
                e = jnp.exp(s - m)
                den = jnp.sum(e, axis=-1, keepdims=True) + jnp.exp(sk - m)
                prob = (e / den).astype(BF16)
                o = jnp.dot(prob, v_half[half], preferred_element_type=F32)
                acc = o if acc is None else acc + o
            write(pr, acc)


def _attn_seq_kernel(sink_ref, q_ref, kc0, kc1, vc0, vc1, kp0, kp1, vp0, vp1,
                     km0, km1, vm0, vm1, o_ref):
    n = pl.program_id(1)
    nk = N_META_TOK + 2 * ATTN_BLK
    ri = lax.broadcasted_iota(I32, (ATTN_BLK, nk), 0)
    ci = lax.broadcasted_iota(I32, (ATTN_BLK, nk), 1)
    no_prev = jnp.where(n > 0, 0, 2 * ATTN_BLK)
    prev_ok = (ci >= N_META_TOK) & (ci < N_META_TOK + ATTN_BLK) & (ci - N_META_TOK > ri + no_prev)
    cur_ok = (ci >= N_META_TOK + ATTN_BLK) & (ci - (N_META_TOK + ATTN_BLK) <= ri)
    mask = (ci < N_META_TOK) | prev_ok | cur_ok
    k_groups = [jnp.concatenate([km[...], kp[...], kc[...]], axis=0)
                for km, kp, kc in ((km0, kp0, kc0), (km1, kp1, kc1))]
    v_groups = [jnp.concatenate([vm[...], vp[...], vc[...]], axis=0)
                for vm, vp, vc in ((vm0, vp0, vc0), (vm1, vp1, vc1))]

    def write(pr, acc):
        o_ref[:, pr * LANES:(pr + 1) * LANES] = acc.astype(o_ref.dtype)

    _attn_core(q_ref, k_groups, v_groups, mask, sink_ref, write)


def _attn_meta_kernel(sink_ref, q_ref, k0, k1, v0, v1, prev_ref, o_ref):
    del prev_ref
    nm = q_ref.shape[0]
    ri = lax.broadcasted_iota(I32, (nm, nm), 0)
    ci = lax.broadcasted_iota(I32, (nm, nm), 1)
    same_batch = (ri // N_META_TOK) == (ci // N_META_TOK)
    mask = same_batch & ((ci % N_META_TOK) <= (ri % N_META_TOK))
    o_ref[...] = jnp.zeros_like(o_ref)

    def write(pr, acc):
        o_ref[0:nm, pr * LANES:(pr + 1) * LANES] = acc.astype(o_ref.dtype)

    _attn_core(q_ref, [k0[...], k1[...]], [v0[...], v1[...]], mask, sink_ref, write)


def _attention(qkv, sinks, B, S):
    R = qkv.shape[0]
    NS = B * S
    NM = B * N_META_TOK
    nb = S // ATTN_BLK
    qc = Q_W // LANES
    smem = pl.BlockSpec(memory_space=pltpu.SMEM)

    def kv_spec(col, prev):
        if prev:
            return pl.BlockSpec((ATTN_BLK, LANES), lambda b, n: (b * nb + jnp.maximum(n - 1, 0), col))
        return pl.BlockSpec((ATTN_BLK, LANES), lambda b, n: (b * nb + n, col))

    def meta_spec(col):
        return pl.BlockSpec((N_META_TOK, LANES), lambda b, n: (NS // N_META_TOK + b, col))

    in_specs = [smem, pl.BlockSpec((ATTN_BLK, Q_W), lambda b, n: (b * nb + n, 0))]
    in_specs += [kv_spec(qc, False), kv_spec(qc + 1, False), kv_spec(qc + 2, False), kv_spec(qc + 3, False)]
    in_specs += [kv_spec(qc, True), kv_spec(qc + 1, True), kv_spec(qc + 2, True), kv_spec(qc + 3, True)]
    in_specs += [meta_spec(qc), meta_spec(qc + 1), meta_spec(qc + 2), meta_spec(qc + 3)]
    attn = pl.pallas_call(
        _attn_seq_kernel,
        grid=(B, nb),
        in_specs=in_specs,
        out_specs=pl.BlockSpec((ATTN_BLK, Q_W), lambda b, n: (b * nb + n, 0)),
        out_shape=jax.ShapeDtypeStruct((R, Q_W), BF16),
        compiler_params=_cparams(("parallel", "parallel"), VMEM_LIMIT),
        name="attn_seq",
    )(sinks, *([qkv] * 13))

    mb = NS // NM
    tail = R - NS
    blk = lambda col: pl.BlockSpec((NM, LANES), lambda i: (mb, col))
    return pl.pallas_call(
        _attn_meta_kernel,
        grid=(1,),
        in_specs=[smem, pl.BlockSpec((NM, Q_W), lambda i: (mb, 0)),
                  blk(qc), blk(qc + 1), blk(qc + 2), blk(qc + 3),
                  pl.BlockSpec(memory_space=pl.ANY)],
        out_specs=pl.BlockSpec((tail, Q_W), lambda i: (NS // tail, 0)),
        out_shape=jax.ShapeDtypeStruct((R, Q_W), BF16),
        input_output_aliases={6: 0},
        compiler_params=_cparams(("arbitrary",), VMEM_LIMIT),
        name="attn_meta",
    )(sinks, qkv, qkv, qkv, qkv, qkv, attn)


def _hgrn_prep_kernel(rq_ref, rf_ref, ri_ref, lbl_ref, qd_ref, kd_ref, ke_ref, v_ref, eb_ref):
    lbl = lbl_ref[...]
    e = jnp.exp(lbl - jnp.max(lbl, axis=0, keepdims=True))
    lb = e[0:1] / jnp.sum(e, axis=0, keepdims=True)
    f = lb + (1.0 - lb) * jax.nn.sigmoid(rf_ref[...])
    logf = jnp.log(f)
    k = 1.0 - f
    tm = logf.shape[0]
    pos = lax.broadcasted_iota(I32, logf.shape, 0) % REC_CHUNK
    b = logf
    suf = logf
    s = 1
    while s < REC_CHUNK:
        b = b + jnp.where(pos >= s, pltpu.roll(b, s, 0), 0.0)
        suf = suf + jnp.where(pos < REC_CHUNK - s, pltpu.roll(suf, tm - s, 0), 0.0)
        s *= 2
    eb = jnp.exp(b)
    eb_ref[...] = eb
    qd_ref[...] = (rq_ref[...] * eb).astype(BF16)
    kd_ref[...] = (k * jnp.exp(-b)).astype(BF16)
    ke_ref[...] = (k * jnp.exp(suf - logf)).astype(BF16)
    v_ref[...] = ri_ref[...].astype(BF16)


def _hgrn_prep(rest, lbl):
    R = rest.shape[0]
    W = REC_W
    col = lambda c: pl.BlockSpec((SEG, W), lambda i: (i, c))
    row_out = pl.BlockSpec((SEG, W), lambda i: (i, 0))
    return pl.pallas_call(
        _hgrn_prep_kernel,
        grid=(R // SEG,),
        in_specs=[col(0), col(1), col(2), pl.BlockSpec(lbl.shape, lambda i: (0, 0))],
        out_specs=[row_out] * 5,
        out_shape=[jax.ShapeDtypeStruct((R, W), BF16)] * 4 + [jax.ShapeDtypeStruct((R, W), F32)],
        compiler_params=_cparams(("parallel",), VMEM_LIMIT),
        name="hgrn_prep",
    )(rest, rest, rest, lbl)


def _hgrn_chunk(qd, kd, ke, v, dec, st, with_state):
    causal = (lax.broadcasted_iota(I32, (REC_CHUNK, REC_CHUNK), 0)
              >= lax.broadcasted_iota(I32, (REC_CHUNK, REC_CHUNK), 1))
    sc = lax.dot_general(qd, kd, (((1,), (1,)), ((), ())), preferred_element_type=F32)
    sc = jnp.where(causal, sc, 0.0)
    o = jnp.dot(sc.astype(BF16), v, preferred_element_type=F32)
    kv_t = lax.dot_general(v, ke, (((0,), (0,)), ((), ())), preferred_element_type=F32)
    if with_state:
        o = o + lax.dot_general(qd, st.astype(BF16), (((1,), (1,)), ((), ())),
                                preferred_element_type=F32)
        st_new = st * dec + kv_t
    else:
        st_new = kv_t
    return o, st_new


def _hgrn_finish(o, g, norm_g):
    y = o * lax.rsqrt(jnp.mean(o * o, axis=-1, keepdims=True) + NORM_EPS) * norm_g
    return (y * (g * jax.nn.sigmoid(g))).astype(BF16)


def _hgrn_meta_kernel(qd_ref, kd_ref, ke_ref, v_ref, rg_ref, ng_ref, st_ref, rec_ref, *, nbatch):
    rec_ref[...] = jnp.zeros_like(rec_ref)
    for b in range(nbatch):
        r0 = b * REC_CHUNK
        for h in range(N_REC_HEADS):
            c0 = h * REC_DIM
            sl = (slice(r0, r0 + REC_CHUNK), slice(c0, c0 + REC_DIM))
            o, st = _hgrn_chunk(qd_ref[sl], kd_ref[sl], ke_ref[sl], v_ref[sl], None, None, False)
            st_ref[b, h] = st
            rec_ref[sl] = _hgrn_finish(o, rg_ref[sl], ng_ref[...])


def _hgrn_scan_kernel(qd_ref, kd_ref, ke_ref, v_ref, eb_ref, rg_ref, ng_ref, st0_ref, prev_ref,
                      rec_ref, st_scr, o_scr):
    del prev_ref

    @pl.when(pl.program_id(1) == 0)
    def _():
        st_scr[...] = st0_ref[0]

    def chunk(c, carry):
        r0 = pl.multiple_of(c * REC_CHUNK, REC_CHUNK)
        rows = pl.ds(r0, REC_CHUNK)
        for h in range(N_REC_HEADS):
            cols = slice(h * REC_DIM, (h + 1) * REC_DIM)
            last8 = eb_ref[pl.ds(pl.multiple_of(r0 + REC_CHUNK - 8, 8), 8), cols]
            dec = last8[7:8]
            o, st = _hgrn_chunk(qd_ref[rows, cols], kd_ref[rows, cols], ke_ref[rows, cols],
                                v_ref[rows, cols], dec, st_scr[h], True)
            st_scr[h] = st
            o_scr[rows, cols] = o
        return carry

    lax.fori_loop(0, SEG // REC_CHUNK, chunk, 0)
    for h in range(N_REC_HEADS):
        cols = slice(h * REC_DIM, (h + 1) * REC_DIM)
        rec_ref[:, cols] = _hgrn_finish(o_scr[:, cols], rg_ref[:, cols], ng_ref[...])


def _hgrn(rest, lbl, norm_g, B, S):
    R = rest.shape[0]
    NS = B * S
    W = REC_W
    tail = R - NS
    tb = NS // tail
    qd, kd, ke, v, eb = _hgrn_prep(rest, lbl)

    tail_spec = pl.BlockSpec((tail, W), lambda i: (tb, 0))
    state, rec = pl.pallas_call(
        functools.partial(_hgrn_meta_kernel, nbatch=B),
        grid=(1,),
        in_specs=[tail_spec, tail_spec, tail_spec, tail_spec,
                  pl.BlockSpec((tail, W), lambda i: (tb, 3)),
                  pl.BlockSpec((1, REC_DIM), lambda i: (0, 0))],
        out_specs=[pl.BlockSpec((B, N_REC_HEADS, REC_DIM, REC_DIM), lambda i: (0, 0, 0, 0)),
                   tail_spec],
        out_shape=[jax.ShapeDtypeStruct((B, N_REC_HEADS, REC_DIM, REC_DIM), F32),
                   jax.ShapeDtypeStruct((R, W), BF16)],
        compiler_params=_cparams(("arbitrary",), VMEM_LIMIT),
        name="hgrn_meta",
    )(qd, kd, ke, v, rest, norm_g)

    ns = S // SEG
    seg = pl.BlockSpec((SEG, W), lambda b, s: (b * ns + s, 0))
    return pl.pallas_call(
        _hgrn_scan_kernel,
        grid=(B, ns),
        in_specs=[seg, seg, seg, seg, seg,
                  pl.BlockSpec((SEG, W), lambda b, s: (b * ns + s, 3)),
                  pl.BlockSpec((1, REC_DIM), lambda b, s: (0, 0)),
                  pl.BlockSpec((1, N_REC_HEADS, REC_DIM, REC_DIM), lambda b, s: (b, 0, 0, 0)),
                  pl.BlockSpec(memory_space=pl.ANY)],
        out_specs=seg,
        out_shape=jax.ShapeDtypeStruct((R, W), BF16),
        input_output_aliases={8: 0},
        scratch_shapes=[pltpu.VMEM((N_REC_HEADS, REC_DIM, REC_DIM), F32),
                        pltpu.VMEM((SEG, W), F32)],
        compiler_params=_cparams(("parallel", "arbitrary"), VMEM_LIMIT),
        name="hgrn_scan",
    )(qd, kd, ke, v, eb, rest, norm_g, state, rec)


def _mix_kernel(attn_ref, rec_ref, ga_ref, gr_ref, h_ref, wa_ref, wr_ref, wo_ref, g_ref,
                wrt_ref, brt_ref, h2_ref, xn_ref, lg_ref):
    a = jnp.dot(attn_ref[...], wa_ref[...], preferred_element_type=F32)
    r = jnp.dot(rec_ref[...], wr_ref[...], preferred_element_type=F32)
    mixed = jax.nn.sigmoid(ga_ref[...]) * a + jax.nn.sigmoid(gr_ref[...]) * r
    h2 = h_ref[...] + jnp.dot(mixed.astype(BF16), wo_ref[...], preferred_element_type=F32)
    h2_ref[...] = h2
    xn = _rms(h2, g_ref[...])
    xn_ref[...] = xn
    lg_ref[...] = jnp.dot(xn, wrt_ref[...], preferred_element_type=F32,
                          precision=lax.Precision.HIGHEST) + brt_ref[...]


def _mix(attn, rec, rest, h_rows, wa, wr, wo, g, w_router, b_router):
    R, D = h_rows.shape
    tm = MIX_TILE
    const = lambda shape: pl.BlockSpec(shape, lambda i: (0, 0), pipeline_mode=pl.Buffered(1))
    row = lambda w: pl.BlockSpec((tm, w), lambda i: (i, 0))
    return pl.pallas_call(
        _mix_kernel,
        grid=(R // tm,),
        in_specs=[row(Q_W), row(REC_W),
                  pl.BlockSpec((tm, D), lambda i: (i, 2)), pl.BlockSpec((tm, D), lambda i: (i, 3)),
                  row(D), const(wa.shape), const(wr.shape), const(wo.shape), const(g.shape),
                  const(w_router.shape), const(b_router.shape)],
        out_specs=[row(D), row(D), row(LANES)],
        out_shape=[jax.ShapeDtypeStruct((R, D), F32), jax.ShapeDtypeStruct((R, D), F32),
                   jax.ShapeDtypeStruct((R, LANES), F32)],
        compiler_params=_cparams(("parallel",), VMEM_LIMIT),
        name="mix_outproj",
    )(attn, rec, rest, rest, h_rows, wa, wr, wo, g, w_router, b_router)


def _route_kernel(lg_ref, dest_ref, gate_ref, cnt_ref, cnt_scr, carry_scr, start_scr, *, n_tok):
    ph = pl.program_id(0)
    i = pl.program_id(1)
    tm = lg_ref.shape[0]

    @pl.when((ph == 0) & (i == 0))
    def _():
        cnt_scr[...] = jnp.zeros_like(cnt_scr)

    lane = lax.broadcasted_iota(I32, (tm, LANES), 1)
    valid = (i * tm + lax.broadcasted_iota(I32, (tm, LANES), 0)) < n_tok
    work = lg_ref[...]
    onehots, vals = [], []
    for _ in range(TOP_K):
        m = jnp.max(work, axis=-1, keepdims=True)
        idx = jnp.min(jnp.where(work == m, lane, LANES), axis=-1, keepdims=True)
        oh = lane == idx
        onehots.append(oh)
        vals.append(m)
        work = jnp.where(oh, -jnp.inf, work)
    multi = jnp.zeros((tm, LANES), F32)
    for oh in onehots:
        multi = multi + jnp.where(oh & valid, 1.0, 0.0)
    tile_cnt = jnp.sum(multi, axis=0, keepdims=True)

    @pl.when(ph == 0)
    def _():
        cnt_scr[...] += tile_cnt

    @pl.when(ph == 1)
    def _():
        @pl.when(i == 0)
        def _():
            c = cnt_scr[...]
            padded = jnp.ceil(c * (1.0 / MOE_BLK)) * MOE_BLK
            before = (lax.broadcasted_iota(I32, (LANES, LANES), 0)
                      < lax.broadcasted_iota(I32, (LANES, LANES), 1))
            start = jnp.dot(jnp.broadcast_to(padded, (8, LANES)), jnp.where(before, 1.0, 0.0),
                            preferred_element_type=F32, precision=lax.Precision.HIGHEST)
            start_scr[...] = start[0:1]
            carry_scr[...] = jnp.zeros_like(carry_scr)
            cnt_ref[...] = c

        earlier = (lax.broadcasted_iota(I32, (tm, tm), 1) < lax.broadcasted_iota(I32, (tm, tm), 0))
        prefix = jnp.dot(jnp.where(earlier, 1.0, 0.0).astype(BF16), multi.astype(BF16),
                         preferred_element_type=F32)
        base = prefix + carry_scr[...] + start_scr[...]
        den = jnp.zeros_like(vals[0])
        for v in vals:
            den = den + jnp.exp(v - vals[0])
        dest = jnp.zeros((tm, LANES), F32)
        gate = jnp.zeros((tm, LANES), F32)
        for k in range(TOP_K):
            d_k = jnp.sum(jnp.where(onehots[k], base, 0.0), axis=-1, keepdims=True)
            dest = jnp.where(lane == k, d_k, dest)
            gate = jnp.where(lane == k, jnp.exp(vals[k] - vals[0]) / den, gate)
        dest_ref[...] = dest.astype(I32)
        gate_ref[...] = gate
        carry_scr[...] += tile_cnt


def _route(logits, n_tok):
    R = logits.shape[0]
    tm = ROW_TILE
    blk = pl.BlockSpec((tm, LANES), lambda p, i: (i * p, 0))
    return pl.pallas_call(
        functools.partial(_route_kernel, n_tok=n_tok),
        grid=(2, R // tm),
        in_specs=[pl.BlockSpec((tm, LANES), lambda p, i: (i, 0))],
        out_specs=[blk, blk, pl.BlockSpec((1, LANES), lambda p, i: (0, 0))],
        out_shape=[jax.ShapeDtypeStruct((R, LANES), I32), jax.ShapeDtypeStruct((R, LANES), F32),
                   jax.ShapeDtypeStruct((1, LANES), F32)],
        scratch_shapes=[pltpu.VMEM((1, LANES), F32)] * 3,
        compiler_params=_cparams(("arbitrary", "arbitrary"), VMEM_LIMIT),
        name="route",
    )(logits)


def _dispatch_kernel(dest_ref, xn_ref, xs_ref, sem):
    def row_copy(r, k):
        return pltpu.make_async_copy(xn_ref.at[pl.ds(r, 1), :],
                                     xs_ref.at[pl.ds(dest_ref[r * TOP_K + k], 1), :], sem)

    def issue(r, carry):
        for k in range(TOP_K):
            row_copy(r, k).start()
        return carry

    lax.fori_loop(0, TOK_TILE, issue, 0)

    def drain(r, carry):
        for k in range(TOP_K):
            row_copy(r, k).wait()
        return carry

    lax.fori_loop(0, TOK_TILE, drain, 0)


def _dispatch(xn, dest_flat, n_tok, n_slots):
    D = xn.shape[1]
    return pl.pallas_call(
        _dispatch_kernel,
        grid=(n_tok // TOK_TILE,),
        in_specs=[pl.BlockSpec((TOK_TILE * TOP_K,), lambda i: (i,), memory_space=pltpu.SMEM),
                  pl.BlockSpec((TOK_TILE, D), lambda i: (i, 0))],
        out_specs=pl.BlockSpec(memory_space=pl.ANY),
        out_shape=jax.ShapeDtypeStruct((n_slots, D), F32),
        scratch_shapes=[pltpu.SemaphoreType.DMA],
        compiler_params=_cparams(("arbitrary",), VMEM_LIMIT),
        name="dispatch",
    )(dest_flat, xn)


def _expert_changed(be_ref, blk):
    prev = be_ref[jnp.maximum(blk - 1, 0)]
    return (blk == 0) | (be_ref[blk] != prev)


def _moe_up_kernel(be_ref, nu_ref, xs_ref, wg_ref, wu_ref, bg_ref, bu_ref, act_ref, wg_bf, wu_bf):
    blk = pl.program_id(1)
    used = blk < nu_ref[0]

    @pl.when(used & _expert_changed(be_ref, blk))
    def _():
        wg_bf[...] = wg_ref[...].astype(BF16)
        wu_bf[...] = wu_ref[...].astype(BF16)

    @pl.when(used)
    def _():
        xb = xs_ref[...].astype(BF16)
        gate = jnp.dot(xb, wg_bf[...], preferred_element_type=F32) + bg_ref[...]
        up = jnp.dot(xb, wu_bf[...], preferred_element_type=F32) + bu_ref[...]
        gate = jnp.minimum(gate, SWIGLU_LIMIT)
        up = jnp.clip(up, -SWIGLU_LIMIT, SWIGLU_LIMIT)
        act = gate * jax.nn.sigmoid(SWIGLU_ALPHA * gate) * (up + 1.0)
        act_ref[...] = act.astype(act_ref.dtype)


def _moe_down_kernel(be_ref, nu_ref, act_ref, wd_ref, bd_ref, y_ref, wd_bf):
    blk = pl.program_id(1)
    used = blk < nu_ref[0]

    @pl.when(used & _expert_changed(be_ref, blk))
    def _():
        wd_bf[...] = wd_ref[...].astype(BF16)

    @pl.when(used)
    def _():
        y_ref[...] = jnp.dot(act_ref[...], wd_bf[...], preferred_element_type=F32) + bd_ref[...]


def _moe(xs, blk_exp, n_used, w_gate, b_gate, w_up, b_up, w_down, b_down):
    P, D = xs.shape
    E, _, F = w_gate.shape
    n_blk = P // MOE_BLK
    fc = MOE_FC
    clamp = lambda blk, nu: jnp.minimum(blk, nu[0] - 1)
    rows = lambda w, col: pl.BlockSpec((MOE_BLK, w), col)
    act = pl.pallas_call(
        _moe_up_kernel,
        grid_spec=pltpu.PrefetchScalarGridSpec(
            num_scalar_prefetch=2,
            grid=(F // fc, n_blk),
            in_specs=[
                rows(D, lambda j, blk, be, nu: (clamp(blk, nu), 0)),
                pl.BlockSpec((None, D, fc), lambda j, blk, be, nu: (be[clamp(blk, nu)], 0, j)),
                pl.BlockSpec((None, D, fc), lambda j, blk, be, nu: (be[clamp(blk, nu)], 0, j)),
                pl.BlockSpec((None, 1, fc), lambda j, blk, be, nu: (be[clamp(blk, nu)], 0, j)),
                pl.BlockSpec((None, 1, fc), lambda j, blk, be, nu: (be[clamp(blk, nu)], 0, j)),
            ],
            out_specs=rows(fc, lambda j, blk, be, nu: (clamp(blk, nu), j)),
            scratch_shapes=[pltpu.VMEM((D, fc), BF16), pltpu.VMEM((D, fc), BF16)],
        ),
        out_shape=jax.ShapeDtypeStruct((P, F), BF16),
        compiler_params=_cparams(("arbitrary", "arbitrary"), VMEM_LIMIT),
        name="moe_up",
    )(blk_exp, n_used, xs, w_gate, w_up, b_gate.reshape(E, 1, F), b_up.reshape(E, 1, F))

    return pl.pallas_call(
        _moe_down_kernel,
        grid_spec=pltpu.PrefetchScalarGridSpec(
            num_scalar_prefetch=2,
            grid=(D // fc, n_blk),
            in_specs=[
                rows(F, lambda j, blk, be, nu: (clamp(blk, nu), 0)),
                pl.BlockSpec((None, F, fc), lambda j, blk, be, nu: (be[clamp(blk, nu)], 0, j)),
                pl.BlockSpec((None, 1, fc), lambda j, blk, be, nu: (be[clamp(blk, nu)], 0, j)),
            ],
            out_specs=rows(fc, lambda j, blk, be, nu: (clamp(blk, nu), j)),
            scratch_shapes=[pltpu.VMEM((F, fc), BF16)],
        ),
        out_shape=jax.ShapeDtypeStruct((P, D), F32),
        compiler_params=_cparams(("arbitrary", "arbitrary"), VMEM_LIMIT),
        name="moe_down",
    )(blk_exp, n_used, act, w_down, b_down.reshape(E, 1, D))


def _combine_kernel(dest_ref, h2_ref, gate_ref, g_ref, y_ref, o_ref, ybuf, sem):
    def row_copy(r, k):
        return pltpu.make_async_copy(y_ref.at[pl.ds(dest_ref[r * TOP_K + k], 1), :],
                                     ybuf.at[k, pl.ds(r, 1), :], sem)

    def issue(r, carry):
        for k in range(TOP_K):
            row_copy(r, k).start()
        return carry

    lax.fori_loop(0, TOK_TILE, issue, 0)

    def drain(r, carry):
        for k in range(TOP_K):
            row_copy(r, k).wait()
        return carry

    lax.fori_loop(0, TOK_TILE, drain, 0)
    gates = gate_ref[...]
    acc = h2_ref[...]
    for k in range(TOP_K):
        acc = acc + gates[:, k:k + 1] * ybuf[k]
    o_ref[...] = _rms(acc, g_ref[...])


def _combine(h2, gates, dest_flat, y, g, n_seq):
    D = h2.shape[1]
    return pl.pallas_call(
        _combine_kernel,
        grid=(n_seq // TOK_TILE,),
        in_specs=[pl.BlockSpec((TOK_TILE * TOP_K,), lambda i: (i,), memory_space=pltpu.SMEM),
                  pl.BlockSpec((TOK_TILE, D), lambda i: (i, 0)),
                  pl.BlockSpec((TOK_TILE, LANES), lambda i: (i, 0)),
                  pl.BlockSpec((1, D), lambda i: (0, 0)),
                  pl.BlockSpec(memory_space=pl.ANY)],
        out_specs=pl.BlockSpec((TOK_TILE, D), lambda i: (i, 0)),
        out_shape=jax.ShapeDtypeStruct((n_seq, D), F32),
        scratch_shapes=[pltpu.VMEM((TOP_K, TOK_TILE, D), F32), pltpu.SemaphoreType.DMA],
        compiler_params=_cparams(("arbitrary",), VMEM_LIMIT),
        name="combine_norm",
    )(dest_flat, h2, gates, g, y)


def _rope_tables(S, n_meta_rows):
    half = HEAD_DIM // 2
    inv_freq = ROPE_THETA ** (-jnp.arange(half, dtype=F32) / half)
    lane = np.arange(LANES)
    fidx = lane % half
    sign = np.where((lane % HEAD_DIM) < half, -1.0, 1.0).astype(np.float32)
    pos_seq = N_META_TOK + jnp.arange(S)
    tail = jnp.arange(ROW_TILE)
    pos_tail = jnp.where(tail < n_meta_rows, tail % N_META_TOK, 0)
    pos = jnp.concatenate([pos_seq, pos_tail]).astype(F32)
    ang = (pos[:, None] * inv_freq[None, :])[:, fidx]
    return jnp.cos(ang), jnp.sin(ang) * sign[None, :]


def kernel(x, meta_tokens, attn_norm_g, w_in, attn_sinks, lower_bound_logits, rec_norm_g,
           w_attn_proj, w_rec_proj, w_out, ffn_norm_g, w_router, b_router, w_gate, b_gate,
           w_up, b_up, w_down, b_down, final_norm_g):
    B, S, D = x.shape
    assert w_in.shape[0] == 1, "single-layer block"
    assert S % ROW_TILE == 0 and (B * S) % ROW_TILE == 0 and S % SEG == 0
    NS = B * S
    NM = B * N_META_TOK
    NT = NS + NM
    R = -(-NT // ROW_TILE) * ROW_TILE
    assert R - NS == ROW_TILE and NT % TOK_TILE == 0 and NS % TOK_TILE == 0

    h_rows = jnp.concatenate([x.reshape(NS, D), jnp.tile(meta_tokens.astype(x.dtype), (B, 1)),
                              jnp.zeros((R - NT, D), x.dtype)], axis=0)

    w0 = w_in[0]
    kw = N_KV_HEADS * HEAD_DIM
    wq, wk, wv = w0[:, :Q_W], w0[:, Q_W:Q_W + kw], w0[:, Q_W + kw:Q_W + 2 * kw]
    dup = lambda w: jnp.concatenate([w[:, :HEAD_DIM]] * 2 + [w[:, HEAD_DIM:]] * 2, axis=1)
    w_qkv = jnp.concatenate([wq, dup(wk), dup(wv)], axis=1).astype(BF16)
    w_rest = w0[:, Q_W + 2 * kw:].astype(BF16)

    cos_t, sin_t = _rope_tables(S, NM)
    u, qkv = _norm_qkv(h_rows, attn_norm_g[0].reshape(1, D), w_qkv, cos_t, sin_t,
                       NS // ROW_TILE, S // ROW_TILE)
    rest = _proj_rest(u, w_rest)

    attn = _attention(qkv, attn_sinks[0], B, S)
    rec = _hgrn(rest, lower_bound_logits, rec_norm_g[0].reshape(1, REC_DIM), B, S)

    wr_pad = jnp.zeros((D, LANES), F32).at[:, :N_EXPERTS].set(w_router[0])
    br_pad = jnp.full((1, LANES), NEG_INF, F32).at[0, :N_EXPERTS].set(b_router[0])
    h2, xn, logits = _mix(attn, rec, rest, h_rows, w_attn_proj[0].astype(BF16),
                          w_rec_proj[0].astype(BF16), w_out[0].astype(BF16),
                          ffn_norm_g[0].reshape(1, D), wr_pad, br_pad)

    dest, gates, counts = _route(logits, NT)
    cnt = counts[0, :N_EXPERTS].astype(I32)
    pad_end = jnp.cumsum(((cnt + MOE_BLK - 1) // MOE_BLK) * MOE_BLK)
    n_blk = -(-(NT * TOP_K + N_EXPERTS * (MOE_BLK - 1)) // MOE_BLK)
    n_used = (pad_end[-1] // MOE_BLK).astype(I32).reshape(1)
    blk_exp = jnp.minimum(jnp.searchsorted(pad_end, jnp.arange(n_blk) * MOE_BLK, side='right'),
                          N_EXPERTS - 1).astype(I32)
    dest_flat = dest[:NT, :TOP_K].reshape(NT * TOP_K)

    xs = _dispatch(xn, dest_flat, NT, n_blk * MOE_BLK)
    y = _moe(xs, blk_exp, n_used, w_gate[0], b_gate[0], w_up[0], b_up[0], w_down[0], b_down[0])
    out = _combine(h2, gates, dest_flat, y, final_norm_g.reshape(1, D), NS)
    return out.reshape(B, S, D)
```

```python
import functools

import jax
import jax.numpy as jnp
import numpy as np
from jax import lax
from jax.experimental import pallas as pl
from jax.experimental.pallas import tpu as pltpu

F32 = jnp.float32
BF16 = jnp.bfloat16
I32 = jnp.int32

N_META_TOK = 16
HEAD_DIM = 64
N_Q_HEADS = 16
N_KV_HEADS = 2
ATTN_BLK = 128
ROPE_THETA = 10000.0
N_REC_HEADS = 8
REC_DIM = 128
REC_CHUNK = 16
N_EXPERTS = 32
TOP_K = 4
SWIGLU_ALPHA = 1.702
SWIGLU_LIMIT = 7.0
NORM_EPS = 1e-5
NEG_INF = -1e30

LANES = 128
ROW_TILE = 512
MIX_TILE = 256
SEG = 256
MOE_BLK = 256
MOE_FC = 1024
TOK_TILE = 128
VMEM_LIMIT = 56 * 1024 * 1024

Q_W = N_Q_HEADS * HEAD_DIM
REC_W = N_REC_HEADS * REC_DIM


def _cparams(sem, vmem=None):
    return pltpu.CompilerParams(dimension_semantics=sem, vmem_limit_bytes=vmem)


def _rms(xf, g):
    return xf * lax.rsqrt(jnp.mean(xf * xf, axis=-1, keepdims=True) + NORM_EPS) * g


def _norm_qkv_kernel(x_ref, tail_ref, g_ref, w_ref, cos_ref, sin_ref, u_ref, qkv_ref, *, n_seq_tiles):
    h = jnp.where(pl.program_id(0) < n_seq_tiles, x_ref[...], tail_ref[...])
    u = _rms(h, g_ref[...]).astype(BF16)
    u_ref[...] = u
    p = jnp.dot(u, w_ref[...], preferred_element_type=F32)
    cos = cos_ref[...]
    sin = sin_ref[...]
    lane = lax.broadcasted_iota(I32, cos.shape, 1)
    first_half = (lane % HEAD_DIM) < (HEAD_DIM // 2)
    n_q = Q_W // LANES
    for c in range(n_q + 2):
        xs = p[:, c * LANES:(c + 1) * LANES]
        swapped = jnp.where(first_half, pltpu.roll(xs, LANES - HEAD_DIM // 2, 1),
                            pltpu.roll(xs, HEAD_DIM // 2, 1))
        r = xs * cos + swapped * sin
        if c < n_q:
            r = r * (HEAD_DIM ** -0.5)
        qkv_ref[:, c * LANES:(c + 1) * LANES] = r.astype(BF16)
    v0 = (n_q + 2) * LANES
    qkv_ref[:, v0:] = p[:, v0:].astype(BF16)


def _norm_qkv(x2d, tail_rows, g, w_qkv, cos_t, sin_t, tiles_per_seq):
    NS, D = x2d.shape
    R = NS + tail_rows.shape[0]
    W = w_qkv.shape[1]
    n_seq_tiles = NS // ROW_TILE

    def tab_map(i):
        return (jnp.where(i < n_seq_tiles, i % tiles_per_seq, tiles_per_seq), 0)

    return pl.pallas_call(
        functools.partial(_norm_qkv_kernel, n_seq_tiles=n_seq_tiles),
        grid=(R // ROW_TILE,),
        in_specs=[
            pl.BlockSpec((ROW_TILE, D), lambda i: (jnp.minimum(i, n_seq_tiles - 1), 0)),
            pl.BlockSpec((ROW_TILE, D), lambda i: (0, 0)),
            pl.BlockSpec((1, D), lambda i: (0, 0)),
            pl.BlockSpec((D, W), lambda i: (0, 0)),
            pl.BlockSpec((ROW_TILE, LANES), tab_map),
            pl.BlockSpec((ROW_TILE, LANES), tab_map),
        ],
        out_specs=[
            pl.BlockSpec((ROW_TILE, D), lambda i: (i, 0)),
            pl.BlockSpec((ROW_TILE, W), lambda i: (i, 0)),
        ],
        out_shape=[jax.ShapeDtypeStruct((R, D), BF16), jax.ShapeDtypeStruct((R, W), BF16)],
        compiler_params=_cparams(("parallel",), VMEM_LIMIT),
        name="norm_qkv",
    )(x2d, tail_rows, g, w_qkv, cos_t, sin_t)


def _matmul_kernel(x_ref, w_ref, o_ref):
    o_ref[...] = jnp.dot(x_ref[...], w_ref[...], preferred_element_type=F32).astype(o_ref.dtype)


def _proj_rest(u, w):
    R, D = u.shape
    N = w.shape[1]
    tn = 2048
    return pl.pallas_call(
        _matmul_kernel,
        grid=(N // tn, R // ROW_TILE),
        in_specs=[pl.BlockSpec((ROW_TILE, D), lambda j, i: (i, 0)),
                  pl.BlockSpec((D, tn), lambda j, i: (0, j))],
        out_specs=pl.BlockSpec((ROW_TILE, tn), lambda j, i: (i, j)),
        out_shape=jax.ShapeDtypeStruct((R, N), F32),
        compiler_params=_cparams(("parallel", "parallel"), VMEM_LIMIT),
        name="proj_rest",
    )(u, w)


def _attn_core(q_ref, k_groups, v_groups, mask, sink_ref, write):
    nkeys = k_groups[0].shape[0]
    lo_k = lax.broadcasted_iota(I32, (nkeys, LANES), 1) < HEAD_DIM
    rows = q_ref.shape[0]
    lo_q = lax.broadcasted_iota(I32, (rows, LANES), 1) < HEAD_DIM
    pairs_per_group = (N_Q_HEADS // N_KV_HEADS) // 2
    for g in range(N_KV_HEADS):
        kk = k_groups[g]
        vv = v_groups[g]
        zero_v = jnp.zeros_like(vv)
        v_half = (jnp.where(lo_k, vv, zero_v), jnp.where(lo_k, zero_v, vv))
        for j in range(pairs_per_group):
            pr = pairs_per_group * g + j
            qp = q_ref[:, pr * LANES:(pr + 1) * LANES]
            zero_q = jnp.zeros_like(qp)
            q_half = (jnp.where(lo_q, qp, zero_q), jnp.where(lo_q, zero_q, qp))
            acc = None
            for half in range(2):
                s = lax.dot_general(q_half[half], kk, (((1,), (1,)), ((), ())),
                                    preferred_element_type=F32)
                s = jnp.where(mask, s, NEG_INF)
                sk = sink_ref[2 * pr + half]
                m = jnp.maximum(jnp.max(s, axis=-1, keepdims=True), sk)
                e = jnp.exp(s - m)
                den = jnp.sum(e, axis=-1, keepdims=True) + jnp.exp(sk - m)
                prob = (e / den).astype(BF16)
                o = jnp.dot(prob, v_half[half], preferred_element_type=F32)
                acc = o if acc is None else acc + o
            write(pr, acc)


def _attn_seq_kernel(sink_ref, q_ref, kc0, kc1, vc0, vc1, kp0, kp1, vp0, vp1,
                     km0, km1, vm0, vm1, o_ref):
    n = pl.program_id(1)
    nk = N_META_TOK + 2 * ATTN_BLK
    ri = lax.broadcasted_iota(I32, (ATTN_BLK, nk), 0)
    ci = lax.broadcasted_iota(I32, (ATTN_BLK, nk), 1)
    no_prev = jnp.where(n > 0, 0, 2 * ATTN_BLK)
    prev_ok = (ci >= N_META_TOK) & (ci < N_META_TOK + ATTN_BLK) & (ci - N_META_TOK > ri + no_prev)
    cur_ok = (ci >= N_META_TOK + ATTN_BLK) & (ci - (N_META_TOK + ATTN_BLK) <= ri)
    mask = (ci < N_META_TOK) | prev_ok | cur_ok
    k_groups = [jnp.concatenate([km[...], kp[...], kc[...]], axis=0)
                for km, kp, kc in ((km0, kp0, kc0), (km1, kp1, kc1))]
    v_groups = [jnp.concatenate([vm[...], vp[...], vc[...]], axis=0)
                for vm, vp, vc in ((vm0, vp0, vc0), (vm1, vp1, vc1))]

    def write(pr, acc):
        o_ref[:, pr * LANES:(pr + 1) * LANES] = acc.astype(o_ref.dtype)

    _attn_core(q_ref, k_groups, v_groups, mask, sink_ref, write)


def _attn_meta_kernel(sink_ref, q_ref, k0, k1, v0, v1, o_ref):
    nm = q_ref.shape[0]
    ri = lax.broadcasted_iota(I32, (nm, nm), 0)
    ci = lax.broadcasted_iota(I32, (nm, nm), 1)
    same_batch = (ri // N_META_TOK) == (ci // N_META_TOK)
    mask = same_batch & ((ci % N_META_TOK) <= (ri % N_META_TOK))
    o_ref[...] = jnp.zeros_like(o_ref)

    def write(pr, acc):
        o_ref[0:nm, pr * LANES:(pr + 1) * LANES] = acc.astype(o_ref.dtype)

    _attn_core(q_ref, [k0[...], k1[...]], [v0[...], v1[...]], mask, sink_ref, write)


def _attention(qkv, sinks, B, S):
    R = qkv.shape[0]
    NS = B * S
    NM = B * N_META_TOK
    nb = S // ATTN_BLK
    qc = Q_W // LANES
    smem = pl.BlockSpec(memory_space=pltpu.SMEM)

    def kv_spec(col, prev):
        if prev:
            return pl.BlockSpec((ATTN_BLK, LANES), lambda b, n: (b * nb + jnp.maximum(n - 1, 0), col))
        return pl.BlockSpec((ATTN_BLK, LANES), lambda b, n: (b * nb + n, col))

    def meta_spec(col):
        return pl.BlockSpec((N_META_TOK, LANES), lambda b, n: (NS // N_META_TOK + b, col))

    in_specs = [smem, pl.BlockSpec((ATTN_BLK, Q_W), lambda b, n: (b * nb + n, 0))]
    in_specs += [kv_spec(qc, False), kv_spec(qc + 1, False), kv_spec(qc + 2, False), kv_spec(qc + 3, False)]
    in_specs += [kv_spec(qc, True), kv_spec(qc + 1, True), kv_spec(qc + 2, True), kv_spec(qc + 3, True)]
    in_specs += [meta_spec(qc), meta_spec(qc + 1), meta_spec(qc + 2), meta_spec(qc + 3)]
    attn = pl.pallas_call(
        _attn_seq_kernel,
        grid=(B, nb),
        in_specs=in_specs,
        out_specs=pl.BlockSpec((ATTN_BLK, Q_W), lambda b, n: (b * nb + n, 0)),
        out_shape=jax.ShapeDtypeStruct((NS, Q_W), BF16),
        compiler_params=_cparams(("parallel", "parallel"), VMEM_LIMIT),
        name="attn_seq",
    )(sinks, *([qkv] * 13))

    mb = NS // NM
    tail = R - NS
    blk = lambda col: pl.BlockSpec((NM, LANES), lambda i: (mb, col))
    attn_tail = pl.pallas_call(
        _attn_meta_kernel,
        grid=(1,),
        in_specs=[smem, pl.BlockSpec((NM, Q_W), lambda i: (mb, 0)),
                  blk(qc), blk(qc + 1), blk(qc + 2), blk(qc + 3)],
        out_specs=pl.BlockSpec((tail, Q_W), lambda i: (0, 0)),
        out_shape=jax.ShapeDtypeStruct((tail, Q_W), BF16),
        compiler_params=_cparams(("arbitrary",), VMEM_LIMIT),
        name="attn_meta",
    )(sinks, qkv, qkv, qkv, qkv, qkv)
    return attn, attn_tail


def _hgrn_prep_kernel(rq_ref, rf_ref, ri_ref, lbl_ref, qd_ref, kd_ref, ke_ref, v_ref, eb_ref):
    lbl = lbl_ref[...]
    e = jnp.exp(lbl - jnp.max(lbl, axis=0, keepdims=True))
    lb = e[0:1] / jnp.sum(e, axis=0, keepdims=True)
    f = lb + (1.0 - lb) * jax.nn.sigmoid(rf_ref[...])
    logf = jnp.log(f)
    k = 1.0 - f
    tm = logf.shape[0]
    pos = lax.broadcasted_iota(I32, logf.shape, 0) % REC_CHUNK
    b = logf
    suf = logf
    s = 1
    while s < REC_CHUNK:
        b = b + jnp.where(pos >= s, pltpu.roll(b, s, 0), 0.0)
        suf = suf + jnp.where(pos < REC_CHUNK - s, pltpu.roll(suf, tm - s, 0), 0.0)
        s *= 2
    eb = jnp.exp(b)
    eb_ref[...] = eb
    qd_ref[...] = (rq_ref[...] * eb).astype(BF16)
    kd_ref[...] = (k * jnp.exp(-b)).astype(BF16)
    ke_ref[...] = (k * jnp.exp(suf - logf)).astype(BF16)
    v_ref[...] = ri_ref[...].astype(BF16)


def _hgrn_prep(rest, lbl):
    R = rest.shape[0]
    W = REC_W
    col = lambda c: pl.BlockSpec((SEG, W), lambda i: (i, c))
    row_out = pl.BlockSpec((SEG, W), lambda i: (i, 0))
    return pl.pallas_call(
        _hgrn_prep_kernel,
        grid=(R // SEG,),
        in_specs=[col(0), col(1), col(2), pl.BlockSpec(lbl.shape, lambda i: (0, 0))],
        out_specs=[row_out] * 5,
        out_shape=[jax.ShapeDtypeStruct((R, W), BF16)] * 4 + [jax.ShapeDtypeStruct((R, W), F32)],
        compiler_params=_cparams(("parallel",), VMEM_LIMIT),
        name="hgrn_prep",
    )(rest, rest, rest, lbl)


def _hgrn_intra(qd, kd, v):
    n = qd.shape[0]
    ri = lax.broadcasted_iota(I32, (n, n), 0)
    ci = lax.broadcasted_iota(I32, (n, n), 1)
    keep = ((ri // REC_CHUNK) == (ci // REC_CHUNK)) & (ri >= ci)
    sc = lax.dot_general(qd, kd, (((1,), (1,)), ((), ())), preferred_element_type=F32)
    sc = jnp.where(keep, sc, 0.0)
    return jnp.dot(sc.astype(BF16), v, preferred_element_type=F32)


def _hgrn_kv_t(v, ke):
    return lax.dot_general(v, ke, (((0,), (0,)), ((), ())), preferred_element_type=F32)


def _hgrn_finish(o, g, norm_g):
    y = o * lax.rsqrt(jnp.mean(o * o, axis=-1, keepdims=True) + NORM_EPS) * norm_g
    return (y * (g * jax.nn.sigmoid(g))).astype(BF16)


def _hgrn_meta_kernel(qd_ref, kd_ref, ke_ref, v_ref, rg_ref, ng_ref, st_ref, rec_ref, *, nbatch):
    rec_ref[...] = jnp.zeros_like(rec_ref)
    nm = nbatch * REC_CHUNK
    for h in range(N_REC_HEADS):
        cols = slice(h * REC_DIM, (h + 1) * REC_DIM)
        o = _hgrn_intra(qd_ref[0:nm, cols], kd_ref[0:nm, cols], v_ref[0:nm, cols])
        rec_ref[0:nm, cols] = _hgrn_finish(o, rg_ref[0:nm, cols], ng_ref[...])
        for b in range(nbatch):
            rows = slice(b * REC_CHUNK, (b + 1) * REC_CHUNK)
            st_ref[b, h] = _hgrn_kv_t(v_ref[rows, cols], ke_ref[rows, cols])


def _hgrn_scan_kernel(qd_ref, kd_ref, ke_ref, v_ref, eb_ref, rg_ref, ng_ref, st0_ref,
                      rec_ref, st_scr, o_scr):
    @pl.when(pl.program_id(1) == 0)
    def _():
        st_scr[...] = st0_ref[0]

    for h in range(N_REC_HEADS):
        cols = slice(h * REC_DIM, (h + 1) * REC_DIM)
        o_scr[:, cols] = _hgrn_intra(qd_ref[:, cols], kd_ref[:, cols], v_ref[:, cols])

    def chunk(c, carry):
        r0 = pl.multiple_of(c * REC_CHUNK, REC_CHUNK)
        rows = pl.ds(r0, REC_CHUNK)
        for h in range(N_REC_HEADS):
            cols = slice(h * REC_DIM, (h + 1) * REC_DIM)
            last8 = eb_ref[pl.ds(pl.multiple_of(r0 + REC_CHUNK - 8, 8), 8), cols]
            dec = last8[7:8]
            st = st_scr[h]
            qd = qd_ref[rows, cols]
            o_scr[rows, cols] += lax.dot_general(qd, st.astype(BF16), (((1,), (1,)), ((), ())),
                                                 preferred_element_type=F32)
            st_scr[h] = st * dec + _hgrn_kv_t(v_ref[rows, cols], ke_ref[rows, cols])
        return carry

    lax.fori_loop(0, SEG // REC_CHUNK, chunk, 0, unroll=2)
    for h in range(N_REC_HEADS):
        cols = slice(h * REC_DIM, (h + 1) * REC_DIM)
        rec_ref[:, cols] = _hgrn_finish(o_scr[:, cols], rg_ref[:, cols], ng_ref[...])


def _hgrn(rest, lbl, norm_g, B, S):
    R = rest.shape[0]
    NS = B * S
    W = REC_W
    tail = R - NS
    tb = NS // tail
    qd, kd, ke, v, eb = _hgrn_prep(rest, lbl)

    tail_spec = pl.BlockSpec((tail, W), lambda i: (tb, 0))
    state, rec_tail = pl.pallas_call(
        functools.partial(_hgrn_meta_kernel, nbatch=B),
        grid=(1,),
        in_specs=[tail_spec, tail_spec, tail_spec, tail_spec,
                  pl.BlockSpec((tail, W), lambda i: (tb, 3)),
                  pl.BlockSpec((1, REC_DIM), lambda i: (0, 0))],
        out_specs=[pl.BlockSpec((B, N_REC_HEADS, REC_DIM, REC_DIM), lambda i: (0, 0, 0, 0)),
                   pl.BlockSpec((tail, W), lambda i: (0, 0))],
        out_shape=[jax.ShapeDtypeStruct((B, N_REC_HEADS, REC_DIM, REC_DIM), F32),
                   jax.ShapeDtypeStruct((tail, W), BF16)],
        compiler_params=_cparams(("arbitrary",), VMEM_LIMIT),
        name="hgrn_meta",
    )(qd, kd, ke, v, rest, norm_g)

    ns = S // SEG
    seg = pl.BlockSpec((SEG, W), lambda b, s: (b * ns + s, 0))
    rec = pl.pallas_call(
        _hgrn_scan_kernel,
        grid=(B, ns),
        in_specs=[seg, seg, seg, seg, seg,
                  pl.BlockSpec((SEG, W), lambda b, s: (b * ns + s, 3)),
                  pl.BlockSpec((1, REC_DIM), lambda b, s: (0, 0)),
                  pl.BlockSpec((1, N_REC_HEADS, REC_DIM, REC_DIM), lambda b, s: (b, 0, 0, 0))],
        out_specs=seg,
        out_shape=jax.ShapeDtypeStruct((NS, W), BF16),
        scratch_shapes=[pltpu.VMEM((N_REC_HEADS, REC_DIM, REC_DIM), F32),
                        pltpu.VMEM((SEG, W), F32)],
        compiler_params=_cparams(("parallel", "arbitrary"), VMEM_LIMIT),
        name="hgrn_scan",
    )(qd, kd, ke, v, eb, rest, norm_g, state)
    return rec, rec_tail


def _mix_kernel(attn_ref, attn_t_ref, rec_ref, rec_t_ref, ga_ref, gr_ref, x_ref, tail_ref,
                wa_ref, wr_ref, wo_ref, g_ref, wrt_ref, brt_ref, h2_ref, xn_ref, lg_ref, *, n_seq_tiles):
    is_seq = pl.program_id(0) < n_seq_tiles
    attn = jnp.where(is_seq, attn_ref[...], attn_t_ref[...])
    rec = jnp.where(is_seq, rec_ref[...], rec_t_ref[...])
    h = jnp.where(is_seq, x_ref[...], tail_ref[...])
    a = jnp.dot(attn, wa_ref[...], preferred_element_type=F32)
    r = jnp.dot(rec, wr_ref[...], preferred_element_type=F32)
    mixed = jax.nn.sigmoid(ga_ref[...]) * a + jax.nn.sigmoid(gr_ref[...]) * r
    h2 = h + jnp.dot(mixed.astype(BF16), wo_ref[...], preferred_element_type=F32)
    h2_ref[...] = h2
    xn = _rms(h2, g_ref[...])
    xn_ref[...] = xn
    xn_hi = xn.astype(BF16)
    xn_lo = (xn - xn_hi.astype(F32)).astype(BF16)
    w2 = wrt_ref[...]
    p_hi = jnp.dot(xn_hi, w2, preferred_element_type=F32)
    p_lo = jnp.dot(xn_lo, w2[:, :LANES], preferred_element_type=F32)
    lg_ref[...] = p_hi[:, :LANES] + p_hi[:, LANES:] + p_lo + brt_ref[...]


def _mix(attn, attn_tail, rec, rec_tail, rest, x2d, tail_rows, wa, wr, wo, g, w_router, b_router):
    NS, D = x2d.shape
    R = NS + tail_rows.shape[0]
    tm = MIX_TILE
    nst = NS // tm
    const = lambda shape: pl.BlockSpec(shape, lambda i: (0, 0), pipeline_mode=pl.Buffered(1))
    row = lambda w: pl.BlockSpec((tm, w), lambda i: (i, 0))
    seq = lambda w: pl.BlockSpec((tm, w), lambda i: (jnp.minimum(i, nst - 1), 0))
    tl = lambda w: pl.BlockSpec((tm, w), lambda i: (jnp.maximum(i - nst, 0), 0))
    return pl.pallas_call(
        functools.partial(_mix_kernel, n_seq_tiles=nst),
        grid=(R // tm,),
        in_specs=[seq(Q_W), tl(Q_W), seq(REC_W), tl(REC_W),
                  pl.BlockSpec((tm, D), lambda i: (i, 2)), pl.BlockSpec((tm, D), lambda i: (i, 3)),
                  seq(D), tl(D), const(wa.shape), const(wr.shape), const(wo.shape), const(g.shape),
                  const(w_router.shape), const(b_router.shape)],
        out_specs=[row(D), row(D), row(LANES)],
        out_shape=[jax.ShapeDtypeStruct((R, D), F32), jax.ShapeDtypeStruct((R, D), F32),
                   jax.ShapeDtypeStruct((R, LANES), F32)],
        compiler_params=_cparams(("parallel",), VMEM_LIMIT),
        name="mix_outproj",
    )(attn, attn_tail, rec, rec_tail, rest, rest, x2d, tail_rows, wa, wr, wo, g, w_router, b_router)


def _route_kernel(lg_ref, dest_ref, gate_ref, cnt_ref, cnt_scr, carry_scr, start_scr, *, n_tok):
    ph = pl.program_id(0)
    i = pl.program_id(1)
    tm = lg_ref.shape[0]

    @pl.when((ph == 0) & (i == 0))
    def _():
        cnt_scr[...] = jnp.zeros_like(cnt_scr)

    lane = lax.broadcasted_iota(I32, (tm, LANES), 1)
    valid = (i * tm + lax.broadcasted_iota(I32, (tm, LANES), 0)) < n_tok
    work = lg_ref[...]
    onehots, vals = [], []
    for _ in range(TOP_K):
        m = jnp.max(work, axis=-1, keepdims=True)
        idx = jnp.min(jnp.where(work == m, lane, LANES), axis=-1, keepdims=True)
        oh = lane == idx
        onehots.append(oh)
        vals.append(m)
        work = jnp.where(oh, -jnp.inf, work)
    multi = jnp.zeros((tm, LANES), F32)
    for oh in onehots:
        multi = multi + jnp.where(oh & valid, 1.0, 0.0)
    tile_cnt = jnp.sum(multi, axis=0, keepdims=True)

    @pl.when(ph == 0)
    def _():
        cnt_scr[...] += tile_cnt

    @pl.when(ph == 1)
    def _():
        @pl.when(i == 0)
        def _():
            c = cnt_scr[...]
            padded = jnp.ceil(c * (1.0 / MOE_BLK)) * MOE_BLK
            before = (lax.broadcasted_iota(I32, (LANES, LANES), 0)
                      < lax.broadcasted_iota(I32, (LANES, LANES), 1))
            start = jnp.dot(jnp.broadcast_to(padded, (8, LANES)), jnp.where(before, 1.0, 0.0),
                            preferred_element_type=F32, precision=lax.Precision.HIGHEST)
            start_scr[...] = start[0:1]
            carry_scr[...] = jnp.zeros_like(carry_scr)
            cnt_ref[...] = c

        earlier = (lax.broadcasted_iota(I32, (tm, tm), 1) < lax.broadcasted_iota(I32, (tm, tm), 0))
        prefix = jnp.dot(jnp.where(earlier, 1.0, 0.0).astype(BF16), multi.astype(BF16),
                         preferred_element_type=F32)
        base = prefix + carry_scr[...] + start_scr[...]
        den = jnp.zeros_like(vals[0])
        for v in vals:
            den = den + jnp.exp(v - vals[0])
        dest = jnp.zeros((tm, LANES), F32)
        gate = jnp.zeros((tm, LANES), F32)
        for k in range(TOP_K):
            d_k = jnp.sum(jnp.where(onehots[k], base, 0.0), axis=-1, keepdims=True)
            dest = jnp.where(lane == k, d_k, dest)
            gate = jnp.where(lane == k, jnp.exp(vals[k] - vals[0]) / den, gate)
        dest_ref[...] = dest.astype(I32)
        gate_ref[...] = gate
        carry_scr[...] += tile_cnt


def _route(logits, n_tok):
    R = logits.shape[0]
    tm = ROW_TILE
    blk = pl.BlockSpec((tm, LANES), lambda p, i: (i * p, 0))
    return pl.pallas_call(
        functools.partial(_route_kernel, n_tok=n_tok),
        grid=(2, R // tm),
        in_specs=[pl.BlockSpec((tm, LANES), lambda p, i: (i, 0))],
        out_specs=[blk, blk, pl.BlockSpec((1, LANES), lambda p, i: (0, 0))],
        out_shape=[jax.ShapeDtypeStruct((R, LANES), I32), jax.ShapeDtypeStruct((R, LANES), F32),
                   jax.ShapeDtypeStruct((1, LANES), F32)],
        scratch_shapes=[pltpu.VMEM((1, LANES), F32)] * 3,
        compiler_params=_cparams(("arbitrary", "arbitrary"), VMEM_LIMIT),
        name="route",
    )(logits)


def _dispatch_kernel(dest_ref, xn_ref, xs_ref, sem):
    def row_copy(r, k):
        return pltpu.make_async_copy(xn_ref.at[pl.ds(r, 1), :],
                                     xs_ref.at[pl.ds(dest_ref[r * TOP_K + k], 1), :], sem)

    def issue(r, carry):
        for k in range(TOP_K):
            row_copy(r, k).start()
        return carry

    lax.fori_loop(0, TOK_TILE, issue, 0)
    for _ in range(TOP_K):
        pltpu.make_async_copy(xn_ref, xs_ref.at[pl.ds(0, TOK_TILE), :], sem).wait()


def _dispatch(xn, dest_flat, n_tok, n_slots):
    D = xn.shape[1]
    return pl.pallas_call(
        _dispatch_kernel,
        grid=(n_tok // TOK_TILE,),
        in_specs=[pl.BlockSpec((TOK_TILE * TOP_K,), lambda i: (i,), memory_space=pltpu.SMEM),
                  pl.BlockSpec((TOK_TILE, D), lambda i: (i, 0))],
        out_specs=pl.BlockSpec(memory_space=pl.ANY),
        out_shape=jax.ShapeDtypeStruct((n_slots, D), F32),
        scratch_shapes=[pltpu.SemaphoreType.DMA],
        compiler_params=_cparams(("arbitrary",), VMEM_LIMIT),
        name="dispatch",
    )(dest_flat, xn)


def _expert_changed(be_ref, blk):
    prev = be_ref[jnp.maximum(blk - 1, 0)]
    return (blk == 0) | (be_ref[blk] != prev)


def _moe_up_kernel(be_ref, nu_ref, xs_ref, wg_ref, wu_ref, bg_ref, bu_ref, act_ref, wg_bf, wu_bf):
    blk = pl.program_id(1)
    used = blk < nu_ref[0]

    @pl.when(used & _expert_changed(be_ref, blk))
    def _():
        wg_bf[...] = wg_ref[...].astype(BF16)
        wu_bf[...] = wu_ref[...].astype(BF16)

    @pl.when(used)
    def _():
        xb = xs_ref[...].astype(BF16)
        gate = jnp.dot(xb, wg_bf[...], preferred_element_type=F32) + bg_ref[...]
        up = jnp.dot(xb, wu_bf[...], preferred_element_type=F32) + bu_ref[...]
        gate = jnp.minimum(gate, SWIGLU_LIMIT)
        up = jnp.clip(up, -SWIGLU_LIMIT, SWIGLU_LIMIT)
        act = gate * jax.nn.sigmoid(SWIGLU_ALPHA * gate) * (up + 1.0)
        act_ref[...] = act.astype(act_ref.dtype)


def _moe_down_kernel(be_ref, nu_ref, act_ref, wd_ref, bd_ref, y_ref, wd_bf):
    blk = pl.program_id(1)
    used = blk < nu_ref[0]

    @pl.when(used & _expert_changed(be_ref, blk))
    def _():
        wd_bf[...] = wd_ref[...].astype(BF16)

    @pl.when(used)
    def _():
        y_ref[...] = jnp.dot(act_ref[...], wd_bf[...], preferred_element_type=F32) + bd_ref[...]


def _moe(xs, blk_exp, n_used, w_gate, b_gate, w_up, b_up, w_down, b_down):
    P, D = xs.shape
    E, _, F = w_gate.shape
    n_blk = P // MOE_BLK
    fc = MOE_FC
    clamp = lambda blk, nu: jnp.minimum(blk, nu[0] - 1)
    rows = lambda w, col: pl.BlockSpec((MOE_BLK, w), col)
    act = pl.pallas_call(
        _moe_up_kernel,
        grid_spec=pltpu.PrefetchScalarGridSpec(
            num_scalar_prefetch=2,
            grid=(F // fc, n_blk),
            in_specs=[
                rows(D, lambda j, blk, be, nu: (clamp(blk, nu), 0)),
                pl.BlockSpec((None, D, fc), lambda j, blk, be, nu: (be[clamp(blk, nu)], 0, j)),
                pl.BlockSpec((None, D, fc), lambda j, blk, be, nu: (be[clamp(blk, nu)], 0, j)),
                pl.BlockSpec((None, 1, fc), lambda j, blk, be, nu: (be[clamp(blk, nu)], 0, j)),
                pl.BlockSpec((None, 1, fc), lambda j, blk, be, nu: (be[clamp(blk, nu)], 0, j)),
            ],
            out_specs=rows(fc, lambda j, blk, be, nu: (clamp(blk, nu), j)),
            scratch_shapes=[pltpu.VMEM((D, fc), BF16), pltpu.VMEM((D, fc), BF16)],
        ),
        out_shape=jax.ShapeDtypeStruct((P, F), BF16),
        compiler_params=_cparams(("arbitrary", "arbitrary"), VMEM_LIMIT),
        name="moe_up",
    )(blk_exp, n_used, xs, w_gate, w_up, b_gate.reshape(E, 1, F), b_up.reshape(E, 1, F))

    return pl.pallas_call(
        _moe_down_kernel,
        grid_spec=pltpu.PrefetchScalarGridSpec(
            num_scalar_prefetch=2,
            grid=(D // fc, n_blk),
            in_specs=[
                rows(F, lambda j, blk, be, nu: (clamp(blk, nu), 0)),
                pl.BlockSpec((None, F, fc), lambda j, blk, be, nu: (be[clamp(blk, nu)], 0, j)),
                pl.BlockSpec((None, 1, fc), lambda j, blk, be, nu: (be[clamp(blk, nu)], 0, j)),
            ],
            out_specs=rows(fc, lambda j, blk, be, nu: (clamp(blk, nu), j)),
            scratch_shapes=[pltpu.VMEM((F, fc), BF16)],
        ),
        out_shape=jax.ShapeDtypeStruct((P, D), F32),
        compiler_params=_cparams(("arbitrary", "arbitrary"), VMEM_LIMIT),
        name="moe_down",
    )(blk_exp, n_used, act, w_down, b_down.reshape(E, 1, D))


def _combine_kernel(dest_ref, dnext_ref, h2_ref, gate_ref, g_ref, y_ref, o_ref, ybuf, sem):
    i = pl.program_id(0)
    slot = i % 2

    def issue(dref, s):
        def body(r, carry):
            for k in range(TOP_K):
                pltpu.make_async_copy(y_ref.at[pl.ds(dref[r * TOP_K + k], 1), :],
                                      ybuf.at[s, k, pl.ds(r, 1), :], sem.at[s]).start()
            return carry

        lax.fori_loop(0, TOK_TILE, body, 0)

    @pl.when(i == 0)
    def _():
        issue(dest_ref, 0)

    @pl.when(i + 1 < pl.num_programs(0))
    def _():
        issue(dnext_ref, 1 - slot)

    for k in range(TOP_K):
        pltpu.make_async_copy(y_ref.at[pl.ds(0, TOK_TILE), :], ybuf.at[slot, k], sem.at[slot]).wait()
    gates = gate_ref[...]
    acc = h2_ref[...]
    for k in range(TOP_K):
        acc = acc + gates[:, k:k + 1] * ybuf[slot, k]
    o_ref[...] = _rms(acc, g_ref[...])


def _combine(h2, gates, dest_flat, y, g, n_seq):
    D = h2.shape[1]
    n_tiles = n_seq // TOK_TILE
    return pl.pallas_call(
        _combine_kernel,
        grid=(n_tiles,),
        in_specs=[pl.BlockSpec((TOK_TILE * TOP_K,), lambda i: (i,), memory_space=pltpu.SMEM),
                  pl.BlockSpec((TOK_TILE * TOP_K,), lambda i: (jnp.minimum(i + 1, n_tiles - 1),),
                               memory_space=pltpu.SMEM),
                  pl.BlockSpec((TOK_TILE, D), lambda i: (i, 0)),
                  pl.BlockSpec((TOK_TILE, LANES), lambda i: (i, 0)),
                  pl.BlockSpec((1, D), lambda i: (0, 0)),
                  pl.BlockSpec(memory_space=pl.ANY)],
        out_specs=pl.BlockSpec((TOK_TILE, D), lambda i: (i, 0)),
        out_shape=jax.ShapeDtypeStruct((n_seq, D), F32),
        scratch_shapes=[pltpu.VMEM((2, TOP_K, TOK_TILE, D), F32), pltpu.SemaphoreType.DMA((2,))],
        compiler_params=_cparams(("arbitrary",), VMEM_LIMIT),
        name="combine_norm",
    )(dest_flat, dest_flat, h2, gates, g, y)


def _rope_tables(S, n_meta_rows):
    half = HEAD_DIM // 2
    inv_freq = ROPE_THETA ** (-jnp.arange(half, dtype=F32) / half)
    lane = np.arange(LANES)
    fidx = lane % half
    sign = np.where((lane % HEAD_DIM) < half, -1.0, 1.0).astype(np.float32)
    pos_seq = N_META_TOK + jnp.arange(S)
    tail = jnp.arange(ROW_TILE)
    pos_tail = jnp.where(tail < n_meta_rows, tail % N_META_TOK, 0)
    pos = jnp.concatenate([pos_seq, pos_tail]).astype(F32)
    ang = (pos[:, None] * inv_freq[None, :])[:, fidx]
    return jnp.cos(ang), jnp.sin(ang) * sign[None, :]


def kernel(x, meta_tokens, attn_norm_g, w_in, attn_sinks, lower_bound_logits, rec_norm_g,
           w_attn_proj, w_rec_proj, w_out, ffn_norm_g, w_router, b_router, w_gate, b_gate,
           w_up, b_up, w_down, b_down, final_norm_g):
    B, S, D = x.shape
    assert w_in.shape[0] == 1, "single-layer block"
    assert S % ROW_TILE == 0 and (B * S) % ROW_TILE == 0 and S % SEG == 0
    NS = B * S
    NM = B * N_META_TOK
    NT = NS + NM
    R = -(-NT // ROW_TILE) * ROW_TILE
    assert R - NS == ROW_TILE and NT % TOK_TILE == 0 and NS % TOK_TILE == 0

    x2d = x.reshape(NS, D)
    tail_rows = jnp.concatenate([jnp.tile(meta_tokens.astype(x.dtype), (B, 1)),
                                 jnp.zeros((R - NT, D), x.dtype)], axis=0)

    w0 = w_in[0]
    kw = N_KV_HEADS * HEAD_DIM
    wq, wk, wv = w0[:, :Q_W], w0[:, Q_W:Q_W + kw], w0[:, Q_W + kw:Q_W + 2 * kw]
    dup = lambda w: jnp.concatenate([w[:, :HEAD_DIM]] * 2 + [w[:, HEAD_DIM:]] * 2, axis=1)
    w_qkv = jnp.concatenate([wq, dup(wk), dup(wv)], axis=1).astype(BF16)
    w_rest = w0[:, Q_W + 2 * kw:].astype(BF16)

    cos_t, sin_t = _rope_tables(S, NM)
    u, qkv = _norm_qkv(x2d, tail_rows, attn_norm_g[0].reshape(1, D), w_qkv, cos_t, sin_t,
                       S // ROW_TILE)
    rest = _proj_rest(u, w_rest)

    attn, attn_tail = _attention(qkv, attn_sinks[0], B, S)
    rec, rec_tail = _hgrn(rest, lower_bound_logits, rec_norm_g[0].reshape(1, REC_DIM), B, S)

    wr_hi = w_router[0].astype(BF16)
    wr_lo = (w_router[0] - wr_hi.astype(F32)).astype(BF16)
    lane_pad = ((0, 0), (0, LANES - N_EXPERTS))
    wr_pad = jnp.concatenate([jnp.pad(wr_hi, lane_pad), jnp.pad(wr_lo, lane_pad)], axis=1)
    br_pad = jnp.pad(b_router[0].reshape(1, N_EXPERTS), ((0, 0), (0, LANES - N_EXPERTS)),
                     constant_values=NEG_INF)
    h2, xn, logits = _mix(attn, attn_tail, rec, rec_tail, rest, x2d, tail_rows,
                          w_attn_proj[0].astype(BF16), w_rec_proj[0].astype(BF16),
                          w_out[0].astype(BF16), ffn_norm_g[0].reshape(1, D), wr_pad, br_pad)

    dest, gates, counts = _route(logits, NT)
    cnt = counts[0, :N_EXPERTS].astype(I32)
    pad_end = jnp.cumsum(((cnt + MOE_BLK - 1) // MOE_BLK) * MOE_BLK)
    n_blk = -(-(NT * TOP_K + N_EXPERTS * (MOE_BLK - 1)) // MOE_BLK)
    n_used = (pad_end[-1] // MOE_BLK).astype(I32).reshape(1)
    blk_start = jnp.arange(n_blk, dtype=I32) * MOE_BLK
    blk_exp = jnp.minimum(jnp.sum((pad_end[None, :] <= blk_start[:, None]).astype(I32), axis=1),
                          N_EXPERTS - 1)
    dest_flat = dest[:NT, :TOP_K].reshape(NT * TOP_K)

    xs = _dispatch(xn, dest_flat, NT, n_blk * MOE_BLK)
    y = _moe(xs, blk_exp, n_used, w_gate[0], b_gate[0], w_up[0], b_up[0], w_down[0], b_down[0])
    out = _combine(h2, gates, dest_flat, y, final_norm_g.reshape(1, D), NS)
    return out.reshape(B, S, D)
```

```python
import functools

import jax
import jax.numpy as jnp
import numpy as np
from jax import lax
from jax.experimental import pallas as pl
from jax.experimental.pallas import tpu as pltpu

F32 = jnp.float32
BF16 = jnp.bfloat16
I32 = jnp.int32

N_META_TOK = 16
HEAD_DIM = 64
N_Q_HEADS = 16
N_KV_HEADS = 2
ATTN_BLK = 128
ROPE_THETA = 10000.0
N_REC_HEADS = 8
REC_DIM = 128
REC_CHUNK = 16
N_EXPERTS = 32
TOP_K = 4
SWIGLU_ALPHA = 1.702
SWIGLU_LIMIT = 7.0
NORM_EPS = 1e-5
NEG_INF = -1e30

LANES = 128
ROW_TILE = 512
MIX_TILE = 256
SEG = 256
MOE_BLK = 256
MOE_UNIT_UP = 128
MOE_UNIT_DOWN = 256
TOK_TILE = 128
VMEM_LIMIT = 56 * 1024 * 1024

Q_W = N_Q_HEADS * HEAD_DIM
REC_W = N_REC_HEADS * REC_DIM


def _cparams(sem, vmem=None, **kw):
    return pltpu.CompilerParams(dimension_semantics=sem, vmem_limit_bytes=vmem, **kw)


def _rms(xf, g):
    return xf * lax.rsqrt(jnp.mean(xf * xf, axis=-1, keepdims=True) + NORM_EPS) * g


def _norm_qkv_kernel(x_ref, tail_ref, g_ref, w_ref, cos_ref, sin_ref, u_ref, qkv_ref, *, n_seq_tiles):
    h = jnp.where(pl.program_id(0) < n_seq_tiles, x_ref[...], tail_ref[...])
    u = _rms(h, g_ref[...]).astype(BF16)
    u_ref[...] = u
    p = jnp.dot(u, w_ref[...], preferred_element_type=F32)
    cos = cos_ref[...]
    sin = sin_ref[...]
    lane = lax.broadcasted_iota(I32, cos.shape, 1)
    first_half = (lane % HEAD_DIM) < (HEAD_DIM // 2)
    n_q = Q_W // LANES
    for c in range(n_q + 2):
        xs = p[:, c * LANES:(c + 1) * LANES]
        swapped = jnp.where(first_half, pltpu.roll(xs, LANES - HEAD_DIM // 2, 1),
                            pltpu.roll(xs, HEAD_DIM // 2, 1))
        r = xs * cos + swapped * sin
        if c < n_q:
            r = r * (HEAD_DIM ** -0.5)
        qkv_ref[:, c * LANES:(c + 1) * LANES] = r.astype(BF16)
    v0 = (n_q + 2) * LANES
    qkv_ref[:, v0:] = p[:, v0:].astype(BF16)


def _norm_qkv(x2d, tail_rows, g, w_qkv, cos_t, sin_t, tiles_per_seq):
    NS, D = x2d.shape
    R = NS + tail_rows.shape[0]
    W = w_qkv.shape[1]
    n_seq_tiles = NS // ROW_TILE

    def tab_map(i):
        return (jnp.where(i < n_seq_tiles, i % tiles_per_seq, tiles_per_seq), 0)

    return pl.pallas_call(
        functools.partial(_norm_qkv_kernel, n_seq_tiles=n_seq_tiles),
        grid=(R // ROW_TILE,),
        in_specs=[
            pl.BlockSpec((ROW_TILE, D), lambda i: (jnp.minimum(i, n_seq_tiles - 1), 0)),
            pl.BlockSpec((ROW_TILE, D), lambda i: (0, 0)),
            pl.BlockSpec((1, D), lambda i: (0, 0)),
            pl.BlockSpec((D, W), lambda i: (0, 0)),
            pl.BlockSpec((ROW_TILE, LANES), tab_map),
            pl.BlockSpec((ROW_TILE, LANES), tab_map),
        ],
        out_specs=[
            pl.BlockSpec((ROW_TILE, D), lambda i: (i, 0)),
            pl.BlockSpec((ROW_TILE, W), lambda i: (i, 0)),
        ],
        out_shape=[jax.ShapeDtypeStruct((R, D), BF16), jax.ShapeDtypeStruct((R, W), BF16)],
        compiler_params=_cparams(("parallel",), VMEM_LIMIT),
        name="norm_qkv",
    )(x2d, tail_rows, g, w_qkv, cos_t, sin_t)


def _matmul_kernel(x_ref, w_ref, o_ref):
    o_ref[...] = jnp.dot(x_ref[...], w_ref[...], preferred_element_type=F32).astype(o_ref.dtype)


def _proj_rest(u, w):
    R, D = u.shape
    N = w.shape[1]
    tn = 2048
    return pl.pallas_call(
        _matmul_kernel,
        grid=(N // tn, R // ROW_TILE),
        in_specs=[pl.BlockSpec((ROW_TILE, D), lambda j, i: (i, 0)),
                  pl.BlockSpec((D, tn), lambda j, i: (0, j))],
        out_specs=pl.BlockSpec((ROW_TILE, tn), lambda j, i: (i, j)),
        out_shape=jax.ShapeDtypeStruct((R, N), F32),
        compiler_params=_cparams(("parallel", "parallel"), VMEM_LIMIT),
        name="proj_rest",
    )(u, w)


def _attn_core(q_ref, k_groups, v_groups, mask, sink_ref, write):
    nkeys = k_groups[0].shape[0]
    lo_k = lax.broadcasted_iota(I32, (nkeys, LANES), 1) < HEAD_DIM
    rows = q_ref.shape[0]
    lo_q = lax.broadcasted_iota(I32, (rows, LANES), 1) < HEAD_DIM
    pairs_per_group = (N_Q_HEADS // N_KV_HEADS) // 2
    for g in range(N_KV_HEADS):
        kk = k_groups[g]
        vv = v_groups[g]
        zero_v = jnp.zeros_like(vv)
        v_half = (jnp.where(lo_k, vv, zero_v), jnp.where(lo_k, zero_v, vv))
        for j in range(pairs_per_group):
            pr = pairs_per_group * g + j
            qp = q_ref[:, pr * LANES:(pr + 1) * LANES]
            zero_q = jnp.zeros_like(qp)
            q_half = (jnp.where(lo_q, qp, zero_q), jnp.where(lo_q, zero_q, qp))
            acc = None
            for half in range(2):
                s = lax.dot_general(q_half[half], kk, (((1,), (1,)), ((), ())),
                                    preferred_element_type=F32)
                s = jnp.where(mask, s, NEG_INF)
                sk = sink_ref[2 * pr + half]
                m = jnp.maximum(jnp.max(s, axis=-1, keepdims=True), sk)
                e = jnp.exp(s - m)
                den = jnp.sum(e, axis=-1, keepdims=True) + jnp.exp(sk - m)
                prob = (e / den).astype(BF16)
                o = jnp.dot(prob, v_half[half], preferred_element_type=F32)
                acc = o if acc is None else acc + o
            write(pr, acc)


def _attn_seq_kernel(sink_ref, q_ref, kc0, kc1, vc0, vc1, kp0, kp1, vp0, vp1,
                     km0, km1, vm0, vm1, o_ref):
    n = pl.program_id(1)
    nk = N_META_TOK + 2 * ATTN_BLK
    ri = lax.broadcasted_iota(I32, (ATTN_BLK, nk), 0)
    ci = lax.broadcasted_iota(I32, (ATTN_BLK, nk), 1)
    no_prev = jnp.where(n > 0, 0, 2 * ATTN_BLK)
    prev_ok = (ci >= N_META_TOK) & (ci < N_META_TOK + ATTN_BLK) & (ci - N_META_TOK > ri + no_prev)
    cur_ok = (ci >= N_META_TOK + ATTN_BLK) & (ci - (N_META_TOK + ATTN_BLK) <= ri)
    mask = (ci < N_META_TOK) | prev_ok | cur_ok
    k_groups = [jnp.concatenate([km[...], kp[...], kc[...]], axis=0)
                for km, kp, kc in ((km0, kp0, kc0), (km1, kp1, kc1))]
    v_groups = [jnp.concatenate([vm[...], vp[...], vc[...]], axis=0)
                for vm, vp, vc in ((vm0, vp0, vc0), (vm1, vp1, vc1))]

    def write(pr, acc):
        o_ref[:, pr * LANES:(pr + 1) * LANES] = acc.astype(o_ref.dtype)

    _attn_core(q_ref, k_groups, v_groups, mask, sink_ref, write)


def _attn_meta_kernel(sink_ref, q_ref, k0, k1, v0, v1, o_ref):
    nm = q_ref.shape[0]
    ri = lax.broadcasted_iota(I32, (nm, nm), 0)
    ci = lax.broadcasted_iota(I32, (nm, nm), 1)
    same_batch = (ri // N_META_TOK) == (ci // N_META_TOK)
    mask = same_batch & ((ci % N_META_TOK) <= (ri % N_META_TOK))
    o_ref[...] = jnp.zeros_like(o_ref)

    def write(pr, acc):
        o_ref[0:nm, pr * LANES:(pr + 1) * LANES] = acc.astype(o_ref.dtype)

    _attn_core(q_ref, [k0[...], k1[...]], [v0[...], v1[...]], mask, sink_ref, write)


def _attention(qkv, sinks, B, S):
    R = qkv.shape[0]
    NS = B * S
    NM = B * N_META_TOK
    nb = S // ATTN_BLK
    qc = Q_W // LANES
    smem = pl.BlockSpec(memory_space=pltpu.SMEM)

    def kv_spec(col, prev):
        if prev:
            return pl.BlockSpec((ATTN_BLK, LANES), lambda b, n: (b * nb + jnp.maximum(n - 1, 0), col))
        return pl.BlockSpec((ATTN_BLK, LANES), lambda b, n: (b * nb + n, col))

    def meta_spec(col):
        return pl.BlockSpec((N_META_TOK, LANES), lambda b, n: (NS // N_META_TOK + b, col))

    in_specs = [smem, pl.BlockSpec((ATTN_BLK, Q_W), lambda b, n: (b * nb + n, 0))]
    in_specs += [kv_spec(qc, False), kv_spec(qc + 1, False), kv_spec(qc + 2, False), kv_spec(qc + 3, False)]
    in_specs += [kv_spec(qc, True), kv_spec(qc + 1, True), kv_spec(qc + 2, True), kv_spec(qc + 3, True)]
    in_specs += [meta_spec(qc), meta_spec(qc + 1), meta_spec(qc + 2), meta_spec(qc + 3)]
    attn = pl.pallas_call(
        _attn_seq_kernel,
        grid=(B, nb),
        in_specs=in_specs,
        out_specs=pl.BlockSpec((ATTN_BLK, Q_W), lambda b, n: (b * nb + n, 0)),
        out_shape=jax.ShapeDtypeStruct((NS, Q_W), BF16),
        compiler_params=_cparams(("parallel", "parallel"), VMEM_LIMIT),
        name="attn_seq",
    )(sinks, *([qkv] * 13))

    mb = NS // NM
    tail = R - NS
    blk = lambda col: pl.BlockSpec((NM, LANES), lambda i: (mb, col))
    attn_tail = pl.pallas_call(
        _attn_meta_kernel,
        grid=(1,),
        in_specs=[smem, pl.BlockSpec((NM, Q_W), lambda i: (mb, 0)),
                  blk(qc), blk(qc + 1), blk(qc + 2), blk(qc + 3)],
        out_specs=pl.BlockSpec((tail, Q_W), lambda i: (0, 0)),
        out_shape=jax.ShapeDtypeStruct((tail, Q_W), BF16),
        compiler_params=_cparams(("arbitrary",), VMEM_LIMIT),
        name="attn_meta",
    )(sinks, qkv, qkv, qkv, qkv, qkv)
    return attn, attn_tail


def _hgrn_prep_kernel(rq_ref, rf_ref, ri_ref, lbl_ref, qd_ref, kd_ref, ke_ref, v_ref, eb_ref):
    lbl = lbl_ref[...]
    e = jnp.exp(lbl - jnp.max(lbl, axis=0, keepdims=True))
    lb = e[0:1] / jnp.sum(e, axis=0, keepdims=True)
    f = lb + (1.0 - lb) * jax.nn.sigmoid(rf_ref[...])
    logf = jnp.log(f)
    k = 1.0 - f
    tm = logf.shape[0]
    pos = lax.broadcasted_iota(I32, logf.shape, 0) % REC_CHUNK
    b = logf
    suf = logf
    s = 1
    while s < REC_CHUNK:
        b = b + jnp.where(pos >= s, pltpu.roll(b, s, 0), 0.0)
        suf = suf + jnp.where(pos < REC_CHUNK - s, pltpu.roll(suf, tm - s, 0), 0.0)
        s *= 2
    eb = jnp.exp(b)
    eb_ref[...] = eb
    qd_ref[...] = (rq_ref[...] * eb).astype(BF16)
    kd_ref[...] = (k * jnp.exp(-b)).astype(BF16)
    ke_ref[...] = (k * jnp.exp(suf - logf)).astype(BF16)
    v_ref[...] = ri_ref[...].astype(BF16)


def _hgrn_prep(rest, lbl):
    R = rest.shape[0]
    W = REC_W
    col = lambda c: pl.BlockSpec((SEG, W), lambda i: (i, c))
    row_out = pl.BlockSpec((SEG, W), lambda i: (i, 0))
    return pl.pallas_call(
        _hgrn_prep_kernel,
        grid=(R // SEG,),
        in_specs=[col(0), col(1), col(2), pl.BlockSpec(lbl.shape, lambda i: (0, 0))],
        out_specs=[row_out] * 5,
        out_shape=[jax.ShapeDtypeStruct((R, W), BF16)] * 4 + [jax.ShapeDtypeStruct((R, W), F32)],
        compiler_params=_cparams(("parallel",), VMEM_LIMIT),
        name="hgrn_prep",
    )(rest, rest, rest, lbl)


def _hgrn_intra(qd, kd, v):
    n = qd.shape[0]
    ri = lax.broadcasted_iota(I32, (n, n), 0)
    ci = lax.broadcasted_iota(I32, (n, n), 1)
    keep = ((ri // REC_CHUNK) == (ci // REC_CHUNK)) & (ri >= ci)
    sc = lax.dot_general(qd, kd, (((1,), (1,)), ((), ())), preferred_element_type=F32)
    sc = jnp.where(keep, sc, 0.0)
    return jnp.dot(sc.astype(BF16), v, preferred_element_type=F32)


def _hgrn_kv_t(v, ke):
    return lax.dot_general(v, ke, (((0,), (0,)), ((), ())), preferred_element_type=F32)


def _hgrn_finish(o, g, norm_g):
    y = o * lax.rsqrt(jnp.mean(o * o, axis=-1, keepdims=True) + NORM_EPS) * norm_g
    return (y * (g * jax.nn.sigmoid(g))).astype(BF16)


def _hgrn_meta_kernel(qd_ref, kd_ref, ke_ref, v_ref, rg_ref, ng_ref, st_ref, rec_ref, *, nbatch):
    rec_ref[...] = jnp.zeros_like(rec_ref)
    nm = nbatch * REC_CHUNK
    for h in range(N_REC_HEADS):
        cols = slice(h * REC_DIM, (h + 1) * REC_DIM)
        o = _hgrn_intra(qd_ref[0:nm, cols], kd_ref[0:nm, cols], v_ref[0:nm, cols])
        rec_ref[0:nm, cols] = _hgrn_finish(o, rg_ref[0:nm, cols], ng_ref[...])
        for b in range(nbatch):
            rows = slice(b * REC_CHUNK, (b + 1) * REC_CHUNK)
            st_ref[b, h] = _hgrn_kv_t(v_ref[rows, cols], ke_ref[rows, cols])


def _hgrn_scan_kernel(qd_ref, kd_ref, ke_ref, v_ref, eb_ref, rg_ref, ng_ref, st0_ref,
                      rec_ref, st_scr, o_scr):
    @pl.when(pl.program_id(1) == 0)
    def _():
        st_scr[...] = st0_ref[0]

    for h in range(N_REC_HEADS):
        cols = slice(h * REC_DIM, (h + 1) * REC_DIM)
        o_scr[:, cols] = _hgrn_intra(qd_ref[:, cols], kd_ref[:, cols], v_ref[:, cols])

    def chunk(c, carry):
        r0 = pl.multiple_of(c * REC_CHUNK, REC_CHUNK)
        rows = pl.ds(r0, REC_CHUNK)
        for h in range(N_REC_HEADS):
            cols = slice(h * REC_DIM, (h + 1) * REC_DIM)
            last8 = eb_ref[pl.ds(pl.multiple_of(r0 + REC_CHUNK - 8, 8), 8), cols]
            dec = last8[7:8]
            st = st_scr[h]
            qd = qd_ref[rows, cols]
            o_scr[rows, cols] += lax.dot_general(qd, st.astype(BF16), (((1,), (1,)), ((), ())),
                                                 preferred_element_type=F32)
            st_scr[h] = st * dec + _hgrn_kv_t(v_ref[rows, cols], ke_ref[rows, cols])
        return carry

    lax.fori_loop(0, SEG // REC_CHUNK, chunk, 0, unroll=2)
    for h in range(N_REC_HEADS):
        cols = slice(h * REC_DIM, (h + 1) * REC_DIM)
        rec_ref[:, cols] = _hgrn_finish(o_scr[:, cols], rg_ref[:, cols], ng_ref[...])


def _hgrn(rest, lbl, norm_g, B, S):
    R = rest.shape[0]
    NS = B * S
    W = REC_W
    tail = R - NS
    tb = NS // tail
    qd, kd, ke, v, eb = _hgrn_prep(rest, lbl)

    tail_spec = pl.BlockSpec((tail, W), lambda i: (tb, 0))
    state, rec_tail = pl.pallas_call(
        functools.partial(_hgrn_meta_kernel, nbatch=B),
        grid=(1,),
        in_specs=[tail_spec, tail_spec, tail_spec, tail_spec,
                  pl.BlockSpec((tail, W), lambda i: (tb, 3)),
                  pl.BlockSpec((1, REC_DIM), lambda i: (0, 0))],
        out_specs=[pl.BlockSpec((B, N_REC_HEADS, REC_DIM, REC_DIM), lambda i: (0, 0, 0, 0)),
                   pl.BlockSpec((tail, W), lambda i: (0, 0))],
        out_shape=[jax.ShapeDtypeStruct((B, N_REC_HEADS, REC_DIM, REC_DIM), F32),
                   jax.ShapeDtypeStruct((tail, W), BF16)],
        compiler_params=_cparams(("arbitrary",), VMEM_LIMIT),
        name="hgrn_meta",
    )(qd, kd, ke, v, rest, norm_g)

    ns = S // SEG
    seg = pl.BlockSpec((SEG, W), lambda b, s: (b * ns + s, 0))
    rec = pl.pallas_call(
        _hgrn_scan_kernel,
        grid=(B, ns),
        in_specs=[seg, seg, seg, seg, seg,
                  pl.BlockSpec((SEG, W), lambda b, s: (b * ns + s, 3)),
                  pl.BlockSpec((1, REC_DIM), lambda b, s: (0, 0)),
                  pl.BlockSpec((1, N_REC_HEADS, REC_DIM, REC_DIM), lambda b, s: (b, 0, 0, 0))],
        out_specs=seg,
        out_shape=jax.ShapeDtypeStruct((NS, W), BF16),
        scratch_shapes=[pltpu.VMEM((N_REC_HEADS, REC_DIM, REC_DIM), F32),
                        pltpu.VMEM((SEG, W), F32)],
        compiler_params=_cparams(("parallel", "arbitrary"), VMEM_LIMIT),
        name="hgrn_scan",
    )(qd, kd, ke, v, eb, rest, norm_g, state)
    return rec, rec_tail


def _mix_kernel(attn_ref, attn_t_ref, rec_ref, rec_t_ref, ga_ref, gr_ref, x_ref, tail_ref,
                wa_ref, wr_ref, wo_ref, g_ref, wrt_ref, brt_ref, h2_ref, xn_ref, lg_ref, *, n_seq_tiles):
    is_seq = pl.program_id(0) < n_seq_tiles
    attn = jnp.where(is_seq, attn_ref[...], attn_t_ref[...])
    rec = jnp.where(is_seq, rec_ref[...], rec_t_ref[...])
    h = jnp.where(is_seq, x_ref[...], tail_ref[...])
    a = jnp.dot(attn, wa_ref[...], preferred_element_type=F32)
    r = jnp.dot(rec, wr_ref[...], preferred_element_type=F32)
    mixed = jax.nn.sigmoid(ga_ref[...]) * a + jax.nn.sigmoid(gr_ref[...]) * r
    h2 = h + jnp.dot(mixed.astype(BF16), wo_ref[...], preferred_element_type=F32)
    h2_ref[...] = h2
    xn = _rms(h2, g_ref[...])
    xn_ref[...] = xn
    xn_hi = xn.astype(BF16)
    xn_lo = (xn - xn_hi.astype(F32)).astype(BF16)
    w2 = wrt_ref[...]
    p_hi = jnp.dot(xn_hi, w2, preferred_element_type=F32)
    p_lo = jnp.dot(xn_lo, w2[:, :LANES], preferred_element_type=F32)
    lg_ref[...] = p_hi[:, :LANES] + p_hi[:, LANES:] + p_lo + brt_ref[...]


def _mix(attn, attn_tail, rec, rec_tail, rest, x2d, tail_rows, wa, wr, wo, g, w_router, b_router):
    NS, D = x2d.shape
    R = NS + tail_rows.shape[0]
    tm = MIX_TILE
    nst = NS // tm
    const = lambda shape: pl.BlockSpec(shape, lambda i: (0, 0), pipeline_mode=pl.Buffered(1))
    row = lambda w: pl.BlockSpec((tm, w), lambda i: (i, 0))
    seq = lambda w: pl.BlockSpec((tm, w), lambda i: (jnp.minimum(i, nst - 1), 0))
    tl = lambda w: pl.BlockSpec((tm, w), lambda i: (jnp.maximum(i - nst, 0), 0))
    return pl.pallas_call(
        functools.partial(_mix_kernel, n_seq_tiles=nst),
        grid=(R // tm,),
        in_specs=[seq(Q_W), tl(Q_W), seq(REC_W), tl(REC_W),
                  pl.BlockSpec((tm, D), lambda i: (i, 2)), pl.BlockSpec((tm, D), lambda i: (i, 3)),
                  seq(D), tl(D), const(wa.shape), const(wr.shape), const(wo.shape), const(g.shape),
                  const(w_router.shape), const(b_router.shape)],
        out_specs=[row(D), row(D), row(LANES)],
        out_shape=[jax.ShapeDtypeStruct((R, D), F32), jax.ShapeDtypeStruct((R, D), F32),
                   jax.ShapeDtypeStruct((R, LANES), F32)],
        compiler_params=_cparams(("parallel",), VMEM_LIMIT),
        name="mix_outproj",
    )(attn, attn_tail, rec, rec_tail, rest, rest, x2d, tail_rows, wa, wr, wo, g, w_router, b_router)


def _route_kernel(lg_ref, dest_ref, gate_ref, cnt_ref, cnt_scr, carry_scr, start_scr, *, n_tok):
    ph = pl.program_id(0)
    i = pl.program_id(1)
    tm = lg_ref.shape[0]

    @pl.when((ph == 0) & (i == 0))
    def _():
        cnt_scr[...] = jnp.zeros_like(cnt_scr)

    lane = lax.broadcasted_iota(I32, (tm, LANES), 1)
    valid = (i * tm + lax.broadcasted_iota(I32, (tm, LANES), 0)) < n_tok
    work = lg_ref[...]
    onehots, vals = [], []
    for _ in range(TOP_K):
        m = jnp.max(work, axis=-1, keepdims=True)
        idx = jnp.min(jnp.where(work == m, lane, LANES), axis=-1, keepdims=True)
        oh = lane == idx
        onehots.append(oh)
        vals.append(m)
        work = jnp.where(oh, -jnp.inf, work)
    multi = jnp.zeros((tm, LANES), F32)
    for oh in onehots:
        multi = multi + jnp.where(oh & valid, 1.0, 0.0)
    tile_cnt = jnp.sum(multi, axis=0, keepdims=True)

    @pl.when(ph == 0)
    def _():
        cnt_scr[...] += tile_cnt

    @pl.when(ph == 1)
    def _():
        @pl.when(i == 0)
        def _():
            c = cnt_scr[...]
            padded = jnp.ceil(c * (1.0 / MOE_BLK)) * MOE_BLK
            before = (lax.broadcasted_iota(I32, (LANES, LANES), 0)
                      < lax.broadcasted_iota(I32, (LANES, LANES), 1))
            start = jnp.dot(jnp.broadcast_to(padded, (8, LANES)), jnp.where(before, 1.0, 0.0),
                            preferred_element_type=F32, precision=lax.Precision.HIGHEST)
            start_scr[...] = start[0:1]
            carry_scr[...] = jnp.zeros_like(carry_scr)
            cnt_ref[...] = c

        earlier = (lax.broadcasted_iota(I32, (tm, tm), 1) < lax.broadcasted_iota(I32, (tm, tm), 0))
        prefix = jnp.dot(jnp.where(earlier, 1.0, 0.0).astype(BF16), multi.astype(BF16),
                         preferred_element_type=F32)
        base = prefix + carry_scr[...] + start_scr[...]
        den = jnp.zeros_like(vals[0])
        for v in vals:
            den = den + jnp.exp(v - vals[0])
        dest = jnp.zeros((tm, LANES), F32)
        gate = jnp.zeros((tm, LANES), F32)
        for k in range(TOP_K):
            d_k = jnp.sum(jnp.where(onehots[k], base, 0.0), axis=-1, keepdims=True)
            dest = jnp.where(lane == k, d_k, dest)
            gate = jnp.where(lane == k, jnp.exp(vals[k] - vals[0]) / den, gate)
        dest_ref[...] = dest.astype(I32)
        gate_ref[...] = gate
        carry_scr[...] += tile_cnt


def _route(logits, n_tok):
    R = logits.shape[0]
    tm = ROW_TILE
    blk = pl.BlockSpec((tm, LANES), lambda p, i: (i * p, 0))
    return pl.pallas_call(
        functools.partial(_route_kernel, n_tok=n_tok),
        grid=(2, R // tm),
        in_specs=[pl.BlockSpec((tm, LANES), lambda p, i: (i, 0))],
        out_specs=[blk, blk, pl.BlockSpec((1, LANES), lambda p, i: (0, 0))],
        out_shape=[jax.ShapeDtypeStruct((R, LANES), I32), jax.ShapeDtypeStruct((R, LANES), F32),
                   jax.ShapeDtypeStruct((1, LANES), F32)],
        scratch_shapes=[pltpu.VMEM((1, LANES), F32)] * 3,
        compiler_params=_cparams(("arbitrary", "arbitrary"), VMEM_LIMIT),
        name="route",
    )(logits)


def _dispatch_kernel(dest_ref, xn_ref, xs_ref, sem):
    def row_copy(r, k):
        return pltpu.make_async_copy(xn_ref.at[pl.ds(r, 1), :],
                                     xs_ref.at[pl.ds(dest_ref[r * TOP_K + k], 1), :], sem)

    def issue(r, carry):
        for k in range(TOP_K):
            row_copy(r, k).start()
        return carry

    lax.fori_loop(0, TOK_TILE, issue, 0)
    for _ in range(TOP_K):
        pltpu.make_async_copy(xn_ref, xs_ref.at[pl.ds(0, TOK_TILE), :], sem).wait()


def _dispatch(xn, dest_flat, n_tok, n_slots):
    D = xn.shape[1]
    return pl.pallas_call(
        _dispatch_kernel,
        grid=(n_tok // TOK_TILE,),
        in_specs=[pl.BlockSpec((TOK_TILE * TOP_K,), lambda i: (i,), memory_space=pltpu.SMEM),
                  pl.BlockSpec((TOK_TILE, D), lambda i: (i, 0))],
        out_specs=pl.BlockSpec(memory_space=pl.ANY),
        out_shape=jax.ShapeDtypeStruct((n_slots, D), F32),
        scratch_shapes=[pltpu.SemaphoreType.DMA],
        compiler_params=_cparams(("arbitrary",), VMEM_LIMIT),
        name="dispatch",
    )(dest_flat, xn)


class _WeightStream:
    def __init__(self, w_refs, stage, wbuf, sems, unit_rows):
        self.w_refs, self.stage, self.wbuf, self.sems = w_refs, stage, wbuf, sems
        self.unit_rows = unit_rows
        self.n_units = w_refs[0].shape[1] // unit_rows
        self.n_stage = stage.shape[0]

    def _copy(self, e, u, m):
        rows = pl.ds(pl.multiple_of(u * self.unit_rows, self.unit_rows), self.unit_rows)
        st = u % self.n_stage
        return pltpu.make_async_copy(self.w_refs[m].at[e, rows, :], self.stage.at[st, m],
                                     self.sems.at[st, m])

    def start(self, e, u):
        for m in range(len(self.w_refs)):
            self._copy(e, u, m).start()

    def finish(self, e, u, slot):
        rows = pl.ds(pl.multiple_of(u * self.unit_rows, self.unit_rows), self.unit_rows)
        for m in range(len(self.w_refs)):
            self._copy(e, u, m).wait()
            self.wbuf[slot, m, rows, :] = self.stage[u % self.n_stage, m].astype(BF16)

    def prime(self, e):
        for u in range(self.n_stage):
            self.start(e, u)

    def convert(self, e, slot, lo, hi):
        def body(u, carry):
            self.finish(e, u, slot)

            @pl.when(u + self.n_stage < self.n_units)
            def _():
                self.start(e, u + self.n_stage)

            return carry

        lax.fori_loop(lo, hi, body, 0)


def _moe_step(plan, stream, compute):
    be_ref, nxt_ref, slot_ref, first_ref, ulo_ref, uhi_ref, nu_ref = plan
    blk = pl.program_id(0)

    @pl.when(blk < nu_ref[0])
    def _():
        e = be_ref[blk]
        ne = nxt_ref[blk]
        slot = slot_ref[blk]

        @pl.when(blk == 0)
        def _():
            stream.prime(e)
            stream.convert(e, slot, 0, stream.n_units)

        @pl.when((first_ref[blk] == 1) & (ne >= 0))
        def _():
            stream.prime(ne)

        compute(slot)

        @pl.when(ne >= 0)
        def _():
            stream.convert(ne, 1 - slot, ulo_ref[blk], uhi_ref[blk])


def _moe_up_kernel(be_ref, nxt_ref, slot_ref, first_ref, ulo_ref, uhi_ref, nu_ref,
                   xs_ref, wg_ref, wu_ref, bg_ref, bu_ref, act_ref, wbuf, stage, sems):
    stream = _WeightStream((wg_ref, wu_ref), stage, wbuf, sems, MOE_UNIT_UP)
    half = act_ref.shape[1] // 2

    def compute(slot):
        xb = xs_ref[...].astype(BF16)
        for c in range(2):
            cols = slice(c * half, (c + 1) * half)
            gate = jnp.dot(xb, wbuf[slot, 0, :, cols], preferred_element_type=F32) + bg_ref[:, cols]
            up = jnp.dot(xb, wbuf[slot, 1, :, cols], preferred_element_type=F32) + bu_ref[:, cols]
            gate = jnp.minimum(gate, SWIGLU_LIMIT)
            up = jnp.clip(up, -SWIGLU_LIMIT, SWIGLU_LIMIT)
            act = gate * jax.nn.sigmoid(SWIGLU_ALPHA * gate) * (up + 1.0)
            act_ref[:, cols] = act.astype(act_ref.dtype)

    _moe_step((be_ref, nxt_ref, slot_ref, first_ref, ulo_ref, uhi_ref, nu_ref), stream, compute)


def _moe_down_kernel(be_ref, nxt_ref, slot_ref, first_ref, ulo_ref, uhi_ref, nu_ref,
                     act_ref, wd_ref, bd_ref, y_ref, wbuf, stage, sems):
    stream = _WeightStream((wd_ref,), stage, wbuf, sems, MOE_UNIT_DOWN)

    def compute(slot):
        y_ref[...] = jnp.dot(act_ref[...], wbuf[slot, 0], preferred_element_type=F32) + bd_ref[...]

    _moe_step((be_ref, nxt_ref, slot_ref, first_ref, ulo_ref, uhi_ref, nu_ref), stream, compute)


def _moe_plan(cnt, n_blk, n_units):
    E = cnt.shape[0]
    nblk_e = (cnt + MOE_BLK - 1) // MOE_BLK
    blk_end = jnp.cumsum(nblk_e)
    blk_first = blk_end - nblk_e
    j = jnp.arange(n_blk, dtype=I32)
    be = jnp.minimum(jnp.sum((blk_end[None, :] <= j[:, None]).astype(I32), axis=1), E - 1)
    k = j - blk_first[be]
    nb = jnp.maximum(nblk_e[be], 1)
    nonempty = nblk_e > 0
    ids = jnp.arange(E, dtype=I32)
    later = (ids[None, :] > ids[:, None]) & nonempty[None, :]
    nxt_e = jnp.min(jnp.where(later, ids[None, :], E), axis=1)
    nxt_e = jnp.where(nxt_e >= E, -1, nxt_e)
    slot_e = (jnp.cumsum(nonempty.astype(I32)) - 1) % 2
    plan = (be, nxt_e[be], slot_e[be], (k == 0).astype(I32),
            (k * n_units) // nb, ((k + 1) * n_units) // nb, blk_end[-1:].astype(I32))
    return tuple(a.astype(I32) for a in plan)


def _moe(xs, cnt, w_gate, b_gate, w_up, b_up, w_down, b_down):
    P, D = xs.shape
    E, _, F = w_gate.shape
    n_blk = P // MOE_BLK
    n_stage = 2
    clamp = lambda blk, nu: jnp.maximum(jnp.minimum(blk, nu[0] - 1), 0)
    row_map = lambda blk, be, nx, sl, fi, lo, hi, nu: (clamp(blk, nu), 0)
    exp_map = lambda blk, be, nx, sl, fi, lo, hi, nu: (be[clamp(blk, nu)], 0, 0)
    hbm = pl.BlockSpec(memory_space=pl.ANY)

    plan_up = _moe_plan(cnt, n_blk, D // MOE_UNIT_UP)
    act = pl.pallas_call(
        _moe_up_kernel,
        grid_spec=pltpu.PrefetchScalarGridSpec(
            num_scalar_prefetch=7,
            grid=(n_blk,),
            in_specs=[pl.BlockSpec((MOE_BLK, D), row_map), hbm, hbm,
                      pl.BlockSpec((None, 1, F), exp_map), pl.BlockSpec((None, 1, F), exp_map)],
            out_specs=pl.BlockSpec((MOE_BLK, F), row_map),
            scratch_shapes=[pltpu.VMEM((2, 2, D, F), BF16),
                            pltpu.VMEM((n_stage, 2, MOE_UNIT_UP, F), F32),
                            pltpu.SemaphoreType.DMA((n_stage, 2))],
        ),
        out_shape=jax.ShapeDtypeStruct((P, F), BF16),
        compiler_params=_cparams(("arbitrary",), VMEM_LIMIT),
        name="moe_up",
    )(*plan_up, xs, w_gate, w_up, b_gate.reshape(E, 1, F), b_up.reshape(E, 1, F))

    plan_down = _moe_plan(cnt, n_blk, F // MOE_UNIT_DOWN)
    return pl.pallas_call(
        _moe_down_kernel,
        grid_spec=pltpu.PrefetchScalarGridSpec(
            num_scalar_prefetch=7,
            grid=(n_blk,),
            in_specs=[pl.BlockSpec((MOE_BLK, F), row_map), hbm, pl.BlockSpec((None, 1, D), exp_map)],
            out_specs=pl.BlockSpec((MOE_BLK, D), row_map),
            scratch_shapes=[pltpu.VMEM((2, 1, F, D), BF16),
                            pltpu.VMEM((n_stage, 1, MOE_UNIT_DOWN, D), F32),
                            pltpu.SemaphoreType.DMA((n_stage, 1))],
        ),
        out_shape=jax.ShapeDtypeStruct((P, D), F32),
        compiler_params=_cparams(("arbitrary",), VMEM_LIMIT),
        name="moe_down",
    )(*plan_down, act, w_down, b_down.reshape(E, 1, D))


def _combine_kernel(dest_ref, dnext_ref, h2_ref, gate_ref, g_ref, y_ref, o_ref, ybuf, sem):
    i = pl.program_id(0)
    slot = i % 2

    def issue(dref, s):
        def body(r, carry):
            for k in range(TOP_K):
                pltpu.make_async_copy(y_ref.at[pl.ds(dref[r * TOP_K + k], 1), :],
                                      ybuf.at[s, k, pl.ds(r, 1), :], sem.at[s]).start()
            return carry

        lax.fori_loop(0, TOK_TILE, body, 0)

    @pl.when(i == 0)
    def _():
        issue(dest_ref, 0)

    @pl.when(i + 1 < pl.num_programs(0))
    def _():
        issue(dnext_ref, 1 - slot)

    for k in range(TOP_K):
        pltpu.make_async_copy(y_ref.at[pl.ds(0, TOK_TILE), :], ybuf.at[slot, k], sem.at[slot]).wait()
    gates = gate_ref[...]
    acc = h2_ref[...]
    for k in range(TOP_K):
        acc = acc + gates[:, k:k + 1] * ybuf[slot, k]
    o_ref[...] = _rms(acc, g_ref[...])


def _combine(h2, gates, dest_flat, y, g, n_seq):
    D = h2.shape[1]
    n_tiles = n_seq // TOK_TILE
    return pl.pallas_call(
        _combine_kernel,
        grid=(n_tiles,),
        in_specs=[pl.BlockSpec((TOK_TILE * TOP_K,), lambda i: (i,), memory_space=pltpu.SMEM),
                  pl.BlockSpec((TOK_TILE * TOP_K,), lambda i: (jnp.minimum(i + 1, n_tiles - 1),),
                               memory_space=pltpu.SMEM),
                  pl.BlockSpec((TOK_TILE, D), lambda i: (i, 0)),
                  pl.BlockSpec((TOK_TILE, LANES), lambda i: (i, 0)),
                  pl.BlockSpec((1, D), lambda i: (0, 0)),
                  pl.BlockSpec(memory_space=pl.ANY)],
        out_specs=pl.BlockSpec((TOK_TILE, D), lambda i: (i, 0)),
        out_shape=jax.ShapeDtypeStruct((n_seq, D), F32),
        scratch_shapes=[pltpu.VMEM((2, TOP_K, TOK_TILE, D), F32), pltpu.SemaphoreType.DMA((2,))],
        compiler_params=_cparams(("arbitrary",), VMEM_LIMIT),
        name="combine_norm",
    )(dest_flat, dest_flat, h2, gates, g, y)


def _rope_tables(S, n_meta_rows):
    half = HEAD_DIM // 2
    inv_freq = ROPE_THETA ** (-jnp.arange(half, dtype=F32) / half)
    lane = np.arange(LANES)
    fidx = lane % half
    sign = np.where((lane % HEAD_DIM) < half, -1.0, 1.0).astype(np.float32)
    pos_seq = N_META_TOK + jnp.arange(S)
    tail = jnp.arange(ROW_TILE)
    pos_tail = jnp.where(tail < n_meta_rows, tail % N_META_TOK, 0)
    pos = jnp.concatenate([pos_seq, pos_tail]).astype(F32)
    ang = (pos[:, None] * inv_freq[None, :])[:, fidx]
    return jnp.cos(ang), jnp.sin(ang) * sign[None, :]


def kernel(x, meta_tokens, attn_norm_g, w_in, attn_sinks, lower_bound_logits, rec_norm_g,
           w_attn_proj, w_rec_proj, w_out, ffn_norm_g, w_router, b_router, w_gate, b_gate,
           w_up, b_up, w_down, b_down, final_norm_g):
    B, S, D = x.shape
    assert w_in.shape[0] == 1, "single-layer block"
    assert S % ROW_TILE == 0 and (B * S) % ROW_TILE == 0 and S % SEG == 0
    NS = B * S
    NM = B * N_META_TOK
    NT = NS + NM
    R = -(-NT // ROW_TILE) * ROW_TILE
    assert R - NS == ROW_TILE and NT % TOK_TILE == 0 and NS % TOK_TILE == 0

    x2d = x.reshape(NS, D)
    tail_rows = jnp.concatenate([jnp.tile(meta_tokens.astype(x.dtype), (B, 1)),
                                 jnp.zeros((R - NT, D), x.dtype)], axis=0)

    w0 = w_in[0]
    kw = N_KV_HEADS * HEAD_DIM
    wq, wk, wv = w0[:, :Q_W], w0[:, Q_W:Q_W + kw], w0[:, Q_W + kw:Q_W + 2 * kw]
    dup = lambda w: jnp.concatenate([w[:, :HEAD_DIM]] * 2 + [w[:, HEAD_DIM:]] * 2, axis=1)
    w_qkv = jnp.concatenate([wq, dup(wk), dup(wv)], axis=1).astype(BF16)
    w_rest = w0[:, Q_W + 2 * kw:].astype(BF16)

    cos_t, sin_t = _rope_tables(S, NM)
    u, qkv = _norm_qkv(x2d, tail_rows, attn_norm_g[0].reshape(1, D), w_qkv, cos_t, sin_t,
                       S // ROW_TILE)
    rest = _proj_rest(u, w_rest)

    attn, attn_tail = _attention(qkv, attn_sinks[0], B, S)
    rec, rec_tail = _hgrn(rest, lower_bound_logits, rec_norm_g[0].reshape(1, REC_DIM), B, S)

    wr_hi = w_router[0].astype(BF16)
    wr_lo = (w_router[0] - wr_hi.astype(F32)).astype(BF16)
    lane_pad = ((0, 0), (0, LANES - N_EXPERTS))
    wr_pad = jnp.concatenate([jnp.pad(wr_hi, lane_pad), jnp.pad(wr_lo, lane_pad)], axis=1)
    br_pad = jnp.pad(b_router[0].reshape(1, N_EXPERTS), ((0, 0), (0, LANES - N_EXPERTS)),
                     constant_values=NEG_INF)
    h2, xn, logits = _mix(attn, attn_tail, rec, rec_tail, rest, x2d, tail_rows,
                          w_attn_proj[0].astype(BF16), w_rec_proj[0].astype(BF16),
                          w_out[0].astype(BF16), ffn_norm_g[0].reshape(1, D), wr_pad, br_pad)

    dest, gates, counts = _route(logits, NT)
    cnt = counts[0, :N_EXPERTS].astype(I32)
    n_blk = -(-(NT * TOP_K + N_EXPERTS * (MOE_BLK - 1)) // MOE_BLK)
    dest_flat = dest[:NT, :TOP_K].reshape(NT * TOP_K)

    xs = _dispatch(xn, dest_flat, NT, n_blk * MOE_BLK)
    y = _moe(xs, cnt, w_gate[0], b_gate[0], w_up[0], b_up[0], w_down[0], b_down[0])
    out = _combine(h2, gates, dest_flat, y, final_norm_g.reshape(1, D), NS)
    return out.reshape(B, S, D)
```

```python
import functools

import jax
import jax.numpy as jnp
import numpy as np
from jax import lax
from jax.experimental import pallas as pl
from jax.experimental.pallas import tpu as pltpu

F32 = jnp.float32
BF16 = jnp.bfloat16
I32 = jnp.int32

N_META_TOK = 16
HEAD_DIM = 64
N_Q_HEADS = 16
N_KV_HEADS = 2
ATTN_BLK = 128
ROPE_THETA = 10000.0
N_REC_HEADS = 8
REC_DIM = 128
REC_CHUNK = 16
N_EXPERTS = 32
TOP_K = 4
SWIGLU_ALPHA = 1.702
SWIGLU_LIMIT = 7.0
NORM_EPS = 1e-5
NEG_INF = -1e30

LANES = 128
ROW_TILE = 512
MIX_TILE = 256
SEG = 256
SCAN_SEG = 256
SCAN_BATCHES = 2
MOE_BLK = 256
MOE_UNIT_UP = 128
MOE_UNIT_DOWN = 256
TOK_TILE = 128
VMEM_LIMIT = 56 * 1024 * 1024

Q_W = N_Q_HEADS * HEAD_DIM
REC_W = N_REC_HEADS * REC_DIM


def _cparams(sem, vmem=None, **kw):
    return pltpu.CompilerParams(dimension_semantics=sem, vmem_limit_bytes=vmem, **kw)


def _rms(xf, g):
    return xf * lax.rsqrt(jnp.mean(xf * xf, axis=-1, keepdims=True) + NORM_EPS) * g


def _norm_qkv_kernel(x_ref, tail_ref, g_ref, w_ref, cos_ref, sin_ref, u_ref, qkv_ref, *, n_seq_tiles):
    h = jnp.where(pl.program_id(0) < n_seq_tiles, x_ref[...], tail_ref[...])
    u = _rms(h, g_ref[...]).astype(BF16)
    u_ref[...] = u
    p = jnp.dot(u, w_ref[...], preferred_element_type=F32)
    cos = cos_ref[...]
    sin = sin_ref[...]
    lane = lax.broadcasted_iota(I32, cos.shape, 1)
    first_half = (lane % HEAD_DIM) < (HEAD_DIM // 2)
    n_q = Q_W // LANES
    for c in range(n_q + 2):
        xs = p[:, c * LANES:(c + 1) * LANES]
        swapped = jnp.where(first_half, pltpu.roll(xs, LANES - HEAD_DIM // 2, 1),
                            pltpu.roll(xs, HEAD_DIM // 2, 1))
        r = xs * cos + swapped * sin
        if c < n_q:
            r = r * (HEAD_DIM ** -0.5)
        qkv_ref[:, c * LANES:(c + 1) * LANES] = r.astype(BF16)
    v0 = (n_q + 2) * LANES
    qkv_ref[:, v0:] = p[:, v0:].astype(BF16)


def _norm_qkv(x2d, tail_rows, g, w_qkv, cos_t, sin_t, tiles_per_seq):
    NS, D = x2d.shape
    R = NS + tail_rows.shape[0]
    W = w_qkv.shape[1]
    n_seq_tiles = NS // ROW_TILE

    def tab_map(i):
        return (jnp.where(i < n_seq_tiles, i % tiles_per_seq, tiles_per_seq), 0)

    return pl.pallas_call(
        functools.partial(_norm_qkv_kernel, n_seq_tiles=n_seq_tiles),
        grid=(R // ROW_TILE,),
        in_specs=[
            pl.BlockSpec((ROW_TILE, D), lambda i: (jnp.minimum(i, n_seq_tiles - 1), 0)),
            pl.BlockSpec((ROW_TILE, D), lambda i: (0, 0)),
            pl.BlockSpec((1, D), lambda i: (0, 0)),
            pl.BlockSpec((D, W), lambda i: (0, 0)),
            pl.BlockSpec((ROW_TILE, LANES), tab_map),
            pl.BlockSpec((ROW_TILE, LANES), tab_map),
        ],
        out_specs=[
            pl.BlockSpec((ROW_TILE, D), lambda i: (i, 0)),
            pl.BlockSpec((ROW_TILE, W), lambda i: (i, 0)),
        ],
        out_shape=[jax.ShapeDtypeStruct((R, D), BF16), jax.ShapeDtypeStruct((R, W), BF16)],
        compiler_params=_cparams(("parallel",), VMEM_LIMIT),
        name="norm_qkv",
    )(x2d, tail_rows, g, w_qkv, cos_t, sin_t)


def _matmul_kernel(x_ref, w_ref, o_ref):
    o_ref[...] = jnp.dot(x_ref[...], w_ref[...], preferred_element_type=F32).astype(o_ref.dtype)


def _proj_rest(u, w):
    R, D = u.shape
    N = w.shape[1]
    tn = 2048
    return pl.pallas_call(
        _matmul_kernel,
        grid=(N // tn, R // ROW_TILE),
        in_specs=[pl.BlockSpec((ROW_TILE, D), lambda j, i: (i, 0)),
                  pl.BlockSpec((D, tn), lambda j, i: (0, j))],
        out_specs=pl.BlockSpec((ROW_TILE, tn), lambda j, i: (i, j)),
        out_shape=jax.ShapeDtypeStruct((R, N), F32),
        compiler_params=_cparams(("parallel", "parallel"), VMEM_LIMIT),
        name="proj_rest",
    )(u, w)


def _attn_core(q_ref, k_groups, v_groups, mask_fn, sink_ref, write):
    nkeys = k_groups[0].shape[0]
    rows = q_ref.shape[0]
    heads_per_group = N_Q_HEADS // N_KV_HEADS
    width = heads_per_group * rows
    lo_q = lax.broadcasted_iota(I32, (rows, LANES), 1) < HEAD_DIM
    ki = lax.broadcasted_iota(I32, (nkeys, width), 0)
    qi = lax.broadcasted_iota(I32, (nkeys, width), 1) % rows
    mask = mask_fn(ki, qi)
    top_half = lax.broadcasted_iota(I32, (LANES, rows), 0) < HEAD_DIM
    for g in range(N_KV_HEADS):
        parts = []
        for j in range(heads_per_group // 2):
            qp = q_ref[:, (heads_per_group // 2 * g + j) * LANES:(heads_per_group // 2 * g + j + 1) * LANES]
            zero_q = jnp.zeros_like(qp)
            parts += [jnp.where(lo_q, qp, zero_q), jnp.where(lo_q, zero_q, qp)]
        q_stack = jnp.concatenate(parts, axis=0)
        s = lax.dot_general(k_groups[g], q_stack, (((1,), (1,)), ((), ())),
                            preferred_element_type=F32)
        s = jnp.where(mask, s, NEG_INF)
        sk = sink_ref[:, g * width:(g + 1) * width]
        m = jnp.maximum(jnp.max(s, axis=0, keepdims=True), sk)
        e = jnp.exp(s - m)
        den = jnp.sum(e, axis=0, keepdims=True) + jnp.exp(sk - m)
        prob = (e * (1.0 / den)).astype(BF16)
        o_t = lax.dot_general(v_groups[g], prob, (((0,), (0,)), ((), ())), preferred_element_type=F32)
        for j in range(heads_per_group // 2):
            even = o_t[:, (2 * j) * rows:(2 * j + 1) * rows]
            odd = o_t[:, (2 * j + 1) * rows:(2 * j + 2) * rows]
            pair_t = jnp.where(top_half, even, odd)
            write(heads_per_group // 2 * g + j, pair_t.T)


def _attn_seq_kernel(sink_ref, q_ref, kc0, kc1, vc0, vc1, kp0, kp1, vp0, vp1,
                     km0, km1, vm0, vm1, o_ref):
    n = pl.program_id(1)
    no_prev = jnp.where(n > 0, 0, 2 * ATTN_BLK)

    def mask_fn(ki, qi):
        prev_ok = (ki >= N_META_TOK) & (ki < N_META_TOK + ATTN_BLK) & (ki - N_META_TOK > qi + no_prev)
        cur_ok = (ki >= N_META_TOK + ATTN_BLK) & (ki - (N_META_TOK + ATTN_BLK) <= qi)
        return (ki < N_META_TOK) | prev_ok | cur_ok

    k_groups = [jnp.concatenate([km[...], kp[...], kc[...]], axis=0)
                for km, kp, kc in ((km0, kp0, kc0), (km1, kp1, kc1))]
    v_groups = [jnp.concatenate([vm[...], vp[...], vc[...]], axis=0)
                for vm, vp, vc in ((vm0, vp0, vc0), (vm1, vp1, vc1))]

    def write(pr, acc):
        o_ref[:, pr * LANES:(pr + 1) * LANES] = acc.astype(o_ref.dtype)

    _attn_core(q_ref, k_groups, v_groups, mask_fn, sink_ref, write)


def _attn_meta_kernel(sink_ref, q_ref, k0, k1, v0, v1, o_ref):
    nm = q_ref.shape[0]

    def mask_fn(ki, qi):
        return ((qi // N_META_TOK) == (ki // N_META_TOK)) & ((ki % N_META_TOK) <= (qi % N_META_TOK))

    o_ref[...] = jnp.zeros_like(o_ref)

    def write(pr, acc):
        o_ref[0:nm, pr * LANES:(pr + 1) * LANES] = acc.astype(o_ref.dtype)

    _attn_core(q_ref, [k0[...], k1[...]], [v0[...], v1[...]], mask_fn, sink_ref, write)


def _attention(qkv, sinks, B, S):
    R = qkv.shape[0]
    NS = B * S
    NM = B * N_META_TOK
    nb = S // ATTN_BLK
    qc = Q_W // LANES
    sink_seq = jnp.repeat(sinks.astype(F32), ATTN_BLK).reshape(1, N_Q_HEADS * ATTN_BLK)
    sink_meta = jnp.repeat(sinks.astype(F32), NM).reshape(1, N_Q_HEADS * NM)

    def kv_spec(col, prev):
        if prev:
            return pl.BlockSpec((ATTN_BLK, LANES), lambda b, n: (b * nb + jnp.maximum(n - 1, 0), col))
        return pl.BlockSpec((ATTN_BLK, LANES), lambda b, n: (b * nb + n, col))

    def meta_spec(col):
        return pl.BlockSpec((N_META_TOK, LANES), lambda b, n: (NS // N_META_TOK + b, col))

    in_specs = [pl.BlockSpec(sink_seq.shape, lambda b, n: (0, 0)),
                pl.BlockSpec((ATTN_BLK, Q_W), lambda b, n: (b * nb + n, 0))]
    in_specs += [kv_spec(qc, False), kv_spec(qc + 1, False), kv_spec(qc + 2, False), kv_spec(qc + 3, False)]
    in_specs += [kv_spec(qc, True), kv_spec(qc + 1, True), kv_spec(qc + 2, True), kv_spec(qc + 3, True)]
    in_specs += [meta_spec(qc), meta_spec(qc + 1), meta_spec(qc + 2), meta_spec(qc + 3)]
    attn = pl.pallas_call(
        _attn_seq_kernel,
        grid=(B, nb),
        in_specs=in_specs,
        out_specs=pl.BlockSpec((ATTN_BLK, Q_W), lambda b, n: (b * nb + n, 0)),
        out_shape=jax.ShapeDtypeStruct((NS, Q_W), BF16),
        compiler_params=_cparams(("parallel", "parallel"), VMEM_LIMIT),
        name="attn_seq",
    )(sink_seq, *([qkv] * 13))

    mb = NS // NM
    tail = R - NS
    blk = lambda col: pl.BlockSpec((NM, LANES), lambda i: (mb, col))
    attn_tail = pl.pallas_call(
        _attn_meta_kernel,
        grid=(1,),
        in_specs=[pl.BlockSpec(sink_meta.shape, lambda i: (0, 0)),
                  pl.BlockSpec((NM, Q_W), lambda i: (mb, 0)),
                  blk(qc), blk(qc + 1), blk(qc + 2), blk(qc + 3)],
        out_specs=pl.BlockSpec((tail, Q_W), lambda i: (0, 0)),
        out_shape=jax.ShapeDtypeStruct((tail, Q_W), BF16),
        compiler_params=_cparams(("arbitrary",), VMEM_LIMIT),
        name="attn_meta",
    )(sink_meta, qkv, qkv, qkv, qkv, qkv)
    return attn, attn_tail


def _hgrn_prep_kernel(rq_ref, rf_ref, ri_ref, lbl_ref, qd_ref, kd_ref, ke_ref, v_ref, eb_ref):
    lbl = lbl_ref[...]
    e = jnp.exp(lbl - jnp.max(lbl, axis=0, keepdims=True))
    lb = e[0:1] / jnp.sum(e, axis=0, keepdims=True)
    f = lb + (1.0 - lb) * jax.nn.sigmoid(rf_ref[...])
    logf = jnp.log(f)
    k = 1.0 - f
    tm = logf.shape[0]
    pos = lax.broadcasted_iota(I32, logf.shape, 0) % REC_CHUNK
    b = logf
    suf = logf
    s = 1
    while s < REC_CHUNK:
        b = b + jnp.where(pos >= s, pltpu.roll(b, s, 0), 0.0)
        suf = suf + jnp.where(pos < REC_CHUNK - s, pltpu.roll(suf, tm - s, 0), 0.0)
        s *= 2
    eb = jnp.exp(b)
    eb_ref[...] = eb
    qd_ref[...] = (rq_ref[...] * eb).astype(BF16)
    kd_ref[...] = (k * jnp.exp(-b)).astype(BF16)
    ke_ref[...] = (k * jnp.exp(suf - logf)).astype(BF16)
    v_ref[...] = ri_ref[...].astype(BF16)


def _hgrn_prep(rest, lbl):
    R = rest.shape[0]
    W = REC_W
    col = lambda c: pl.BlockSpec((SEG, W), lambda i: (i, c))
    row_out = pl.BlockSpec((SEG, W), lambda i: (i, 0))
    return pl.pallas_call(
        _hgrn_prep_kernel,
        grid=(R // SEG,),
        in_specs=[col(0), col(1), col(2), pl.BlockSpec(lbl.shape, lambda i: (0, 0))],
        out_specs=[row_out] * 5,
        out_shape=[jax.ShapeDtypeStruct((R, W), BF16)] * 4 + [jax.ShapeDtypeStruct((R, W), F32)],
        compiler_params=_cparams(("parallel",), VMEM_LIMIT),
        name="hgrn_prep",
    )(rest, rest, rest, lbl)


def _hgrn_intra(qd, kd, v):
    n = qd.shape[0]
    ri = lax.broadcasted_iota(I32, (n, n), 0)
    ci = lax.broadcasted_iota(I32, (n, n), 1)
    keep = ((ri // REC_CHUNK) == (ci // REC_CHUNK)) & (ri >= ci)
    sc = lax.dot_general(qd, kd, (((1,), (1,)), ((), ())), preferred_element_type=F32)
    sc = jnp.where(keep, sc, 0.0)
    return jnp.dot(sc.astype(BF16), v, preferred_element_type=F32)


def _hgrn_kv_t(v, ke):
    return lax.dot_general(v, ke, (((0,), (0,)), ((), ())), preferred_element_type=F32)


def _hgrn_finish(o, g, norm_g):
    y = o * lax.rsqrt(jnp.mean(o * o, axis=-1, keepdims=True) + NORM_EPS) * norm_g
    return (y * (g * jax.nn.sigmoid(g))).astype(BF16)


def _hgrn_meta_kernel(qd_ref, kd_ref, ke_ref, v_ref, rg_ref, ng_ref, st_ref, rec_ref, *, nbatch):
    rec_ref[...] = jnp.zeros_like(rec_ref)
    nm = nbatch * REC_CHUNK
    for h in range(N_REC_HEADS):
        cols = slice(h * REC_DIM, (h + 1) * REC_DIM)
        o = _hgrn_intra(qd_ref[0:nm, cols], kd_ref[0:nm, cols], v_ref[0:nm, cols])
        rec_ref[0:nm, cols] = _hgrn_finish(o, rg_ref[0:nm, cols], ng_ref[...])
        for b in range(nbatch):
            rows = slice(b * REC_CHUNK, (b + 1) * REC_CHUNK)
            st_ref[b, h] = _hgrn_kv_t(v_ref[rows, cols], ke_ref[rows, cols])


def _hgrn_scan_kernel(*refs, gb):
    qd, kd, ke, v, eb, rg = (refs[i * gb:(i + 1) * gb] for i in range(6))
    ng_ref, st0_ref, rec_ref, st_scr, o_scr = refs[6 * gb:]
    seg = rec_ref.shape[1]
    heads = [slice(h * REC_DIM, (h + 1) * REC_DIM) for h in range(N_REC_HEADS)]

    @pl.when(pl.program_id(1) == 0)
    def _():
        st_scr[...] = st0_ref[...]

    for i in range(gb):
        for cols in heads:
            o_scr[i, :, cols] = _hgrn_intra(qd[i][:, cols], kd[i][:, cols], v[i][:, cols])

    def chunk(c, carry):
        r0 = pl.multiple_of(c * REC_CHUNK, REC_CHUNK)
        rows = pl.ds(r0, REC_CHUNK)
        for i in range(gb):
            for h, cols in enumerate(heads):
                last8 = eb[i][pl.ds(pl.multiple_of(r0 + REC_CHUNK - 8, 8), 8), cols]
                dec = last8[7:8]
                st = st_scr[i, h]
                o_scr[i, rows, cols] += lax.dot_general(qd[i][rows, cols], st.astype(BF16),
                                                        (((1,), (1,)), ((), ())),
                                                        preferred_element_type=F32)
                st_scr[i, h] = st * dec + _hgrn_kv_t(v[i][rows, cols], ke[i][rows, cols])
        return carry

    lax.fori_loop(0, seg // REC_CHUNK, chunk, 0, unroll=2)
    for i in range(gb):
        for cols in heads:
            rec_ref[i, :, cols] = _hgrn_finish(o_scr[i, :, cols], rg[i][:, cols], ng_ref[...])


def _hgrn(rest, lbl, norm_g, B, S):
    R = rest.shape[0]
    NS = B * S
    W = REC_W
    tail = R - NS
    tb = NS // tail
    qd, kd, ke, v, eb = _hgrn_prep(rest, lbl)

    tail_spec = pl.BlockSpec((tail, W), lambda i: (tb, 0))
    state, rec_tail = pl.pallas_call(
        functools.partial(_hgrn_meta_kernel, nbatch=B),
        grid=(1,),
        in_specs=[tail_spec, tail_spec, tail_spec, tail_spec,
                  pl.BlockSpec((tail, W), lambda i: (tb, 3)),
                  pl.BlockSpec((1, REC_DIM), lambda i: (0, 0))],
        out_specs=[pl.BlockSpec((B, N_REC_HEADS, REC_DIM, REC_DIM), lambda i: (0, 0, 0, 0)),
                   pl.BlockSpec((tail, W), lambda i: (0, 0))],
        out_shape=[jax.ShapeDtypeStruct((B, N_REC_HEADS, REC_DIM, REC_DIM), F32),
                   jax.ShapeDtypeStruct((tail, W), BF16)],
        compiler_params=_cparams(("arbitrary",), VMEM_LIMIT),
        name="hgrn_meta",
    )(qd, kd, ke, v, rest, norm_g)

    gb = SCAN_BATCHES
    ns = S // SCAN_SEG

    def seg_specs(col):
        return [pl.BlockSpec((SCAN_SEG, W), lambda g, s, i=i: ((g * gb + i) * ns + s, col))
                for i in range(gb)]

    rec = pl.pallas_call(
        functools.partial(_hgrn_scan_kernel, gb=gb),
        grid=(B // gb, ns),
        in_specs=seg_specs(0) * 5 + seg_specs(3)
        + [pl.BlockSpec((1, REC_DIM), lambda g, s: (0, 0)),
           pl.BlockSpec((gb, N_REC_HEADS, REC_DIM, REC_DIM), lambda g, s: (g, 0, 0, 0))],
        out_specs=pl.BlockSpec((gb, SCAN_SEG, W), lambda g, s: (g, s, 0)),
        out_shape=jax.ShapeDtypeStruct((B, S, W), BF16),
        scratch_shapes=[pltpu.VMEM((gb, N_REC_HEADS, REC_DIM, REC_DIM), F32),
                        pltpu.VMEM((gb, SCAN_SEG, W), F32)],
        compiler_params=_cparams(("parallel", "arbitrary"), VMEM_LIMIT),
        name="hgrn_scan",
    )(*([qd] * gb + [kd] * gb + [ke] * gb + [v] * gb + [eb] * gb + [rest] * gb), norm_g, state)
    return rec.reshape(NS, W), rec_tail


def _mix_kernel(attn_ref, attn_t_ref, rec_ref, rec_t_ref, ga_ref, gr_ref, x_ref, tail_ref,
                wa_ref, wr_ref, wo_ref, g_ref, wrt_ref, brt_ref, h2_ref, xn_ref, lg_ref, *, n_seq_tiles):
    is_seq = pl.program_id(0) < n_seq_tiles
    attn = jnp.where(is_seq, attn_ref[...], attn_t_ref[...])
    rec = jnp.where(is_seq, rec_ref[...], rec_t_ref[...])
    h = jnp.where(is_seq, x_ref[...], tail_ref[...])
    a = jnp.dot(attn, wa_ref[...], preferred_element_type=F32)
    r = jnp.dot(rec, wr_ref[...], preferred_element_type=F32)
    mixed = jax.nn.sigmoid(ga_ref[...]) * a + jax.nn.sigmoid(gr_ref[...]) * r
    h2 = h + jnp.dot(mixed.astype(BF16), wo_ref[...], preferred_element_type=F32)
    h2_ref[...] = h2
    xn = _rms(h2, g_ref[...])
    xn_ref[...] = xn
    xn_hi = xn.astype(BF16)
    xn_lo = (xn - xn_hi.astype(F32)).astype(BF16)
    w2 = wrt_ref[...]
    p_hi = jnp.dot(xn_hi, w2, preferred_element_type=F32)
    p_lo = jnp.dot(xn_lo, w2[:, :LANES], preferred_element_type=F32)
    lg_ref[...] = p_hi[:, :LANES] + p_hi[:, LANES:] + p_lo + brt_ref[...]


def _mix(attn, attn_tail, rec, rec_tail, rest, x2d, tail_rows, wa, wr, wo, g, w_router, b_router):
    NS, D = x2d.shape
    R = NS + tail_rows.shape[0]
    tm = MIX_TILE
    nst = NS // tm
    const = lambda shape: pl.BlockSpec(shape, lambda i: (0, 0), pipeline_mode=pl.Buffered(1))
    row = lambda w: pl.BlockSpec((tm, w), lambda i: (i, 0))
    seq = lambda w: pl.BlockSpec((tm, w), lambda i: (jnp.minimum(i, nst - 1), 0))
    tl = lambda w: pl.BlockSpec((tm, w), lambda i: (jnp.maximum(i - nst, 0), 0))
    return pl.pallas_call(
        functools.partial(_mix_kernel, n_seq_tiles=nst),
        grid=(R // tm,),
        in_specs=[seq(Q_W), tl(Q_W), seq(REC_W), tl(REC_W),
                  pl.BlockSpec((tm, D), lambda i: (i, 2)), pl.BlockSpec((tm, D), lambda i: (i, 3)),
                  seq(D), tl(D), const(wa.shape), const(wr.shape), const(wo.shape), const(g.shape),
                  const(w_router.shape), const(b_router.shape)],
        out_specs=[row(D), row(D), row(LANES)],
        out_shape=[jax.ShapeDtypeStruct((R, D), F32), jax.ShapeDtypeStruct((R, D), F32),
                   jax.ShapeDtypeStruct((R, LANES), F32)],
        compiler_params=_cparams(("parallel",), VMEM_LIMIT),
        name="mix_outproj",
    )(attn, attn_tail, rec, rec_tail, rest, rest, x2d, tail_rows, wa, wr, wo, g, w_router, b_router)


def _route_kernel(lg_ref, dest_ref, gate_ref, cnt_ref, cnt_scr, carry_scr, start_scr, *, n_tok):
    ph = pl.program_id(0)
    i = pl.program_id(1)
    tm = lg_ref.shape[0]

    @pl.when((ph == 0) & (i == 0))
    def _():
        cnt_scr[...] = jnp.zeros_like(cnt_scr)

    lane = lax.broadcasted_iota(I32, (tm, LANES), 1)
    valid = (i * tm + lax.broadcasted_iota(I32, (tm, LANES), 0)) < n_tok
    work = lg_ref[...]
    onehots, vals = [], []
    for _ in range(TOP_K):
        m = jnp.max(work, axis=-1, keepdims=True)
        idx = jnp.min(jnp.where(work == m, lane, LANES), axis=-1, keepdims=True)
        oh = lane == idx
        onehots.append(oh)
        vals.append(m)
        work = jnp.where(oh, -jnp.inf, work)
    multi = jnp.zeros((tm, LANES), F32)
    for oh in onehots:
        multi = multi + jnp.where(oh & valid, 1.0, 0.0)
    tile_cnt = jnp.sum(multi, axis=0, keepdims=True)

    @pl.when(ph == 0)
    def _():
        cnt_scr[...] += tile_cnt

    @pl.when(ph == 1)
    def _():
        @pl.when(i == 0)
        def _():
            c = cnt_scr[...]
            padded = jnp.ceil(c * (1.0 / MOE_BLK)) * MOE_BLK
            before = (lax.broadcasted_iota(I32, (LANES, LANES), 0)
                      < lax.broadcasted_iota(I32, (LANES, LANES), 1))
            start = jnp.dot(jnp.broadcast_to(padded, (8, LANES)), jnp.where(before, 1.0, 0.0),
                            preferred_element_type=F32, precision=lax.Precision.HIGHEST)
            start_scr[...] = start[0:1]
            carry_scr[...] = jnp.zeros_like(carry_scr)
            cnt_ref[...] = c

        earlier = (lax.broadcasted_iota(I32, (tm, tm), 1) < lax.broadcasted_iota(I32, (tm, tm), 0))
        prefix = jnp.dot(jnp.where(earlier, 1.0, 0.0).astype(BF16), multi.astype(BF16),
                         preferred_element_type=F32)
        base = prefix + carry_scr[...] + start_scr[...]
        den = jnp.zeros_like(vals[0])
        for v in vals:
            den = den + jnp.exp(v - vals[0])
        dest = jnp.zeros((tm, LANES), F32)
        gate = jnp.zeros((tm, LANES), F32)
        for k in range(TOP_K):
            d_k = jnp.sum(jnp.where(onehots[k], base, 0.0), axis=-1, keepdims=True)
            dest = jnp.where(lane == k, d_k, dest)
            gate = jnp.where(lane == k, jnp.exp(vals[k] - vals[0]) / den, gate)
        dest_ref[...] = dest.astype(I32)
        gate_ref[...] = gate
        carry_scr[...] += tile_cnt


def _route(logits, n_tok):
    R = logits.shape[0]
    tm = ROW_TILE
    blk = pl.BlockSpec((tm, LANES), lambda p, i: (i * p, 0))
    return pl.pallas_call(
        functools.partial(_route_kernel, n_tok=n_tok),
        grid=(2, R // tm),
        in_specs=[pl.BlockSpec((tm, LANES), lambda p, i: (i, 0))],
        out_specs=[blk, blk, pl.BlockSpec((1, LANES), lambda p, i: (0, 0))],
        out_shape=[jax.ShapeDtypeStruct((R, LANES), I32), jax.ShapeDtypeStruct((R, LANES), F32),
                   jax.ShapeDtypeStruct((1, LANES), F32)],
        scratch_shapes=[pltpu.VMEM((1, LANES), F32)] * 3,
        compiler_params=_cparams(("arbitrary", "arbitrary"), VMEM_LIMIT),
        name="route",
    )(logits)


def _dispatch_kernel(dest_ref, xn_ref, xs_ref, sem):
    def row_copy(r, k):
        return pltpu.make_async_copy(xn_ref.at[pl.ds(r, 1), :],
                                     xs_ref.at[pl.ds(dest_ref[r * TOP_K + k], 1), :], sem)

    def issue(r, carry):
        for k in range(TOP_K):
            row_copy(r, k).start()
        return carry

    lax.fori_loop(0, TOK_TILE, issue, 0)
    for _ in range(TOP_K):
        pltpu.make_async_copy(xn_ref, xs_ref.at[pl.ds(0, TOK_TILE), :], sem).wait()


def _dispatch(xn, dest_flat, n_tok, n_slots):
    D = xn.shape[1]
    return pl.pallas_call(
        _dispatch_kernel,
        grid=(n_tok // TOK_TILE,),
        in_specs=[pl.BlockSpec((TOK_TILE * TOP_K,), lambda i: (i,), memory_space=pltpu.SMEM),
                  pl.BlockSpec((TOK_TILE, D), lambda i: (i, 0))],
        out_specs=pl.BlockSpec(memory_space=pl.ANY),
        out_shape=jax.ShapeDtypeStruct((n_slots, D), F32),
        scratch_shapes=[pltpu.SemaphoreType.DMA],
        compiler_params=_cparams(("arbitrary",), VMEM_LIMIT),
        name="dispatch",
    )(dest_flat, xn)


class _WeightStream:
    def __init__(self, w_refs, stage, wbuf, sems, unit_rows):
        self.w_refs, self.stage, self.wbuf, self.sems = w_refs, stage, wbuf, sems
        self.unit_rows = unit_rows
        self.n_units = w_refs[0].shape[1] // unit_rows
        self.n_stage = stage.shape[0]

    def _copy(self, e, u, m):
        rows = pl.ds(pl.multiple_of(u * self.unit_rows, self.unit_rows), self.unit_rows)
        st = u % self.n_stage
        return pltpu.make_async_copy(self.w_refs[m].at[e, rows, :], self.stage.at[st, m],
                                     self.sems.at[st, m])

    def start(self, e, u):
        for m in range(len(self.w_refs)):
            self._copy(e, u, m).start()

    def finish(self, e, u, slot):
        rows = pl.ds(pl.multiple_of(u * self.unit_rows, self.unit_rows), self.unit_rows)
        for m in range(len(self.w_refs)):
            self._copy(e, u, m).wait()
            self.wbuf[slot, m, rows, :] = self.stage[u % self.n_stage, m].astype(BF16)

    def prime(self, e):
        for u in range(self.n_stage):
            self.start(e, u)

    def convert(self, e, slot, lo, hi):
        def body(u, carry):
            self.finish(e, u, slot)

            @pl.when(u + self.n_stage < self.n_units)
            def _():
                self.start(e, u + self.n_stage)

            return carry

        lax.fori_loop(lo, hi, body, 0)


def _moe_step(plan, stream, compute):
    be_ref, nxt_ref, slot_ref, first_ref, ulo_ref, uhi_ref, nu_ref = plan
    blk = pl.program_id(0)

    @pl.when(blk < nu_ref[0])
    def _():
        e = be_ref[blk]
        ne = nxt_ref[blk]
        slot = slot_ref[blk]

        @pl.when(blk == 0)
        def _():
            stream.prime(e)
            stream.convert(e, slot, 0, stream.n_units)

        @pl.when((first_ref[blk] == 1) & (ne >= 0))
        def _():
            stream.prime(ne)

        compute(slot)

        @pl.when(ne >= 0)
        def _():
            stream.convert(ne, 1 - slot, ulo_ref[blk], uhi_ref[blk])


def _moe_up_kernel(be_ref, nxt_ref, slot_ref, first_ref, ulo_ref, uhi_ref, nu_ref,
                   xs_ref, wg_ref, wu_ref, bg_ref, bu_ref, act_ref, wbuf, stage, sems):
    stream = _WeightStream((wg_ref, wu_ref), stage, wbuf, sems, MOE_UNIT_UP)
    half = act_ref.shape[1] // 2

    def compute(slot):
        xb = xs_ref[...].astype(BF16)
        for c in range(2):
            cols = slice(c * half, (c + 1) * half)
            gate = jnp.dot(xb, wbuf[slot, 0, :, cols], preferred_element_type=F32) + bg_ref[:, cols]
            up = jnp.dot(xb, wbuf[slot, 1, :, cols], preferred_element_type=F32) + bu_ref[:, cols]
            gate = jnp.minimum(gate, SWIGLU_LIMIT)
            up = jnp.clip(up, -SWIGLU_LIMIT, SWIGLU_LIMIT)
            act = gate * jax.nn.sigmoid(SWIGLU_ALPHA * gate) * (up + 1.0)
            act_ref[:, cols] = act.astype(act_ref.dtype)

    _moe_step((be_ref, nxt_ref, slot_ref, first_ref, ulo_ref, uhi_ref, nu_ref), stream, compute)


def _moe_down_kernel(be_ref, nxt_ref, slot_ref, first_ref, ulo_ref, uhi_ref, nu_ref,
                     act_ref, wd_ref, bd_ref, y_ref, wbuf, stage, sems):
    stream = _WeightStream((wd_ref,), stage, wbuf, sems, MOE_UNIT_DOWN)

    def compute(slot):
        y_ref[...] = jnp.dot(act_ref[...], wbuf[slot, 0], preferred_element_type=F32) + bd_ref[...]

    _moe_step((be_ref, nxt_ref, slot_ref, first_ref, ulo_ref, uhi_ref, nu_ref), stream, compute)


def _moe_plan(cnt, n_blk, n_units):
    E = cnt.shape[0]
    nblk_e = (cnt + MOE_BLK - 1) // MOE_BLK
    blk_end = jnp.cumsum(nblk_e)
    blk_first = blk_end - nblk_e
    j = jnp.arange(n_blk, dtype=I32)
    be = jnp.minimum(jnp.sum((blk_end[None, :] <= j[:, None]).astype(I32), axis=1), E - 1)
    k = j - blk_first[be]
    nb = jnp.maximum(nblk_e[be], 1)
    nonempty = nblk_e > 0
    ids = jnp.arange(E, dtype=I32)
    later = (ids[None, :] > ids[:, None]) & nonempty[None, :]
    nxt_e = jnp.min(jnp.where(later, ids[None, :], E), axis=1)
    nxt_e = jnp.where(nxt_e >= E, -1, nxt_e)
    slot_e = (jnp.cumsum(nonempty.astype(I32)) - 1) % 2
    plan = (be, nxt_e[be], slot_e[be], (k == 0).astype(I32),
            (k * n_units) // nb, ((k + 1) * n_units) // nb, blk_end[-1:].astype(I32))
    return tuple(a.astype(I32) for a in plan)


def _moe(xs, cnt, w_gate, b_gate, w_up, b_up, w_down, b_down):
    P, D = xs.shape
    E, _, F = w_gate.shape
    n_blk = P // MOE_BLK
    n_stage = 2
    clamp = lambda blk, nu: jnp.maximum(jnp.minimum(blk, nu[0] - 1), 0)
    row_map = lambda blk, be, nx, sl, fi, lo, hi, nu: (clamp(blk, nu), 0)
    exp_map = lambda blk, be, nx, sl, fi, lo, hi, nu: (be[clamp(blk, nu)], 0, 0)
    hbm = pl.BlockSpec(memory_space=pl.ANY)

    plan_up = _moe_plan(cnt, n_blk, D // MOE_UNIT_UP)
    act = pl.pallas_call(
        _moe_up_kernel,
        grid_spec=pltpu.PrefetchScalarGridSpec(
            num_scalar_prefetch=7,
            grid=(n_blk,),
            in_specs=[pl.BlockSpec((MOE_BLK, D), row_map), hbm, hbm,
                      pl.BlockSpec((None, 1, F), exp_map), pl.BlockSpec((None, 1, F), exp_map)],
            out_specs=pl.BlockSpec((MOE_BLK, F), row_map),
            scratch_shapes=[pltpu.VMEM((2, 2, D, F), BF16),
                            pltpu.VMEM((n_stage, 2, MOE_UNIT_UP, F), F32),
                            pltpu.SemaphoreType.DMA((n_stage, 2))],
        ),
        out_shape=jax.ShapeDtypeStruct((P, F), BF16),
        compiler_params=_cparams(("arbitrary",), VMEM_LIMIT),
        name="moe_up",
    )(*plan_up, xs, w_gate, w_up, b_gate.reshape(E, 1, F), b_up.reshape(E, 1, F))

    plan_down = _moe_plan(cnt, n_blk, F // MOE_UNIT_DOWN)
    return pl.pallas_call(
        _moe_down_kernel,
        grid_spec=pltpu.PrefetchScalarGridSpec(
            num_scalar_prefetch=7,
            grid=(n_blk,),
            in_specs=[pl.BlockSpec((MOE_BLK, F), row_map), hbm, pl.BlockSpec((None, 1, D), exp_map)],
            out_specs=pl.BlockSpec((MOE_BLK, D), row_map),
            scratch_shapes=[pltpu.VMEM((2, 1, F, D), BF16),
                            pltpu.VMEM((n_stage, 1, MOE_UNIT_DOWN, D), F32),
                            pltpu.SemaphoreType.DMA((n_stage, 1))],
        ),
        out_shape=jax.ShapeDtypeStruct((P, D), F32),
        compiler_params=_cparams(("arbitrary",), VMEM_LIMIT),
        name="moe_down",
    )(*plan_down, act, w_down, b_down.reshape(E, 1, D))


def _combine_kernel(dest_ref, dnext_ref, h2_ref, gate_ref, g_ref, y_ref, o_ref, ybuf, sem):
    i = pl.program_id(0)
    slot = i % 2

    def issue(dref, s):
        def body(r, carry):
            for k in range(TOP_K):
                pltpu.make_async_copy(y_ref.at[pl.ds(dref[r * TOP_K + k], 1), :],
                                      ybuf.at[s, k, pl.ds(r, 1), :], sem.at[s]).start()
            return carry

        lax.fori_loop(0, TOK_TILE, body, 0)

    @pl.when(i == 0)
    def _():
        issue(dest_ref, 0)

    @pl.when(i + 1 < pl.num_programs(0))
    def _():
        issue(dnext_ref, 1 - slot)

    for k in range(TOP_K):
        pltpu.make_async_copy(y_ref.at[pl.ds(0, TOK_TILE), :], ybuf.at[slot, k], sem.at[slot]).wait()
    gates = gate_ref[...]
    acc = h2_ref[...]
    for k in range(TOP_K):
        acc = acc + gates[:, k:k + 1] * ybuf[slot, k]
    o_ref[...] = _rms(acc, g_ref[...])


def _combine(h2, gates, dest_flat, y, g, n_seq):
    D = h2.shape[1]
    n_tiles = n_seq // TOK_TILE
    return pl.pallas_call(
        _combine_kernel,
        grid=(n_tiles,),
        in_specs=[pl.BlockSpec((TOK_TILE * TOP_K,), lambda i: (i,), memory_space=pltpu.SMEM),
                  pl.BlockSpec((TOK_TILE * TOP_K,), lambda i: (jnp.minimum(i + 1, n_tiles - 1),),
                               memory_space=pltpu.SMEM),
                  pl.BlockSpec((TOK_TILE, D), lambda i: (i, 0)),
                  pl.BlockSpec((TOK_TILE, LANES), lambda i: (i, 0)),
                  pl.BlockSpec((1, D), lambda i: (0, 0)),
                  pl.BlockSpec(memory_space=pl.ANY)],
        out_specs=pl.BlockSpec((TOK_TILE, D), lambda i: (i, 0)),
        out_shape=jax.ShapeDtypeStruct((n_seq, D), F32),
        scratch_shapes=[pltpu.VMEM((2, TOP_K, TOK_TILE, D), F32), pltpu.SemaphoreType.DMA((2,))],
        compiler_params=_cparams(("arbitrary",), VMEM_LIMIT),
        name="combine_norm",
    )(dest_flat, dest_flat, h2, gates, g, y)


def _rope_tables(S, n_meta_rows):
    half = HEAD_DIM // 2
    inv_freq = ROPE_THETA ** (-jnp.arange(half, dtype=F32) / half)
    lane = np.arange(LANES)
    fidx = lane % half
    sign = np.where((lane % HEAD_DIM) < half, -1.0, 1.0).astype(np.float32)
    pos_seq = N_META_TOK + jnp.arange(S)
    tail = jnp.arange(ROW_TILE)
    pos_tail = jnp.where(tail < n_meta_rows, tail % N_META_TOK, 0)
    pos = jnp.concatenate([pos_seq, pos_tail]).astype(F32)
    ang = (pos[:, None] * inv_freq[None, :])[:, fidx]
    return jnp.cos(ang), jnp.sin(ang) * sign[None, :]


def kernel(x, meta_tokens, attn_norm_g, w_in, attn_sinks, lower_bound_logits, rec_norm_g,
           w_attn_proj, w_rec_proj, w_out, ffn_norm_g, w_router, b_router, w_gate, b_gate,
           w_up, b_up, w_down, b_down, final_norm_g):
    B, S, D = x.shape
    assert w_in.shape[0] == 1, "single-layer block"
    assert S % ROW_TILE == 0 and (B * S) % ROW_TILE == 0 and S % SEG == 0
    assert S % SCAN_SEG == 0 and B % SCAN_BATCHES == 0
    NS = B * S
    NM = B * N_META_TOK
    NT = NS + NM
    R = -(-NT // ROW_TILE) * ROW_TILE
    assert R - NS == ROW_TILE and NT % TOK_TILE == 0 and NS % TOK_TILE == 0

    x2d = x.reshape(NS, D)
    tail_rows = jnp.concatenate([jnp.tile(meta_tokens.astype(x.dtype), (B, 1)),
                                 jnp.zeros((R - NT, D), x.dtype)], axis=0)

    w0 = w_in[0]
    kw = N_KV_HEADS * HEAD_DIM
    wq, wk, wv = w0[:, :Q_W], w0[:, Q_W:Q_W + kw], w0[:, Q_W + kw:Q_W + 2 * kw]
    dup = lambda w: jnp.concatenate([w[:, :HEAD_DIM]] * 2 + [w[:, HEAD_DIM:]] * 2, axis=1)
    w_qkv = jnp.concatenate([wq, dup(wk), dup(wv)], axis=1).astype(BF16)
    w_rest = w0[:, Q_W + 2 * kw:].astype(BF16)

    cos_t, sin_t = _rope_tables(S, NM)
    u, qkv = _norm_qkv(x2d, tail_rows, attn_norm_g[0].reshape(1, D), w_qkv, cos_t, sin_t,
                       S // ROW_TILE)
    rest = _proj_rest(u, w_rest)

    attn, attn_tail = _attention(qkv, attn_sinks[0], B, S)
    rec, rec_tail = _hgrn(rest, lower_bound_logits, rec_norm_g[0].reshape(1, REC_DIM), B, S)

    wr_hi = w_router[0].astype(BF16)
    wr_lo = (w_router[0] - wr_hi.astype(F32)).astype(BF16)
    lane_pad = ((0, 0), (0, LANES - N_EXPERTS))
    wr_pad = jnp.concatenate([jnp.pad(wr_hi, lane_pad), jnp.pad(wr_lo, lane_pad)], axis=1)
    br_pad = jnp.pad(b_router[0].reshape(1, N_EXPERTS), ((0, 0), (0, LANES - N_EXPERTS)),
                     constant_values=NEG_INF)
    h2, xn, logits = _mix(attn, attn_tail, rec, rec_tail, rest, x2d, tail_rows,
                          w_attn_proj[0].astype(BF16), w_rec_proj[0].astype(BF16),
                          w_out[0].astype(BF16), ffn_norm_g[0].reshape(1, D), wr_pad, br_pad)

    dest, gates, counts = _route(logits, NT)
    cnt = counts[0, :N_EXPERTS].astype(I32)
    n_blk = -(-(NT * TOP_K + N_EXPERTS * (MOE_BLK - 1)) // MOE_BLK)
    dest_flat = dest[:NT, :TOP_K].reshape(NT * TOP_K)

    xs = _dispatch(xn, dest_flat, NT, n_blk * MOE_BLK)
    y = _moe(xs, cnt, w_gate[0], b_gate[0], w_up[0], b_up[0], w_down[0], b_down[0])
    out = _combine(h2, gates, dest_flat, y, final_norm_g.reshape(1, D), NS)
    return out.reshape(B, S, D)
```

```python
import functools

import jax
import jax.numpy as jnp
import numpy as np
from jax import lax
from jax.experimental import pallas as pl
from jax.experimental.pallas import tpu as pltpu

F32 = jnp.float32
BF16 = jnp.bfloat16
I32 = jnp.int32

N_META_TOK = 16
HEAD_DIM = 64
N_Q_HEADS = 16
N_KV_HEADS = 2
ATTN_BLK = 128
ROPE_THETA = 10000.0
N_REC_HEADS = 8
REC_DIM = 128
REC_CHUNK = 16
N_EXPERTS = 32
TOP_K = 4
SWIGLU_ALPHA = 1.702
SWIGLU_LIMIT = 7.0
NORM_EPS = 1e-5
NEG_INF = -1e30

LANES = 128
ROW_TILE = 512
MIX_TILE = 256
SEG = 256
SCAN_SEG = 256
SCAN_BATCHES = 2
MOE_BLK = 256
MOE_UNIT_UP = 128
MOE_UNIT_DOWN = 256
MOE_STAGES = 3
TOK_TILE = 128
VMEM_LIMIT = 56 * 1024 * 1024

Q_W = N_Q_HEADS * HEAD_DIM
REC_W = N_REC_HEADS * REC_DIM


def _cparams(sem, vmem=None, **kw):
    return pltpu.CompilerParams(dimension_semantics=sem, vmem_limit_bytes=vmem, **kw)


def _rms(xf, g):
    return xf * lax.rsqrt(jnp.mean(xf * xf, axis=-1, keepdims=True) + NORM_EPS) * g


def _for_rows(n, fn, unrolled):
    if unrolled:
        for r in range(n):
            fn(r)
    else:
        def body(r, carry):
            fn(r)
            return carry

        lax.fori_loop(0, n, body, 0)


def _norm_qkv_kernel(x_ref, tail_ref, g_ref, w_ref, cos_ref, sin_ref, u_ref, qkv_ref, *, n_seq_tiles):
    h = jnp.where(pl.program_id(0) < n_seq_tiles, x_ref[...], tail_ref[...])
    u = _rms(h, g_ref[...]).astype(BF16)
    u_ref[...] = u
    p = jnp.dot(u, w_ref[...], preferred_element_type=F32)
    cos = cos_ref[...]
    sin = sin_ref[...]
    lane = lax.broadcasted_iota(I32, cos.shape, 1)
    first_half = (lane % HEAD_DIM) < (HEAD_DIM // 2)
    n_q = Q_W // LANES
    for c in range(n_q + 2):
        xs = p[:, c * LANES:(c + 1) * LANES]
        swapped = jnp.where(first_half, pltpu.roll(xs, LANES - HEAD_DIM // 2, 1),
                            pltpu.roll(xs, HEAD_DIM // 2, 1))
        r = xs * cos + swapped * sin
        if c < n_q:
            r = r * (HEAD_DIM ** -0.5)
        qkv_ref[:, c * LANES:(c + 1) * LANES] = r.astype(BF16)
    v0 = (n_q + 2) * LANES
    qkv_ref[:, v0:] = p[:, v0:].astype(BF16)


def _norm_qkv(x2d, tail_rows, g, w_qkv, cos_t, sin_t, tiles_per_seq):
    NS, D = x2d.shape
    R = NS + tail_rows.shape[0]
    W = w_qkv.shape[1]
    n_seq_tiles = NS // ROW_TILE

    def tab_map(i):
        return (jnp.where(i < n_seq_tiles, i % tiles_per_seq, tiles_per_seq), 0)

    return pl.pallas_call(
        functools.partial(_norm_qkv_kernel, n_seq_tiles=n_seq_tiles),
        grid=(R // ROW_TILE,),
        in_specs=[
            pl.BlockSpec((ROW_TILE, D), lambda i: (jnp.minimum(i, n_seq_tiles - 1), 0)),
            pl.BlockSpec((ROW_TILE, D), lambda i: (0, 0)),
            pl.BlockSpec((1, D), lambda i: (0, 0)),
            pl.BlockSpec((D, W), lambda i: (0, 0)),
            pl.BlockSpec((ROW_TILE, LANES), tab_map),
            pl.BlockSpec((ROW_TILE, LANES), tab_map),
        ],
        out_specs=[
            pl.BlockSpec((ROW_TILE, D), lambda i: (i, 0)),
            pl.BlockSpec((ROW_TILE, W), lambda i: (i, 0)),
        ],
        out_shape=[jax.ShapeDtypeStruct((R, D), BF16), jax.ShapeDtypeStruct((R, W), BF16)],
        compiler_params=_cparams(("parallel",), VMEM_LIMIT),
        name="norm_qkv",
    )(x2d, tail_rows, g, w_qkv, cos_t, sin_t)


def _matmul_kernel(x_ref, w_ref, o_ref):
    o_ref[...] = jnp.dot(x_ref[...], w_ref[...], preferred_element_type=F32).astype(o_ref.dtype)


def _proj_rest(u, w):
    R, D = u.shape
    N = w.shape[1]
    tn = 2048
    return pl.pallas_call(
        _matmul_kernel,
        grid=(N // tn, R // ROW_TILE),
        in_specs=[pl.BlockSpec((ROW_TILE, D), lambda j, i: (i, 0)),
                  pl.BlockSpec((D, tn), lambda j, i: (0, j))],
        out_specs=pl.BlockSpec((ROW_TILE, tn), lambda j, i: (i, j)),
        out_shape=jax.ShapeDtypeStruct((R, N), F32),
        compiler_params=_cparams(("parallel", "parallel"), VMEM_LIMIT),
        name="proj_rest",
    )(u, w)


def _attn_core(q_ref, k_groups, v_groups, mask_fn, sink_ref, write):
    nkeys = k_groups[0].shape[0]
    rows = q_ref.shape[0]
    heads_per_group = N_Q_HEADS // N_KV_HEADS
    width = heads_per_group * rows
    lo_q = lax.broadcasted_iota(I32, (rows, LANES), 1) < HEAD_DIM
    ki = lax.broadcasted_iota(I32, (nkeys, width), 0)
    qi = lax.broadcasted_iota(I32, (nkeys, width), 1) % rows
    mask = mask_fn(ki, qi)
    top_half = lax.broadcasted_iota(I32, (LANES, rows), 0) < HEAD_DIM
    for g in range(N_KV_HEADS):
        parts = []
        for j in range(heads_per_group // 2):
            qp = q_ref[:, (heads_per_group // 2 * g + j) * LANES:(heads_per_group // 2 * g + j + 1) * LANES]
            zero_q = jnp.zeros_like(qp)
            parts += [jnp.where(lo_q, qp, zero_q), jnp.where(lo_q, zero_q, qp)]
        q_stack = jnp.concatenate(parts, axis=0)
        s = lax.dot_general(k_groups[g], q_stack, (((1,), (1,)), ((), ())),
                            preferred_element_type=F32)
        s = jnp.where(mask, s, NEG_INF)
        sk = sink_ref[:, g * width:(g + 1) * width]
        m = jnp.maximum(jnp.max(s, axis=0, keepdims=True), sk)
        e = jnp.exp(s - m)
        den = jnp.sum(e, axis=0, keepdims=True) + jnp.exp(sk - m)
        prob = (e * (1.0 / den)).astype(BF16)
        o_t = lax.dot_general(v_groups[g], prob, (((0,), (0,)), ((), ())), preferred_element_type=F32)
        for j in range(heads_per_group // 2):
            even = o_t[:, (2 * j) * rows:(2 * j + 1) * rows]
            odd = o_t[:, (2 * j + 1) * rows:(2 * j + 2) * rows]
            pair_t = jnp.where(top_half, even, odd)
            write(heads_per_group // 2 * g + j, pair_t.T)


def _attn_seq_kernel(sink_ref, q_ref, kc0, kc1, vc0, vc1, kp0, kp1, vp0, vp1,
                     km0, km1, vm0, vm1, o_ref):
    n = pl.program_id(1)
    no_prev = jnp.where(n > 0, 0, 2 * ATTN_BLK)

    def mask_fn(ki, qi):
        prev_ok = (ki >= N_META_TOK) & (ki < N_META_TOK + ATTN_BLK) & (ki - N_META_TOK > qi + no_prev)
        cur_ok = (ki >= N_META_TOK + ATTN_BLK) & (ki - (N_META_TOK + ATTN_BLK) <= qi)
        return (ki < N_META_TOK) | prev_ok | cur_ok

    k_groups = [jnp.concatenate([km[...], kp[...], kc[...]], axis=0)
                for km, kp, kc in ((km0, kp0, kc0), (km1, kp1, kc1))]
    v_groups = [jnp.concatenate([vm[...], vp[...], vc[...]], axis=0)
                for vm, vp, vc in ((vm0, vp0, vc0), (vm1, vp1, vc1))]

    def write(pr, acc):
        o_ref[:, pr * LANES:(pr + 1) * LANES] = acc.astype(o_ref.dtype)

    _attn_core(q_ref, k_groups, v_groups, mask_fn, sink_ref, write)


def _attn_meta_kernel(sink_ref, q_ref, k0, k1, v0, v1, o_ref):
    nm = q_ref.shape[0]

    def mask_fn(ki, qi):
        return ((qi // N_META_TOK) == (ki // N_META_TOK)) & ((ki % N_META_TOK) <= (qi % N_META_TOK))

    o_ref[...] = jnp.zeros_like(o_ref)

    def write(pr, acc):
        o_ref[0:nm, pr * LANES:(pr + 1) * LANES] = acc.astype(o_ref.dtype)

    _attn_core(q_ref, [k0[...], k1[...]], [v0[...], v1[...]], mask_fn, sink_ref, write)


def _attention(qkv, sinks, B, S):
    R = qkv.shape[0]
    NS = B * S
    NM = B * N_META_TOK
    nb = S // ATTN_BLK
    qc = Q_W // LANES
    sink_seq = jnp.repeat(sinks.astype(F32), ATTN_BLK).reshape(1, N_Q_HEADS * ATTN_BLK)
    sink_meta = jnp.repeat(sinks.astype(F32), NM).reshape(1, N_Q_HEADS * NM)

    def kv_spec(col, prev):
        if prev:
            return pl.BlockSpec((ATTN_BLK, LANES), lambda b, n: (b * nb + jnp.maximum(n - 1, 0), col))
        return pl.BlockSpec((ATTN_BLK, LANES), lambda b, n: (b * nb + n, col))

    def meta_spec(col):
        return pl.BlockSpec((N_META_TOK, LANES), lambda b, n: (NS // N_META_TOK + b, col))

    in_specs = [pl.BlockSpec(sink_seq.shape, lambda b, n: (0, 0)),
                pl.BlockSpec((ATTN_BLK, Q_W), lambda b, n: (b * nb + n, 0))]
    in_specs += [kv_spec(qc, False), kv_spec(qc + 1, False), kv_spec(qc + 2, False), kv_spec(qc + 3, False)]
    in_specs += [kv_spec(qc, True), kv_spec(qc + 1, True), kv_spec(qc + 2, True), kv_spec(qc + 3, True)]
    in_specs += [meta_spec(qc), meta_spec(qc + 1), meta_spec(qc + 2), meta_spec(qc + 3)]
    attn = pl.pallas_call(
        _attn_seq_kernel,
        grid=(B, nb),
        in_specs=in_specs,
        out_specs=pl.BlockSpec((ATTN_BLK, Q_W), lambda b, n: (b * nb + n, 0)),
        out_shape=jax.ShapeDtypeStruct((NS, Q_W), BF16),
        compiler_params=_cparams(("parallel", "parallel"), VMEM_LIMIT),
        name="attn_seq",
    )(sink_seq, *([qkv] * 13))

    mb = NS // NM
    tail = R - NS
    blk = lambda col: pl.BlockSpec((NM, LANES), lambda i: (mb, col))
    attn_tail = pl.pallas_call(
        _attn_meta_kernel,
        grid=(1,),
        in_specs=[pl.BlockSpec(sink_meta.shape, lambda i: (0, 0)),
                  pl.BlockSpec((NM, Q_W), lambda i: (mb, 0)),
                  blk(qc), blk(qc + 1), blk(qc + 2), blk(qc + 3)],
        out_specs=pl.BlockSpec((tail, Q_W), lambda i: (0, 0)),
        out_shape=jax.ShapeDtypeStruct((tail, Q_W), BF16),
        compiler_params=_cparams(("arbitrary",), VMEM_LIMIT),
        name="attn_meta",
    )(sink_meta, qkv, qkv, qkv, qkv, qkv)
    return attn, attn_tail


def _hgrn_prep_kernel(rq_ref, rf_ref, ri_ref, lbl_ref, qd_ref, kd_ref, ke_ref, v_ref, eb_ref):
    lbl = lbl_ref[...]
    e = jnp.exp(lbl - jnp.max(lbl, axis=0, keepdims=True))
    lb = e[0:1] / jnp.sum(e, axis=0, keepdims=True)
    f = lb + (1.0 - lb) * jax.nn.sigmoid(rf_ref[...])
    logf = jnp.log(f)
    k = 1.0 - f
    tm = logf.shape[0]
    ri = lax.broadcasted_iota(I32, (tm, tm), 0)
    ci = lax.broadcasted_iota(I32, (tm, tm), 1)
    same_chunk = (ri // REC_CHUNK) == (ci // REC_CHUNK)
    sel = jnp.concatenate([jnp.where(same_chunk & (ci <= ri), 1.0, 0.0).astype(BF16),
                           jnp.where(same_chunk, 1.0, 0.0).astype(BF16)], axis=0)
    hi = logf.astype(BF16)
    rem = logf - hi.astype(F32)
    mid = rem.astype(BF16)
    lo = (rem - mid.astype(F32)).astype(BF16)
    sums = (jnp.dot(sel, hi, preferred_element_type=F32) + jnp.dot(sel, mid, preferred_element_type=F32)
            + jnp.dot(sel, lo, preferred_element_type=F32))
    b = sums[:tm]
    total = sums[tm:]
    eb = jnp.exp(b)
    eb_ref[...] = eb
    qd_ref[...] = (rq_ref[...] * eb).astype(BF16)
    kd_ref[...] = (k * jnp.exp(-b)).astype(BF16)
    ke_ref[...] = (k * jnp.exp(total - b)).astype(BF16)
    v_ref[...] = ri_ref[...].astype(BF16)


def _hgrn_prep(rest, lbl):
    R = rest.shape[0]
    W = REC_W
    col = lambda c: pl.BlockSpec((SEG, W), lambda i: (i, c))
    row_out = pl.BlockSpec((SEG, W), lambda i: (i, 0))
    return pl.pallas_call(
        _hgrn_prep_kernel,
        grid=(R // SEG,),
        in_specs=[col(0), col(1), col(2), pl.BlockSpec(lbl.shape, lambda i: (0, 0))],
        out_specs=[row_out] * 5,
        out_shape=[jax.ShapeDtypeStruct((R, W), BF16)] * 4 + [jax.ShapeDtypeStruct((R, W), F32)],
        compiler_params=_cparams(("parallel",), VMEM_LIMIT),
        name="hgrn_prep",
    )(rest, rest, rest, lbl)


def _hgrn_intra(qd, kd, v):
    n = qd.shape[0]
    ri = lax.broadcasted_iota(I32, (n, n), 0)
    ci = lax.broadcasted_iota(I32, (n, n), 1)
    keep = ((ri // REC_CHUNK) == (ci // REC_CHUNK)) & (ri >= ci)
    sc = lax.dot_general(qd, kd, (((1,), (1,)), ((), ())), preferred_element_type=F32)
    sc = jnp.where(keep, sc, 0.0)
    return jnp.dot(sc.astype(BF16), v, preferred_element_type=F32)


def _hgrn_kv_t(v, ke):
    return lax.dot_general(v, ke, (((0,), (0,)), ((), ())), preferred_element_type=F32)


def _hgrn_finish(o, g, norm_g):
    y = o * lax.rsqrt(jnp.mean(o * o, axis=-1, keepdims=True) + NORM_EPS) * norm_g
    return (y * (g * jax.nn.sigmoid(g))).astype(BF16)


def _hgrn_meta_kernel(qd_ref, kd_ref, ke_ref, v_ref, rg_ref, ng_ref, st_ref, rec_ref, *, nbatch):
    rec_ref[...] = jnp.zeros_like(rec_ref)
    nm = nbatch * REC_CHUNK
    for h in range(N_REC_HEADS):
        cols = slice(h * REC_DIM, (h + 1) * REC_DIM)
        o = _hgrn_intra(qd_ref[0:nm, cols], kd_ref[0:nm, cols], v_ref[0:nm, cols])
        rec_ref[0:nm, cols] = _hgrn_finish(o, rg_ref[0:nm, cols], ng_ref[...])
        for b in range(nbatch):
            rows = slice(b * REC_CHUNK, (b + 1) * REC_CHUNK)
            st_ref[b, h] = _hgrn_kv_t(v_ref[rows, cols], ke_ref[rows, cols])


def _hgrn_scan_kernel(*refs, gb):
    qd, kd, ke, v, eb, rg = (refs[i * gb:(i + 1) * gb] for i in range(6))
    ng_ref, st0_ref, rec_ref, st_scr, o_scr = refs[6 * gb:]
    seg = rec_ref.shape[1]
    heads = [slice(h * REC_DIM, (h + 1) * REC_DIM) for h in range(N_REC_HEADS)]

    @pl.when(pl.program_id(1) == 0)
    def _():
        st_scr[...] = st0_ref[...]

    for i in range(gb):
        for cols in heads:
            o_scr[i, :, cols] = _hgrn_intra(qd[i][:, cols], kd[i][:, cols], v[i][:, cols])

    def chunk(c, carry):
        r0 = pl.multiple_of(c * REC_CHUNK, REC_CHUNK)
        rows = pl.ds(r0, REC_CHUNK)
        for i in range(gb):
            for h, cols in enumerate(heads):
                last8 = eb[i][pl.ds(pl.multiple_of(r0 + REC_CHUNK - 8, 8), 8), cols]
                dec = last8[7:8]
                st = st_scr[i, h]
                o_scr[i, rows, cols] += lax.dot_general(qd[i][rows, cols], st.astype(BF16),
                                                        (((1,), (1,)), ((), ())),
                                                        preferred_element_type=F32)
                st_scr[i, h] = st * dec + _hgrn_kv_t(v[i][rows, cols], ke[i][rows, cols])
        return carry

    lax.fori_loop(0, seg // REC_CHUNK, chunk, 0, unroll=2)
    for i in range(gb):
        for cols in heads:
            rec_ref[i, :, cols] = _hgrn_finish(o_scr[i, :, cols], rg[i][:, cols], ng_ref[...])


def _hgrn(rest, lbl, norm_g, B, S):
    R = rest.shape[0]
    NS = B * S
    W = REC_W
    tail = R - NS
    tb = NS // tail
    qd, kd, ke, v, eb = _hgrn_prep(rest, lbl)

    tail_spec = pl.BlockSpec((tail, W), lambda i: (tb, 0))
    state, rec_tail = pl.pallas_call(
        functools.partial(_hgrn_meta_kernel, nbatch=B),
        grid=(1,),
        in_specs=[tail_spec, tail_spec, tail_spec, tail_spec,
                  pl.BlockSpec((tail, W), lambda i: (tb, 3)),
                  pl.BlockSpec((1, REC_DIM), lambda i: (0, 0))],
        out_specs=[pl.BlockSpec((B, N_REC_HEADS, REC_DIM, REC_DIM), lambda i: (0, 0, 0, 0)),
                   pl.BlockSpec((tail, W), lambda i: (0, 0))],
        out_shape=[jax.ShapeDtypeStruct((B, N_REC_HEADS, REC_DIM, REC_DIM), F32),
                   jax.ShapeDtypeStruct((tail, W), BF16)],
        compiler_params=_cparams(("arbitrary",), VMEM_LIMIT),
        name="hgrn_meta",
    )(qd, kd, ke, v, rest, norm_g)

    gb = SCAN_BATCHES
    ns = S // SCAN_SEG

    def seg_specs(col):
        return [pl.BlockSpec((SCAN_SEG, W), lambda g, s, i=i: ((g * gb + i) * ns + s, col))
                for i in range(gb)]

    rec = pl.pallas_call(
        functools.partial(_hgrn_scan_kernel, gb=gb),
        grid=(B // gb, ns),
        in_specs=seg_specs(0) * 5 + seg_specs(3)
        + [pl.BlockSpec((1, REC_DIM), lambda g, s: (0, 0)),
           pl.BlockSpec((gb, N_REC_HEADS, REC_DIM, REC_DIM), lambda g, s: (g, 0, 0, 0))],
        out_specs=pl.BlockSpec((gb, SCAN_SEG, W), lambda g, s: (g, s, 0)),
        out_shape=jax.ShapeDtypeStruct((B, S, W), BF16),
        scratch_shapes=[pltpu.VMEM((gb, N_REC_HEADS, REC_DIM, REC_DIM), F32),
                        pltpu.VMEM((gb, SCAN_SEG, W), F32)],
        compiler_params=_cparams(("parallel", "arbitrary"), VMEM_LIMIT),
        name="hgrn_scan",
    )(*([qd] * gb + [kd] * gb + [ke] * gb + [v] * gb + [eb] * gb + [rest] * gb), norm_g, state)
    return rec.reshape(NS, W), rec_tail


def _mix_kernel(attn_ref, attn_t_ref, rec_ref, rec_t_ref, ga_ref, gr_ref, x_ref, tail_ref,
                wa_ref, wr_ref, wo_ref, g_ref, wrt_ref, brt_ref, h2_ref, xn_ref, lg_ref, *, n_seq_tiles):
    is_seq = pl.program_id(0) < n_seq_tiles
    attn = jnp.where(is_seq, attn_ref[...], attn_t_ref[...])
    rec = jnp.where(is_seq, rec_ref[...], rec_t_ref[...])
    h = jnp.where(is_seq, x_ref[...], tail_ref[...])
    a = jnp.dot(attn, wa_ref[...], preferred_element_type=F32)
    r = jnp.dot(rec, wr_ref[...], preferred_element_type=F32)
    mixed = jax.nn.sigmoid(ga_ref[...]) * a + jax.nn.sigmoid(gr_ref[...]) * r
    h2 = h + jnp.dot(mixed.astype(BF16), wo_ref[...], preferred_element_type=F32)
    h2_ref[...] = h2
    xn = _rms(h2, g_ref[...])
    xn_ref[...] = xn
    xn_hi = xn.astype(BF16)
    xn_lo = (xn - xn_hi.astype(F32)).astype(BF16)
    w2 = wrt_ref[...]
    p_hi = jnp.dot(xn_hi, w2, preferred_element_type=F32)
    p_lo = jnp.dot(xn_lo, w2[:, :LANES], preferred_element_type=F32)
    lg_ref[...] = p_hi[:, :LANES] + p_hi[:, LANES:] + p_lo + brt_ref[...]


def _mix(attn, attn_tail, rec, rec_tail, rest, x2d, tail_rows, wa, wr, wo, g, w_router, b_router):
    NS, D = x2d.shape
    R = NS + tail_rows.shape[0]
    tm = MIX_TILE
    nst = NS // tm
    const = lambda shape: pl.BlockSpec(shape, lambda i: (0, 0), pipeline_mode=pl.Buffered(1))
    row = lambda w: pl.BlockSpec((tm, w), lambda i: (i, 0))
    seq = lambda w: pl.BlockSpec((tm, w), lambda i: (jnp.minimum(i, nst - 1), 0))
    tl = lambda w: pl.BlockSpec((tm, w), lambda i: (jnp.maximum(i - nst, 0), 0))
    return pl.pallas_call(
        functools.partial(_mix_kernel, n_seq_tiles=nst),
        grid=(R // tm,),
        in_specs=[seq(Q_W), tl(Q_W), seq(REC_W), tl(REC_W),
                  pl.BlockSpec((tm, D), lambda i: (i, 2)), pl.BlockSpec((tm, D), lambda i: (i, 3)),
                  seq(D), tl(D), const(wa.shape), const(wr.shape), const(wo.shape), const(g.shape),
                  const(w_router.shape), const(b_router.shape)],
        out_specs=[row(D), row(D), row(LANES)],
        out_shape=[jax.ShapeDtypeStruct((R, D), F32), jax.ShapeDtypeStruct((R, D), F32),
                   jax.ShapeDtypeStruct((R, LANES), F32)],
        compiler_params=_cparams(("parallel",), VMEM_LIMIT),
        name="mix_outproj",
    )(attn, attn_tail, rec, rec_tail, rest, rest, x2d, tail_rows, wa, wr, wo, g, w_router, b_router)


def _route_kernel(lg_ref, dest_ref, gate_ref, cnt_ref, cnt_scr, carry_scr, start_scr, *, n_tok):
    ph = pl.program_id(0)
    i = pl.program_id(1)
    tm = lg_ref.shape[0]

    @pl.when((ph == 0) & (i == 0))
    def _():
        cnt_scr[...] = jnp.zeros_like(cnt_scr)

    lane = lax.broadcasted_iota(I32, (tm, LANES), 1)
    valid = (i * tm + lax.broadcasted_iota(I32, (tm, LANES), 0)) < n_tok
    work = lg_ref[...]
    onehots, vals = [], []
    for _ in range(TOP_K):
        m = jnp.max(work, axis=-1, keepdims=True)
        idx = jnp.min(jnp.where(work == m, lane, LANES), axis=-1, keepdims=True)
        oh = lane == idx
        onehots.append(oh)
        vals.append(m)
        work = jnp.where(oh, -jnp.inf, work)
    multi = jnp.zeros((tm, LANES), F32)
    for oh in onehots:
        multi = multi + jnp.where(oh & valid, 1.0, 0.0)
    tile_cnt = jnp.sum(multi, axis=0, keepdims=True)

    @pl.when(ph == 0)
    def _():
        cnt_scr[...] += tile_cnt

    @pl.when(ph == 1)
    def _():
        @pl.when(i == 0)
        def _():
            c = cnt_scr[...]
            padded = jnp.ceil(c * (1.0 / MOE_BLK)) * MOE_BLK
            before = (lax.broadcasted_iota(I32, (LANES, LANES), 0)
                      < lax.broadcasted_iota(I32, (LANES, LANES), 1))
            start = jnp.dot(jnp.broadcast_to(padded, (8, LANES)), jnp.where(before, 1.0, 0.0),
                            preferred_element_type=F32, precision=lax.Precision.HIGHEST)
            start_scr[...] = start[0:1]
            carry_scr[...] = jnp.zeros_like(carry_scr)
            cnt_ref[...] = c

        earlier = (lax.broadcasted_iota(I32, (tm, tm), 1) < lax.broadcasted_iota(I32, (tm, tm), 0))
        prefix = jnp.dot(jnp.where(earlier, 1.0, 0.0).astype(BF16), multi.astype(BF16),
                         preferred_element_type=F32)
        base = prefix + carry_scr[...] + start_scr[...]
        den = jnp.zeros_like(vals[0])
        for v in vals:
            den = den + jnp.exp(v - vals[0])
        dest = jnp.zeros((tm, LANES), F32)
        gate = jnp.zeros((tm, LANES), F32)
        for k in range(TOP_K):
            d_k = jnp.sum(jnp.where(onehots[k], base, 0.0), axis=-1, keepdims=True)
            dest = jnp.where(lane == k, d_k, dest)
            gate = jnp.where(lane == k, jnp.exp(vals[k] - vals[0]) / den, gate)
        dest_ref[...] = dest.astype(I32)
        gate_ref[...] = gate
        carry_scr[...] += tile_cnt


def _route(logits, n_tok):
    R = logits.shape[0]
    tm = ROW_TILE
    blk = pl.BlockSpec((tm, LANES), lambda p, i: (i * p, 0))
    return pl.pallas_call(
        functools.partial(_route_kernel, n_tok=n_tok),
        grid=(2, R // tm),
        in_specs=[pl.BlockSpec((tm, LANES), lambda p, i: (i, 0))],
        out_specs=[blk, blk, pl.BlockSpec((1, LANES), lambda p, i: (0, 0))],
        out_shape=[jax.ShapeDtypeStruct((R, LANES), I32), jax.ShapeDtypeStruct((R, LANES), F32),
                   jax.ShapeDtypeStruct((1, LANES), F32)],
        scratch_shapes=[pltpu.VMEM((1, LANES), F32)] * 3,
        compiler_params=_cparams(("arbitrary", "arbitrary"), VMEM_LIMIT),
        name="route",
    )(logits)


def _dispatch_kernel(dest_ref, xn_ref, xs_ref, sem):
    def row_copy(r, k):
        return pltpu.make_async_copy(xn_ref.at[pl.ds(r, 1), :],
                                     xs_ref.at[pl.ds(dest_ref[r * TOP_K + k], 1), :], sem)

    def issue(r, carry):
        for k in range(TOP_K):
            row_copy(r, k).start()
        return carry

    lax.fori_loop(0, TOK_TILE, issue, 0)
    for _ in range(TOP_K):
        pltpu.make_async_copy(xn_ref, xs_ref.at[pl.ds(0, TOK_TILE), :], sem).wait()


def _dispatch(xn, dest_flat, n_tok, n_slots):
    D = xn.shape[1]
    return pl.pallas_call(
        _dispatch_kernel,
        grid=(n_tok // TOK_TILE,),
        in_specs=[pl.BlockSpec((TOK_TILE * TOP_K,), lambda i: (i,), memory_space=pltpu.SMEM),
                  pl.BlockSpec((TOK_TILE, D), lambda i: (i, 0))],
        out_specs=pl.BlockSpec(memory_space=pl.ANY),
        out_shape=jax.ShapeDtypeStruct((n_slots, D), F32),
        scratch_shapes=[pltpu.SemaphoreType.DMA],
        compiler_params=_cparams(("arbitrary",), VMEM_LIMIT),
        name="dispatch",
    )(dest_flat, xn)


class _WeightStream:
    def __init__(self, w_refs, stage, wbuf, sems, unit_rows):
        self.w_refs, self.stage, self.wbuf, self.sems = w_refs, stage, wbuf, sems
        self.unit_rows = unit_rows
        self.n_units = w_refs[0].shape[1] // unit_rows
        self.n_stage = stage.shape[0]

    def _copy(self, e, u, m):
        rows = pl.ds(pl.multiple_of(u * self.unit_rows, self.unit_rows), self.unit_rows)
        st = u % self.n_stage
        return pltpu.make_async_copy(self.w_refs[m].at[e, rows, :], self.stage.at[st, m],
                                     self.sems.at[st, m])

    def start(self, e, u):
        for m in range(len(self.w_refs)):
            self._copy(e, u, m).start()

    def finish(self, e, u, slot):
        rows = pl.ds(pl.multiple_of(u * self.unit_rows, self.unit_rows), self.unit_rows)
        for m in range(len(self.w_refs)):
            self._copy(e, u, m).wait()
            self.wbuf[slot, m, rows, :] = self.stage[u % self.n_stage, m].astype(BF16)

    def prime(self, e):
        for u in range(self.n_stage):
            self.start(e, u)

    def convert(self, e, slot, lo, hi):
        def body(u, carry):
            self.finish(e, u, slot)

            @pl.when(u + self.n_stage < self.n_units)
            def _():
                self.start(e, u + self.n_stage)

            return carry

        lax.fori_loop(lo, hi, body, 0)


def _moe_step(plan, stream, compute):
    be_ref, nxt_ref, slot_ref, first_ref, ulo_ref, uhi_ref, nu_ref = plan
    blk = pl.program_id(0)

    @pl.when(blk < nu_ref[0])
    def _():
        e = be_ref[blk]
        ne = nxt_ref[blk]
        slot = slot_ref[blk]

        @pl.when(blk == 0)
        def _():
            stream.prime(e)
            stream.convert(e, slot, 0, stream.n_units)

        @pl.when((first_ref[blk] == 1) & (ne >= 0))
        def _():
            stream.prime(ne)

        compute(slot)

        @pl.when(ne >= 0)
        def _():
            stream.convert(ne, 1 - slot, ulo_ref[blk], uhi_ref[blk])


def _moe_up_kernel(be_ref, nxt_ref, slot_ref, first_ref, ulo_ref, uhi_ref, nu_ref,
                   xs_ref, wg_ref, wu_ref, bg_ref, bu_ref, act_ref, wbuf, stage, sems):
    stream = _WeightStream((wg_ref, wu_ref), stage, wbuf, sems, MOE_UNIT_UP)
    half = act_ref.shape[1] // 2

    def compute(slot):
        xb = xs_ref[...].astype(BF16)
        for c in range(2):
            cols = slice(c * half, (c + 1) * half)
            gate = jnp.dot(xb, wbuf[slot, 0, :, cols], preferred_element_type=F32) + bg_ref[:, cols]
            up = jnp.dot(xb, wbuf[slot, 1, :, cols], preferred_element_type=F32) + bu_ref[:, cols]
            gate = jnp.minimum(gate, SWIGLU_LIMIT)
            up = jnp.clip(up, -SWIGLU_LIMIT, SWIGLU_LIMIT)
            act = gate * jax.nn.sigmoid(SWIGLU_ALPHA * gate) * (up + 1.0)
            act_ref[:, cols] = act.astype(act_ref.dtype)

    _moe_step((be_ref, nxt_ref, slot_ref, first_ref, ulo_ref, uhi_ref, nu_ref), stream, compute)


def _moe_down_kernel(be_ref, nxt_ref, slot_ref, first_ref, ulo_ref, uhi_ref, nu_ref,
                     act_ref, wd_ref, bd_ref, y_ref, wbuf, stage, sems):
    stream = _WeightStream((wd_ref,), stage, wbuf, sems, MOE_UNIT_DOWN)

    def compute(slot):
        y_ref[...] = jnp.dot(act_ref[...], wbuf[slot, 0], preferred_element_type=F32) + bd_ref[...]

    _moe_step((be_ref, nxt_ref, slot_ref, first_ref, ulo_ref, uhi_ref, nu_ref), stream, compute)


def _moe_plan(cnt, n_blk, n_units):
    E = cnt.shape[0]
    nblk_e = (cnt + MOE_BLK - 1) // MOE_BLK
    blk_end = jnp.cumsum(nblk_e)
    blk_first = blk_end - nblk_e
    j = jnp.arange(n_blk, dtype=I32)
    be = jnp.minimum(jnp.sum((blk_end[None, :] <= j[:, None]).astype(I32), axis=1), E - 1)
    k = j - blk_first[be]
    nb = jnp.maximum(nblk_e[be], 1)
    nonempty = nblk_e > 0
    ids = jnp.arange(E, dtype=I32)
    later = (ids[None, :] > ids[:, None]) & nonempty[None, :]
    nxt_e = jnp.min(jnp.where(later, ids[None, :], E), axis=1)
    nxt_e = jnp.where(nxt_e >= E, -1, nxt_e)
    slot_e = (jnp.cumsum(nonempty.astype(I32)) - 1) % 2
    plan = (be, nxt_e[be], slot_e[be], (k == 0).astype(I32),
            (k * n_units) // nb, ((k + 1) * n_units) // nb, blk_end[-1:].astype(I32))
    return tuple(a.astype(I32) for a in plan)


def _moe(xs, cnt, w_gate, b_gate, w_up, b_up, w_down, b_down):
    P, D = xs.shape
    E, _, F = w_gate.shape
    n_blk = P // MOE_BLK
    n_stage = MOE_STAGES
    clamp = lambda blk, nu: jnp.maximum(jnp.minimum(blk, nu[0] - 1), 0)
    row_map = lambda blk, be, nx, sl, fi, lo, hi, nu: (clamp(blk, nu), 0)
    exp_map = lambda blk, be, nx, sl, fi, lo, hi, nu: (be[clamp(blk, nu)], 0, 0)
    hbm = pl.BlockSpec(memory_space=pl.ANY)

    plan_up = _moe_plan(cnt, n_blk, D // MOE_UNIT_UP)
    act = pl.pallas_call(
        _moe_up_kernel,
        grid_spec=pltpu.PrefetchScalarGridSpec(
            num_scalar_prefetch=7,
            grid=(n_blk,),
            in_specs=[pl.BlockSpec((MOE_BLK, D), row_map), hbm, hbm,
                      pl.BlockSpec((None, 1, F), exp_map), pl.BlockSpec((None, 1, F), exp_map)],
            out_specs=pl.BlockSpec((MOE_BLK, F), row_map),
            scratch_shapes=[pltpu.VMEM((2, 2, D, F), BF16),
                            pltpu.VMEM((n_stage, 2, MOE_UNIT_UP, F), F32),
                            pltpu.SemaphoreType.DMA((n_stage, 2))],
        ),
        out_shape=jax.ShapeDtypeStruct((P, F), BF16),
        compiler_params=_cparams(("arbitrary",), VMEM_LIMIT),
        name="moe_up",
    )(*plan_up, xs, w_gate, w_up, b_gate.reshape(E, 1, F), b_up.reshape(E, 1, F))

    plan_down = _moe_plan(cnt, n_blk, F // MOE_UNIT_DOWN)
    return pl.pallas_call(
        _moe_down_kernel,
        grid_spec=pltpu.PrefetchScalarGridSpec(
            num_scalar_prefetch=7,
            grid=(n_blk,),
            in_specs=[pl.BlockSpec((MOE_BLK, F), row_map), hbm, pl.BlockSpec((None, 1, D), exp_map)],
            out_specs=pl.BlockSpec((MOE_BLK, D), row_map),
            scratch_shapes=[pltpu.VMEM((2, 1, F, D), BF16),
                            pltpu.VMEM((n_stage, 1, MOE_UNIT_DOWN, D), F32),
                            pltpu.SemaphoreType.DMA((n_stage, 1))],
        ),
        out_shape=jax.ShapeDtypeStruct((P, D), F32),
        compiler_params=_cparams(("arbitrary",), VMEM_LIMIT),
        name="moe_down",
    )(*plan_down, act, w_down, b_down.reshape(E, 1, D))


def _combine_kernel(dest_ref, dnext_ref, h2_ref, gate_ref, g_ref, y_ref, o_ref, ybuf, sem):
    i = pl.program_id(0)
    slot = i % 2

    def issue(dref, s, unrolled):
        def one(r):
            for k in range(TOP_K):
                pltpu.make_async_copy(y_ref.at[pl.ds(dref[r * TOP_K + k], 1), :],
                                      ybuf.at[s, k, pl.ds(r, 1), :], sem.at[s]).start()

        _for_rows(TOK_TILE, one, unrolled)

    @pl.when(i == 0)
    def _():
        issue(dest_ref, 0, False)

    for s in range(2):
        @pl.when((slot == s) & (i + 1 < pl.num_programs(0)))
        def _():
            issue(dnext_ref, 1 - s, True)

    for k in range(TOP_K):
        pltpu.make_async_copy(y_ref.at[pl.ds(0, TOK_TILE), :], ybuf.at[slot, k], sem.at[slot]).wait()
    gates = gate_ref[...]
    acc = h2_ref[...]
    for k in range(TOP_K):
        acc = acc + gates[:, k:k + 1] * ybuf[slot, k]
    o_ref[...] = _rms(acc, g_ref[...])


def _combine(h2, gates, dest_flat, y, g, n_seq):
    D = h2.shape[1]
    n_tiles = n_seq // TOK_TILE
    return pl.pallas_call(
        _combine_kernel,
        grid=(n_tiles,),
        in_specs=[pl.BlockSpec((TOK_TILE * TOP_K,), lambda i: (i,), memory_space=pltpu.SMEM),
                  pl.BlockSpec((TOK_TILE * TOP_K,), lambda i: (jnp.minimum(i + 1, n_tiles - 1),),
                               memory_space=pltpu.SMEM),
                  pl.BlockSpec((TOK_TILE, D), lambda i: (i, 0)),
                  pl.BlockSpec((TOK_TILE, LANES), lambda i: (i, 0)),
                  pl.BlockSpec((1, D), lambda i: (0, 0)),
                  pl.BlockSpec(memory_space=pl.ANY)],
        out_specs=pl.BlockSpec((TOK_TILE, D), lambda i: (i, 0)),
        out_shape=jax.ShapeDtypeStruct((n_seq, D), F32),
        scratch_shapes=[pltpu.VMEM((2, TOP_K, TOK_TILE, D), F32), pltpu.SemaphoreType.DMA((2,))],
        compiler_params=_cparams(("arbitrary",), VMEM_LIMIT),
        name="combine_norm",
    )(dest_flat, dest_flat, h2, gates, g, y)


def _rope_tables(S, n_meta_rows):
    half = HEAD_DIM // 2
    inv_freq = ROPE_THETA ** (-jnp.arange(half, dtype=F32) / half)
    lane = np.arange(LANES)
    fidx = lane % half
    sign = np.where((lane % HEAD_DIM) < half, -1.0, 1.0).astype(np.float32)
    pos_seq = N_META_TOK + jnp.arange(S)
    tail = jnp.arange(ROW_TILE)
    pos_tail = jnp.where(tail < n_meta_rows, tail % N_META_TOK, 0)
    pos = jnp.concatenate([pos_seq, pos_tail]).astype(F32)
    ang = (pos[:, None] * inv_freq[None, :])[:, fidx]
    return jnp.cos(ang), jnp.sin(ang) * sign[None, :]


def kernel(x, meta_tokens, attn_norm_g, w_in, attn_sinks, lower_bound_logits, rec_norm_g,
           w_attn_proj, w_rec_proj, w_out, ffn_norm_g, w_router, b_router, w_gate, b_gate,
           w_up, b_up, w_down, b_down, final_norm_g):
    B, S, D = x.shape
    assert w_in.shape[0] == 1, "single-layer block"
    assert S % ROW_TILE == 0 and (B * S) % ROW_TILE == 0 and S % SEG == 0
    assert S % SCAN_SEG == 0 and B % SCAN_BATCHES == 0
    NS = B * S
    NM = B * N_META_TOK
    NT = NS + NM
    R = -(-NT // ROW_TILE) * ROW_TILE
    assert R - NS == ROW_TILE and NT % TOK_TILE == 0 and NS % TOK_TILE == 0

    x2d = x.reshape(NS, D)
    tail_rows = jnp.concatenate([jnp.tile(meta_tokens.astype(x.dtype), (B, 1)),
                                 jnp.zeros((R - NT, D), x.dtype)], axis=0)

    w0 = w_in[0]
    kw = N_KV_HEADS * HEAD_DIM
    wq, wk, wv = w0[:, :Q_W], w0[:, Q_W:Q_W + kw], w0[:, Q_W + kw:Q_W + 2 * kw]
    dup = lambda w: jnp.concatenate([w[:, :HEAD_DIM]] * 2 + [w[:, HEAD_DIM:]] * 2, axis=1)
    w_qkv = jnp.concatenate([wq, dup(wk), dup(wv)], axis=1).astype(BF16)
    w_rest = w0[:, Q_W + 2 * kw:].astype(BF16)

    cos_t, sin_t = _rope_tables(S, NM)
    u, qkv = _norm_qkv(x2d, tail_rows, attn_norm_g[0].reshape(1, D), w_qkv, cos_t, sin_t,
                       S // ROW_TILE)
    rest = _proj_rest(u, w_rest)

    attn, attn_tail = _attention(qkv, attn_sinks[0], B, S)
    rec, rec_tail = _hgrn(rest, lower_bound_logits, rec_norm_g[0].reshape(1, REC_DIM), B, S)

    wr_hi = w_router[0].astype(BF16)
    wr_lo = (w_router[0] - wr_hi.astype(F32)).astype(BF16)
    lane_pad = ((0, 0), (0, LANES - N_EXPERTS))
    wr_pad = jnp.concatenate([jnp.pad(wr_hi, lane_pad), jnp.pad(wr_lo, lane_pad)], axis=1)
    br_pad = jnp.pad(b_router[0].reshape(1, N_EXPERTS), ((0, 0), (0, LANES - N_EXPERTS)),
                     constant_values=NEG_INF)
    h2, xn, logits = _mix(attn, attn_tail, rec, rec_tail, rest, x2d, tail_rows,
                          w_attn_proj[0].astype(BF16), w_rec_proj[0].astype(BF16),
                          w_out[0].astype(BF16), ffn_norm_g[0].reshape(1, D), wr_pad, br_pad)

    dest, gates, counts = _route(logits, NT)
    cnt = counts[0, :N_EXPERTS].astype(I32)
    n_blk = -(-(NT * TOP_K + N_EXPERTS * (MOE_BLK - 1)) // MOE_BLK)
    dest_flat = dest[:NT, :TOP_K].reshape(NT * TOP_K)

    xs = _dispatch(xn, dest_flat, NT, n_blk * MOE_BLK)
    y = _moe(xs, cnt, w_gate[0], b_gate[0], w_up[0], b_up[0], w_down[0], b_down[0])
    out = _combine(h2, gates, dest_flat, y, final_norm_g.reshape(1, D), NS)
    return out.reshape(B, S, D)
```

```python
import functools

import jax
import jax.numpy as jnp
import numpy as np
from jax import lax
from jax.experimental import pallas as pl
from jax.experimental.pallas import tpu as pltpu

F32 = jnp.float32
BF16 = jnp.bfloat16
I32 = jnp.int32

N_META_TOK = 16
HEAD_DIM = 64
N_Q_HEADS = 16
N_KV_HEADS = 2
ATTN_BLK = 128
ROPE_THETA = 10000.0
N_REC_HEADS = 8
REC_DIM = 128
REC_CHUNK = 16
N_EXPERTS = 32
TOP_K = 4
SWIGLU_ALPHA = 1.702
SWIGLU_LIMIT = 7.0
NORM_EPS = 1e-5
NEG_INF = -1e30

LANES = 128
ROW_TILE = 512
MIX_TILE = 256
SEG = 256
SCAN_SEG = 256
SCAN_BATCHES = 2
MOE_BLK = 256
MOE_UNIT = 128
MOE_STAGES = 3
TOK_TILE = 128
VMEM_LIMIT = 56 * 1024 * 1024

Q_W = N_Q_HEADS * HEAD_DIM
REC_W = N_REC_HEADS * REC_DIM


def _cparams(sem, vmem=None, **kw):
    return pltpu.CompilerParams(dimension_semantics=sem, vmem_limit_bytes=vmem, **kw)


def _rms(xf, g):
    return xf * lax.rsqrt(jnp.mean(xf * xf, axis=-1, keepdims=True) + NORM_EPS) * g


def _for_rows(n, fn, unrolled):
    if unrolled:
        for r in range(n):
            fn(r)
    else:
        def body(r, carry):
            fn(r)
            return carry

        lax.fori_loop(0, n, body, 0)


def _norm_qkv_kernel(x_ref, tail_ref, g_ref, w_ref, cos_ref, sin_ref, u_ref, qkv_ref, *, n_seq_tiles):
    h = jnp.where(pl.program_id(0) < n_seq_tiles, x_ref[...], tail_ref[...])
    u = _rms(h, g_ref[...]).astype(BF16)
    u_ref[...] = u
    p = jnp.dot(u, w_ref[...], preferred_element_type=F32)
    cos = cos_ref[...]
    sin = sin_ref[...]
    lane = lax.broadcasted_iota(I32, cos.shape, 1)
    first_half = (lane % HEAD_DIM) < (HEAD_DIM // 2)
    n_q = Q_W // LANES
    for c in range(n_q + 2):
        xs = p[:, c * LANES:(c + 1) * LANES]
        swapped = jnp.where(first_half, pltpu.roll(xs, LANES - HEAD_DIM // 2, 1),
                            pltpu.roll(xs, HEAD_DIM // 2, 1))
        r = xs * cos + swapped * sin
        if c < n_q:
            r = r * (HEAD_DIM ** -0.5)
        qkv_ref[:, c * LANES:(c + 1) * LANES] = r.astype(BF16)
    v0 = (n_q + 2) * LANES
    qkv_ref[:, v0:] = p[:, v0:].astype(BF16)


def _norm_qkv(x2d, tail_rows, g, w_qkv, cos_t, sin_t, tiles_per_seq):
    NS, D = x2d.shape
    R = NS + tail_rows.shape[0]
    W = w_qkv.shape[1]
    n_seq_tiles = NS // ROW_TILE

    def tab_map(i):
        return (jnp.where(i < n_seq_tiles, i % tiles_per_seq, tiles_per_seq), 0)

    return pl.pallas_call(
        functools.partial(_norm_qkv_kernel, n_seq_tiles=n_seq_tiles),
        grid=(R // ROW_TILE,),
        in_specs=[
            pl.BlockSpec((ROW_TILE, D), lambda i: (jnp.minimum(i, n_seq_tiles - 1), 0)),
            pl.BlockSpec((ROW_TILE, D), lambda i: (0, 0)),
            pl.BlockSpec((1, D), lambda i: (0, 0)),
            pl.BlockSpec((D, W), lambda i: (0, 0)),
            pl.BlockSpec((ROW_TILE, LANES), tab_map),
            pl.BlockSpec((ROW_TILE, LANES), tab_map),
        ],
        out_specs=[
            pl.BlockSpec((ROW_TILE, D), lambda i: (i, 0)),
            pl.BlockSpec((ROW_TILE, W), lambda i: (i, 0)),
        ],
        out_shape=[jax.ShapeDtypeStruct((R, D), BF16), jax.ShapeDtypeStruct((R, W), BF16)],
        compiler_params=_cparams(("parallel",), VMEM_LIMIT),
        name="norm_qkv",
    )(x2d, tail_rows, g, w_qkv, cos_t, sin_t)


def _matmul_kernel(x_ref, w_ref, o_ref):
    o_ref[...] = jnp.dot(x_ref[...], w_ref[...], preferred_element_type=F32).astype(o_ref.dtype)


def _proj_rest(u, w):
    R, D = u.shape
    N = w.shape[1]
    tn = 2048
    return pl.pallas_call(
        _matmul_kernel,
        grid=(N // tn, R // ROW_TILE),
        in_specs=[pl.BlockSpec((ROW_TILE, D), lambda j, i: (i, 0)),
                  pl.BlockSpec((D, tn), lambda j, i: (0, j))],
        out_specs=pl.BlockSpec((ROW_TILE, tn), lambda j, i: (i, j)),
        out_shape=jax.ShapeDtypeStruct((R, N), F32),
        compiler_params=_cparams(("parallel", "parallel"), VMEM_LIMIT),
        name="proj_rest",
    )(u, w)


def _attn_core(q_ref, k_groups, v_groups, mask_fn, sink_ref, write):
    nkeys = k_groups[0].shape[0]
    rows = q_ref.shape[0]
    heads_per_group = N_Q_HEADS // N_KV_HEADS
    width = heads_per_group * rows
    lo_q = lax.broadcasted_iota(I32, (rows, LANES), 1) < HEAD_DIM
    ki = lax.broadcasted_iota(I32, (nkeys, width), 0)
    qi = lax.broadcasted_iota(I32, (nkeys, width), 1) % rows
    mask = mask_fn(ki, qi)
    top_half = lax.broadcasted_iota(I32, (LANES, rows), 0) < HEAD_DIM
    for g in range(N_KV_HEADS):
        parts = []
        for j in range(heads_per_group // 2):
            qp = q_ref[:, (heads_per_group // 2 * g + j) * LANES:(heads_per_group // 2 * g + j + 1) * LANES]
            zero_q = jnp.zeros_like(qp)
            parts += [jnp.where(lo_q, qp, zero_q), jnp.where(lo_q, zero_q, qp)]
        q_stack = jnp.concatenate(parts, axis=0)
        s = lax.dot_general(k_groups[g], q_stack, (((1,), (1,)), ((), ())),
                            preferred_element_type=F32)
        s = jnp.where(mask, s, NEG_INF)
        sk = sink_ref[:, g * width:(g + 1) * width]
        m = jnp.maximum(jnp.max(s, axis=0, keepdims=True), sk)
        e = jnp.exp(s - m)
        den = jnp.sum(e, axis=0, keepdims=True) + jnp.exp(sk - m)
        prob = (e * (1.0 / den)).astype(BF16)
        o_t = lax.dot_general(v_groups[g], prob, (((0,), (0,)), ((), ())), preferred_element_type=F32)
        for j in range(heads_per_group // 2):
            even = o_t[:, (2 * j) * rows:(2 * j + 1) * rows]
            odd = o_t[:, (2 * j + 1) * rows:(2 * j + 2) * rows]
            pair_t = jnp.where(top_half, even, odd)
            write(heads_per_group // 2 * g + j, pair_t.T)


def _attn_seq_kernel(sink_ref, q_ref, kc0, kc1, vc0, vc1, kp0, kp1, vp0, vp1,
                     km0, km1, vm0, vm1, o_ref):
    n = pl.program_id(1)
    no_prev = jnp.where(n > 0, 0, 2 * ATTN_BLK)

    def mask_fn(ki, qi):
        prev_ok = (ki >= N_META_TOK) & (ki < N_META_TOK + ATTN_BLK) & (ki - N_META_TOK > qi + no_prev)
        cur_ok = (ki >= N_META_TOK + ATTN_BLK) & (ki - (N_META_TOK + ATTN_BLK) <= qi)
        return (ki < N_META_TOK) | prev_ok | cur_ok

    k_groups = [jnp.concatenate([km[...], kp[...], kc[...]], axis=0)
                for km, kp, kc in ((km0, kp0, kc0), (km1, kp1, kc1))]
    v_groups = [jnp.concatenate([vm[...], vp[...], vc[...]], axis=0)
                for vm, vp, vc in ((vm0, vp0, vc0), (vm1, vp1, vc1))]

    def write(pr, acc):
        o_ref[:, pr * LANES:(pr + 1) * LANES] = acc.astype(o_ref.dtype)

    _attn_core(q_ref, k_groups, v_groups, mask_fn, sink_ref, write)


def _attn_meta_kernel(sink_ref, q_ref, k0, k1, v0, v1, o_ref):
    nm = q_ref.shape[0]

    def mask_fn(ki, qi):
        return ((qi // N_META_TOK) == (ki // N_META_TOK)) & ((ki % N_META_TOK) <= (qi % N_META_TOK))

    o_ref[...] = jnp.zeros_like(o_ref)

    def write(pr, acc):
        o_ref[0:nm, pr * LANES:(pr + 1) * LANES] = acc.astype(o_ref.dtype)

    _attn_core(q_ref, [k0[...], k1[...]], [v0[...], v1[...]], mask_fn, sink_ref, write)


def _attention(qkv, sinks, B, S):
    R = qkv.shape[0]
    NS = B * S
    NM = B * N_META_TOK
    nb = S // ATTN_BLK
    qc = Q_W // LANES
    sink_seq = jnp.repeat(sinks.astype(F32), ATTN_BLK).reshape(1, N_Q_HEADS * ATTN_BLK)
    sink_meta = jnp.repeat(sinks.astype(F32), NM).reshape(1, N_Q_HEADS * NM)

    def kv_spec(col, prev):
        if prev:
            return pl.BlockSpec((ATTN_BLK, LANES), lambda b, n: (b * nb + jnp.maximum(n - 1, 0), col))
        return pl.BlockSpec((ATTN_BLK, LANES), lambda b, n: (b * nb + n, col))

    def meta_spec(col):
        return pl.BlockSpec((N_META_TOK, LANES), lambda b, n: (NS // N_META_TOK + b, col))

    in_specs = [pl.BlockSpec(sink_seq.shape, lambda b, n: (0, 0)),
                pl.BlockSpec((ATTN_BLK, Q_W), lambda b, n: (b * nb + n, 0))]
    in_specs += [kv_spec(qc, False), kv_spec(qc + 1, False), kv_spec(qc + 2, False), kv_spec(qc + 3, False)]
    in_specs += [kv_spec(qc, True), kv_spec(qc + 1, True), kv_spec(qc + 2, True), kv_spec(qc + 3, True)]
    in_specs += [meta_spec(qc), meta_spec(qc + 1), meta_spec(qc + 2), meta_spec(qc + 3)]
    attn = pl.pallas_call(
        _attn_seq_kernel,
        grid=(B, nb),
        in_specs=in_specs,
        out_specs=pl.BlockSpec((ATTN_BLK, Q_W), lambda b, n: (b * nb + n, 0)),
        out_shape=jax.ShapeDtypeStruct((NS, Q_W), BF16),
        compiler_params=_cparams(("parallel", "parallel"), VMEM_LIMIT),
        name="attn_seq",
    )(sink_seq, *([qkv] * 13))

    mb = NS // NM
    tail = R - NS
    blk = lambda col: pl.BlockSpec((NM, LANES), lambda i: (mb, col))
    attn_tail = pl.pallas_call(
        _attn_meta_kernel,
        grid=(1,),
        in_specs=[pl.BlockSpec(sink_meta.shape, lambda i: (0, 0)),
                  pl.BlockSpec((NM, Q_W), lambda i: (mb, 0)),
                  blk(qc), blk(qc + 1), blk(qc + 2), blk(qc + 3)],
        out_specs=pl.BlockSpec((tail, Q_W), lambda i: (0, 0)),
        out_shape=jax.ShapeDtypeStruct((tail, Q_W), BF16),
        compiler_params=_cparams(("arbitrary",), VMEM_LIMIT),
        name="attn_meta",
    )(sink_meta, qkv, qkv, qkv, qkv, qkv)
    return attn, attn_tail


def _hgrn_prep_kernel(rq_ref, rf_ref, ri_ref, lbl_ref, qd_ref, kd_ref, ke_ref, v_ref, eb_ref):
    lbl = lbl_ref[...]
    e = jnp.exp(lbl - jnp.max(lbl, axis=0, keepdims=True))
    lb = e[0:1] / jnp.sum(e, axis=0, keepdims=True)
    f = lb + (1.0 - lb) * jax.nn.sigmoid(rf_ref[...])
    logf = jnp.log(f)
    k = 1.0 - f
    tm = logf.shape[0]
    ri = lax.broadcasted_iota(I32, (tm, tm), 0)
    ci = lax.broadcasted_iota(I32, (tm, tm), 1)
    same_chunk = (ri // REC_CHUNK) == (ci // REC_CHUNK)
    sel = jnp.concatenate([jnp.where(same_chunk & (ci <= ri), 1.0, 0.0).astype(BF16),
                           jnp.where(same_chunk, 1.0, 0.0).astype(BF16)], axis=0)
    hi = logf.astype(BF16)
    rem = logf - hi.astype(F32)
    mid = rem.astype(BF16)
    lo = (rem - mid.astype(F32)).astype(BF16)
    sums = (jnp.dot(sel, hi, preferred_element_type=F32) + jnp.dot(sel, mid, preferred_element_type=F32)
            + jnp.dot(sel, lo, preferred_element_type=F32))
    b = sums[:tm]
    total = sums[tm:]
    eb = jnp.exp(b)
    eb_ref[...] = eb
    qd_ref[...] = (rq_ref[...] * eb).astype(BF16)
    kd_ref[...] = (k * jnp.exp(-b)).astype(BF16)
    ke_ref[...] = (k * jnp.exp(total - b)).astype(BF16)
    v_ref[...] = ri_ref[...].astype(BF16)


def _hgrn_prep(rest, lbl):
    R = rest.shape[0]
    W = REC_W
    col = lambda c: pl.BlockSpec((SEG, W), lambda i: (i, c))
    row_out = pl.BlockSpec((SEG, W), lambda i: (i, 0))
    return pl.pallas_call(
        _hgrn_prep_kernel,
        grid=(R // SEG,),
        in_specs=[col(0), col(1), col(2), pl.BlockSpec(lbl.shape, lambda i: (0, 0))],
        out_specs=[row_out] * 5,
        out_shape=[jax.ShapeDtypeStruct((R, W), BF16)] * 4 + [jax.ShapeDtypeStruct((R, W), F32)],
        compiler_params=_cparams(("parallel",), VMEM_LIMIT),
        name="hgrn_prep",
    )(rest, rest, rest, lbl)


def _hgrn_intra(qd, kd, v):
    n = qd.shape[0]
    ri = lax.broadcasted_iota(I32, (n, n), 0)
    ci = lax.broadcasted_iota(I32, (n, n), 1)
    keep = ((ri // REC_CHUNK) == (ci // REC_CHUNK)) & (ri >= ci)
    sc = lax.dot_general(qd, kd, (((1,), (1,)), ((), ())), preferred_element_type=F32)
    sc = jnp.where(keep, sc, 0.0)
    return jnp.dot(sc.astype(BF16), v, preferred_element_type=F32)


def _hgrn_kv_t(v, ke):
    return lax.dot_general(v, ke, (((0,), (0,)), ((), ())), preferred_element_type=F32)


def _hgrn_finish(o, g, norm_g):
    y = o * lax.rsqrt(jnp.mean(o * o, axis=-1, keepdims=True) + NORM_EPS) * norm_g
    return (y * (g * jax.nn.sigmoid(g))).astype(BF16)


def _hgrn_meta_kernel(qd_ref, kd_ref, ke_ref, v_ref, rg_ref, ng_ref, st_ref, rec_ref, *, nbatch):
    rec_ref[...] = jnp.zeros_like(rec_ref)
    nm = nbatch * REC_CHUNK
    for h in range(N_REC_HEADS):
        cols = slice(h * REC_DIM, (h + 1) * REC_DIM)
        o = _hgrn_intra(qd_ref[0:nm, cols], kd_ref[0:nm, cols], v_ref[0:nm, cols])
        rec_ref[0:nm, cols] = _hgrn_finish(o, rg_ref[0:nm, cols], ng_ref[...])
        for b in range(nbatch):
            rows = slice(b * REC_CHUNK, (b + 1) * REC_CHUNK)
            st_ref[b, h] = _hgrn_kv_t(v_ref[rows, cols], ke_ref[rows, cols])


def _hgrn_scan_kernel(*refs, gb):
    qd, kd, ke, v, eb, rg = (refs[i * gb:(i + 1) * gb] for i in range(6))
    ng_ref, st0_ref, rec_ref, st_scr, o_scr = refs[6 * gb:]
    seg = rec_ref.shape[1]
    heads = [slice(h * REC_DIM, (h + 1) * REC_DIM) for h in range(N_REC_HEADS)]

    @pl.when(pl.program_id(1) == 0)
    def _():
        st_scr[...] = st0_ref[...]

    for i in range(gb):
        for cols in heads:
            o_scr[i, :, cols] = _hgrn_intra(qd[i][:, cols], kd[i][:, cols], v[i][:, cols])

    def chunk(c, carry):
        r0 = pl.multiple_of(c * REC_CHUNK, REC_CHUNK)
        rows = pl.ds(r0, REC_CHUNK)
        for i in range(gb):
            for h, cols in enumerate(heads):
                last8 = eb[i][pl.ds(pl.multiple_of(r0 + REC_CHUNK - 8, 8), 8), cols]
                dec = last8[7:8]
                st = st_scr[i, h]
                o_scr[i, rows, cols] += lax.dot_general(qd[i][rows, cols], st.astype(BF16),
                                                        (((1,), (1,)), ((), ())),
                                                        preferred_element_type=F32)
                st_scr[i, h] = st * dec + _hgrn_kv_t(v[i][rows, cols], ke[i][rows, cols])
        return carry

    lax.fori_loop(0, seg // REC_CHUNK, chunk, 0, unroll=2)
    for i in range(gb):
        for cols in heads:
            rec_ref[i, :, cols] = _hgrn_finish(o_scr[i, :, cols], rg[i][:, cols], ng_ref[...])


def _hgrn(rest, lbl, norm_g, B, S):
    R = rest.shape[0]
    NS = B * S
    W = REC_W
    tail = R - NS
    tb = NS // tail
    qd, kd, ke, v, eb = _hgrn_prep(rest, lbl)

    tail_spec = pl.BlockSpec((tail, W), lambda i: (tb, 0))
    state, rec_tail = pl.pallas_call(
        functools.partial(_hgrn_meta_kernel, nbatch=B),
        grid=(1,),
        in_specs=[tail_spec, tail_spec, tail_spec, tail_spec,
                  pl.BlockSpec((tail, W), lambda i: (tb, 3)),
                  pl.BlockSpec((1, REC_DIM), lambda i: (0, 0))],
        out_specs=[pl.BlockSpec((B, N_REC_HEADS, REC_DIM, REC_DIM), lambda i: (0, 0, 0, 0)),
                   pl.BlockSpec((tail, W), lambda i: (0, 0))],
        out_shape=[jax.ShapeDtypeStruct((B, N_REC_HEADS, REC_DIM, REC_DIM), F32),
                   jax.ShapeDtypeStruct((tail, W), BF16)],
        compiler_params=_cparams(("arbitrary",), VMEM_LIMIT),
        name="hgrn_meta",
    )(qd, kd, ke, v, rest, norm_g)

    gb = SCAN_BATCHES
    ns = S // SCAN_SEG

    def seg_specs(col):
        return [pl.BlockSpec((SCAN_SEG, W), lambda g, s, i=i: ((g * gb + i) * ns + s, col))
                for i in range(gb)]

    rec = pl.pallas_call(
        functools.partial(_hgrn_scan_kernel, gb=gb),
        grid=(B // gb, ns),
        in_specs=seg_specs(0) * 5 + seg_specs(3)
        + [pl.BlockSpec((1, REC_DIM), lambda g, s: (0, 0)),
           pl.BlockSpec((gb, N_REC_HEADS, REC_DIM, REC_DIM), lambda g, s: (g, 0, 0, 0))],
        out_specs=pl.BlockSpec((gb, SCAN_SEG, W), lambda g, s: (g, s, 0)),
        out_shape=jax.ShapeDtypeStruct((B, S, W), BF16),
        scratch_shapes=[pltpu.VMEM((gb, N_REC_HEADS, REC_DIM, REC_DIM), F32),
                        pltpu.VMEM((gb, SCAN_SEG, W), F32)],
        compiler_params=_cparams(("parallel", "arbitrary"), VMEM_LIMIT),
        name="hgrn_scan",
    )(*([qd] * gb + [kd] * gb + [ke] * gb + [v] * gb + [eb] * gb + [rest] * gb), norm_g, state)
    return rec.reshape(NS, W), rec_tail


def _mix_kernel(attn_ref, attn_t_ref, rec_ref, rec_t_ref, ga_ref, gr_ref, x_ref, tail_ref,
                wa_ref, wr_ref, wo_ref, g_ref, wrt_ref, brt_ref, h2_ref, xn_ref, lg_ref, *, n_seq_tiles):
    is_seq = pl.program_id(0) < n_seq_tiles
    attn = jnp.where(is_seq, attn_ref[...], attn_t_ref[...])
    rec = jnp.where(is_seq, rec_ref[...], rec_t_ref[...])
    h = jnp.where(is_seq, x_ref[...], tail_ref[...])
    a = jnp.dot(attn, wa_ref[...], preferred_element_type=F32)
    r = jnp.dot(rec, wr_ref[...], preferred_element_type=F32)
    mixed = jax.nn.sigmoid(ga_ref[...]) * a + jax.nn.sigmoid(gr_ref[...]) * r
    h2 = h + jnp.dot(mixed.astype(BF16), wo_ref[...], preferred_element_type=F32)
    h2_ref[...] = h2
    xn = _rms(h2, g_ref[...])
    xn_ref[...] = xn
    xn_hi = xn.astype(BF16)
    xn_lo = (xn - xn_hi.astype(F32)).astype(BF16)
    w2 = wrt_ref[...]
    p_hi = jnp.dot(xn_hi, w2, preferred_element_type=F32)
    p_lo = jnp.dot(xn_lo, w2[:, :LANES], preferred_element_type=F32)
    lg_ref[...] = p_hi[:, :LANES] + p_hi[:, LANES:] + p_lo + brt_ref[...]


def _mix(attn, attn_tail, rec, rec_tail, rest, x2d, tail_rows, wa, wr, wo, g, w_router, b_router):
    NS, D = x2d.shape
    R = NS + tail_rows.shape[0]
    tm = MIX_TILE
    nst = NS // tm
    const = lambda shape: pl.BlockSpec(shape, lambda i: (0, 0), pipeline_mode=pl.Buffered(1))
    row = lambda w: pl.BlockSpec((tm, w), lambda i: (i, 0))
    seq = lambda w: pl.BlockSpec((tm, w), lambda i: (jnp.minimum(i, nst - 1), 0))
    tl = lambda w: pl.BlockSpec((tm, w), lambda i: (jnp.maximum(i - nst, 0), 0))
    return pl.pallas_call(
        functools.partial(_mix_kernel, n_seq_tiles=nst),
        grid=(R // tm,),
        in_specs=[seq(Q_W), tl(Q_W), seq(REC_W), tl(REC_W),
                  pl.BlockSpec((tm, D), lambda i: (i, 2)), pl.BlockSpec((tm, D), lambda i: (i, 3)),
                  seq(D), tl(D), const(wa.shape), const(wr.shape), const(wo.shape), const(g.shape),
                  const(w_router.shape), const(b_router.shape)],
        out_specs=[row(D), row(D), row(LANES)],
        out_shape=[jax.ShapeDtypeStruct((R, D), F32), jax.ShapeDtypeStruct((R, D), F32),
                   jax.ShapeDtypeStruct((R, LANES), F32)],
        compiler_params=_cparams(("parallel",), VMEM_LIMIT),
        name="mix_outproj",
    )(attn, attn_tail, rec, rec_tail, rest, rest, x2d, tail_rows, wa, wr, wo, g, w_router, b_router)


def _route_kernel(lg_ref, dest_ref, gate_ref, cnt_ref, cnt_scr, carry_scr, start_scr, *, n_tok):
    ph = pl.program_id(0)
    i = pl.program_id(1)
    tm = lg_ref.shape[0]

    @pl.when((ph == 0) & (i == 0))
    def _():
        cnt_scr[...] = jnp.zeros_like(cnt_scr)

    lane = lax.broadcasted_iota(I32, (tm, LANES), 1)
    valid = (i * tm + lax.broadcasted_iota(I32, (tm, LANES), 0)) < n_tok
    work = lg_ref[...]
    onehots, vals = [], []
    for _ in range(TOP_K):
        m = jnp.max(work, axis=-1, keepdims=True)
        idx = jnp.min(jnp.where(work == m, lane, LANES), axis=-1, keepdims=True)
        oh = lane == idx
        onehots.append(oh)
        vals.append(m)
        work = jnp.where(oh, -jnp.inf, work)
    multi = jnp.zeros((tm, LANES), F32)
    for oh in onehots:
        multi = multi + jnp.where(oh & valid, 1.0, 0.0)
    tile_cnt = jnp.sum(multi, axis=0, keepdims=True)

    @pl.when(ph == 0)
    def _():
        cnt_scr[...] += tile_cnt

    @pl.when(ph == 1)
    def _():
        @pl.when(i == 0)
        def _():
            c = cnt_scr[...]
            padded = jnp.ceil(c * (1.0 / MOE_BLK)) * MOE_BLK
            before = (lax.broadcasted_iota(I32, (LANES, LANES), 0)
                      < lax.broadcasted_iota(I32, (LANES, LANES), 1))
            start = jnp.dot(jnp.broadcast_to(padded, (8, LANES)), jnp.where(before, 1.0, 0.0),
                            preferred_element_type=F32, precision=lax.Precision.HIGHEST)
            start_scr[...] = start[0:1]
            carry_scr[...] = jnp.zeros_like(carry_scr)
            cnt_ref[...] = c

        earlier = (lax.broadcasted_iota(I32, (tm, tm), 1) < lax.broadcasted_iota(I32, (tm, tm), 0))
        prefix = jnp.dot(jnp.where(earlier, 1.0, 0.0).astype(BF16), multi.astype(BF16),
                         preferred_element_type=F32)
        base = prefix + carry_scr[...] + start_scr[...]
        den = jnp.zeros_like(vals[0])
        for v in vals:
            den = den + jnp.exp(v - vals[0])
        dest = jnp.zeros((tm, LANES), F32)
        gate = jnp.zeros((tm, LANES), F32)
        for k in range(TOP_K):
            d_k = jnp.sum(jnp.where(onehots[k], base, 0.0), axis=-1, keepdims=True)
            dest = jnp.where(lane == k, d_k, dest)
            gate = jnp.where(lane == k, jnp.exp(vals[k] - vals[0]) / den, gate)
        dest_ref[...] = dest.astype(I32)
        gate_ref[...] = gate
        carry_scr[...] += tile_cnt


def _route(logits, n_tok):
    R = logits.shape[0]
    tm = ROW_TILE
    blk = pl.BlockSpec((tm, LANES), lambda p, i: (i * p, 0))
    return pl.pallas_call(
        functools.partial(_route_kernel, n_tok=n_tok),
        grid=(2, R // tm),
        in_specs=[pl.BlockSpec((tm, LANES), lambda p, i: (i, 0))],
        out_specs=[blk, blk, pl.BlockSpec((1, LANES), lambda p, i: (0, 0))],
        out_shape=[jax.ShapeDtypeStruct((R, LANES), I32), jax.ShapeDtypeStruct((R, LANES), F32),
                   jax.ShapeDtypeStruct((1, LANES), F32)],
        scratch_shapes=[pltpu.VMEM((1, LANES), F32)] * 3,
        compiler_params=_cparams(("arbitrary", "arbitrary"), VMEM_LIMIT),
        name="route",
    )(logits)


def _dispatch_kernel(dest_ref, xn_ref, xs_ref, sem):
    def one(r):
        for k in range(TOP_K):
            pltpu.make_async_copy(xn_ref.at[pl.ds(r, 1), :],
                                  xs_ref.at[pl.ds(dest_ref[r * TOP_K + k], 1), :], sem).start()

    _for_rows(TOK_TILE, one, True)
    for _ in range(TOP_K):
        pltpu.make_async_copy(xn_ref, xs_ref.at[pl.ds(0, TOK_TILE), :], sem).wait()


def _dispatch(xn, dest_flat, n_tok, n_slots):
    D = xn.shape[1]
    return pl.pallas_call(
        _dispatch_kernel,
        grid=(n_tok // TOK_TILE,),
        in_specs=[pl.BlockSpec((TOK_TILE * TOP_K,), lambda i: (i,), memory_space=pltpu.SMEM),
                  pl.BlockSpec((TOK_TILE, D), lambda i: (i, 0))],
        out_specs=pl.BlockSpec(memory_space=pl.ANY),
        out_shape=jax.ShapeDtypeStruct((n_slots, D), F32),
        scratch_shapes=[pltpu.SemaphoreType.DMA],
        compiler_params=_cparams(("arbitrary",), VMEM_LIMIT),
        name="dispatch",
    )(dest_flat, xn)


class _WeightStream:
    def __init__(self, w_refs, stage, wbuf, sems, unit_rows):
        self.w_refs, self.stage, self.wbuf, self.sems = w_refs, stage, wbuf, sems
        self.unit_rows = unit_rows
        self.n_units = w_refs[0].shape[1] // unit_rows
        self.n_stage = stage.shape[0]

    def _copy(self, e, u, m):
        rows = pl.ds(pl.multiple_of(u * self.unit_rows, self.unit_rows), self.unit_rows)
        st = u % self.n_stage
        return pltpu.make_async_copy(self.w_refs[m].at[e, rows, :], self.stage.at[st, m],
                                     self.sems.at[st, m])

    def start(self, e, u):
        for m in range(len(self.w_refs)):
            self._copy(e, u, m).start(priority=1)

    def finish(self, e, u, slot):
        rows = pl.ds(pl.multiple_of(u * self.unit_rows, self.unit_rows), self.unit_rows)
        for m in range(len(self.w_refs)):
            self._copy(e, u, m).wait()
            self.wbuf[slot, m, rows, :] = self.stage[u % self.n_stage, m].astype(BF16)

    def prime(self, e):
        for u in range(self.n_stage):
            self.start(e, u)

    def convert(self, e, slot, lo, hi):
        def body(u, carry):
            self.finish(e, u, slot)

            @pl.when(u + self.n_stage < self.n_units)
            def _():
                self.start(e, u + self.n_stage)

            return carry

        lax.fori_loop(lo, hi, body, 0)


def _moe_step(plan, stream, compute):
    be_ref, nxt_ref, slot_ref, first_ref, ulo_ref, uhi_ref, nu_ref = plan
    blk = pl.program_id(0)

    @pl.when(blk < nu_ref[0])
    def _():
        e = be_ref[blk]
        ne = nxt_ref[blk]
        slot = slot_ref[blk]

        @pl.when(blk == 0)
        def _():
            stream.prime(e)
            stream.convert(e, slot, 0, stream.n_units)

        @pl.when((first_ref[blk] == 1) & (ne >= 0))
        def _():
            stream.prime(ne)

        compute(slot)

        @pl.when(ne >= 0)
        def _():
            stream.convert(ne, 1 - slot, ulo_ref[blk], uhi_ref[blk])


def _row_variants(n_valid, fn):
    half = MOE_BLK // 2

    @pl.when(n_valid > half)
    def _():
        fn(MOE_BLK)

    @pl.when(n_valid <= half)
    def _():
        fn(half)


def _moe_up_kernel(be_ref, nxt_ref, slot_ref, first_ref, ulo_ref, uhi_ref, nu_ref, nval_ref,
                   xs_ref, wg_ref, wu_ref, bg_ref, bu_ref, act_ref, wbuf, stage, sems):
    stream = _WeightStream((wg_ref, wu_ref), stage, wbuf, sems, MOE_UNIT)
    half = act_ref.shape[1] // 2

    def compute(slot):
        def rows_fn(rows):
            xb = xs_ref[0:rows, :].astype(BF16)
            for c in range(2):
                cols = slice(c * half, (c + 1) * half)
                gate = jnp.dot(xb, wbuf[slot, 0, :, cols], preferred_element_type=F32) + bg_ref[:, cols]
                up = jnp.dot(xb, wbuf[slot, 1, :, cols], preferred_element_type=F32) + bu_ref[:, cols]
                gate = jnp.minimum(gate, SWIGLU_LIMIT)
                up = jnp.clip(up, -SWIGLU_LIMIT, SWIGLU_LIMIT)
                act = gate * jax.nn.sigmoid(SWIGLU_ALPHA * gate) * (up + 1.0)
                act_ref[0:rows, cols] = act.astype(act_ref.dtype)
            if rows < MOE_BLK:
                act_ref[rows:, :] = jnp.zeros((MOE_BLK - rows, act_ref.shape[1]), act_ref.dtype)

        _row_variants(nval_ref[pl.program_id(0)], rows_fn)

    _moe_step((be_ref, nxt_ref, slot_ref, first_ref, ulo_ref, uhi_ref, nu_ref), stream, compute)


def _moe_down_kernel(be_ref, nxt_ref, slot_ref, first_ref, ulo_ref, uhi_ref, nu_ref, nval_ref,
                     act_ref, wd_ref, bd_ref, y_ref, wbuf, stage, sems):
    stream = _WeightStream((wd_ref,), stage, wbuf, sems, MOE_UNIT)

    def compute(slot):
        def rows_fn(rows):
            y_ref[0:rows, :] = (jnp.dot(act_ref[0:rows, :], wbuf[slot, 0], preferred_element_type=F32)
                                + bd_ref[...])
            if rows < MOE_BLK:
                y_ref[rows:, :] = jnp.zeros((MOE_BLK - rows, y_ref.shape[1]), y_ref.dtype)

        _row_variants(nval_ref[pl.program_id(0)], rows_fn)

    _moe_step((be_ref, nxt_ref, slot_ref, first_ref, ulo_ref, uhi_ref, nu_ref), stream, compute)


def _moe_plan(cnt, n_blk, n_units):
    E = cnt.shape[0]
    nblk_e = (cnt + MOE_BLK - 1) // MOE_BLK
    blk_end = jnp.cumsum(nblk_e)
    blk_first = blk_end - nblk_e
    j = jnp.arange(n_blk, dtype=I32)
    be = jnp.minimum(jnp.sum((blk_end[None, :] <= j[:, None]).astype(I32), axis=1), E - 1)
    k = j - blk_first[be]
    nb = jnp.maximum(nblk_e[be], 1)
    nonempty = nblk_e > 0
    ids = jnp.arange(E, dtype=I32)
    later = (ids[None, :] > ids[:, None]) & nonempty[None, :]
    nxt_e = jnp.min(jnp.where(later, ids[None, :], E), axis=1)
    nxt_e = jnp.where(nxt_e >= E, -1, nxt_e)
    slot_e = (jnp.cumsum(nonempty.astype(I32)) - 1) % 2
    n_valid = jnp.clip(cnt[be] - k * MOE_BLK, 0, MOE_BLK)
    plan = (be, nxt_e[be], slot_e[be], (k == 0).astype(I32),
            (k * n_units) // nb, ((k + 1) * n_units) // nb, blk_end[-1:].astype(I32), n_valid)
    return tuple(a.astype(I32) for a in plan)


def _moe(xs, cnt, w_gate, b_gate, w_up, b_up, w_down, b_down):
    P, D = xs.shape
    E, _, F = w_gate.shape
    n_blk = P // MOE_BLK
    n_stage = MOE_STAGES
    clamp = lambda blk, nu: jnp.maximum(jnp.minimum(blk, nu[0] - 1), 0)
    row_map = lambda blk, be, nx, sl, fi, lo, hi, nu, nv: (clamp(blk, nu), 0)
    exp_map = lambda blk, be, nx, sl, fi, lo, hi, nu, nv: (be[clamp(blk, nu)], 0, 0)
    hbm = pl.BlockSpec(memory_space=pl.ANY)

    assert D == F, "one streaming plan serves all three expert matrices"
    plan = _moe_plan(cnt, n_blk, D // MOE_UNIT)
    act = pl.pallas_call(
        _moe_up_kernel,
        grid_spec=pltpu.PrefetchScalarGridSpec(
            num_scalar_prefetch=8,
            grid=(n_blk,),
            in_specs=[pl.BlockSpec((MOE_BLK, D), row_map), hbm, hbm,
                      pl.BlockSpec((None, 1, F), exp_map), pl.BlockSpec((None, 1, F), exp_map)],
            out_specs=pl.BlockSpec((MOE_BLK, F), row_map),
            scratch_shapes=[pltpu.VMEM((2, 2, D, F), BF16),
                            pltpu.VMEM((n_stage, 2, MOE_UNIT, F), F32),
                            pltpu.SemaphoreType.DMA((n_stage, 2))],
        ),
        out_shape=jax.ShapeDtypeStruct((P, F), BF16),
        compiler_params=_cparams(("arbitrary",), VMEM_LIMIT),
        name="moe_up",
    )(*plan, xs, w_gate, w_up, b_gate.reshape(E, 1, F), b_up.reshape(E, 1, F))

    return pl.pallas_call(
        _moe_down_kernel,
        grid_spec=pltpu.PrefetchScalarGridSpec(
            num_scalar_prefetch=8,
            grid=(n_blk,),
            in_specs=[pl.BlockSpec((MOE_BLK, F), row_map), hbm, pl.BlockSpec((None, 1, D), exp_map)],
            out_specs=pl.BlockSpec((MOE_BLK, D), row_map),
            scratch_shapes=[pltpu.VMEM((2, 1, F, D), BF16),
                            pltpu.VMEM((n_stage, 1, MOE_UNIT, D), F32),
                            pltpu.SemaphoreType.DMA((n_stage, 1))],
        ),
        out_shape=jax.ShapeDtypeStruct((P, D), F32),
        compiler_params=_cparams(("arbitrary",), VMEM_LIMIT),
        name="moe_down",
    )(*plan, act, w_down, b_down.reshape(E, 1, D))


def _combine_kernel(dest_ref, dnext_ref, h2_ref, gate_ref, g_ref, y_ref, o_ref, ybuf, sem):
    i = pl.program_id(0)
    slot = i % 2

    def issue(dref, s, unrolled):
        def one(r):
            for k in range(TOP_K):
                pltpu.make_async_copy(y_ref.at[pl.ds(dref[r * TOP_K + k], 1), :],
                                      ybuf.at[s, k, pl.ds(r, 1), :], sem.at[s]).start()

        _for_rows(TOK_TILE, one, unrolled)

    @pl.when(i == 0)
    def _():
        issue(dest_ref, 0, False)

    for s in range(2):
        @pl.when((slot == s) & (i + 1 < pl.num_programs(0)))
        def _():
            issue(dnext_ref, 1 - s, True)

    for k in range(TOP_K):
        pltpu.make_async_copy(y_ref.at[pl.ds(0, TOK_TILE), :], ybuf.at[slot, k], sem.at[slot]).wait()
    gates = gate_ref[...]
    acc = h2_ref[...]
    for k in range(TOP_K):
        acc = acc + gates[:, k:k + 1] * ybuf[slot, k]
    o_ref[...] = _rms(acc, g_ref[...])


def _combine(h2, gates, dest_flat, y, g, n_seq):
    D = h2.shape[1]
    n_tiles = n_seq // TOK_TILE
    return pl.pallas_call(
        _combine_kernel,
        grid=(n_tiles,),
        in_specs=[pl.BlockSpec((TOK_TILE * TOP_K,), lambda i: (i,), memory_space=pltpu.SMEM),
                  pl.BlockSpec((TOK_TILE * TOP_K,), lambda i: (jnp.minimum(i + 1, n_tiles - 1),),
                               memory_space=pltpu.SMEM),
                  pl.BlockSpec((TOK_TILE, D), lambda i: (i, 0)),
                  pl.BlockSpec((TOK_TILE, LANES), lambda i: (i, 0)),
                  pl.BlockSpec((1, D), lambda i: (0, 0)),
                  pl.BlockSpec(memory_space=pl.ANY)],
        out_specs=pl.BlockSpec((TOK_TILE, D), lambda i: (i, 0)),
        out_shape=jax.ShapeDtypeStruct((n_seq, D), F32),
        scratch_shapes=[pltpu.VMEM((2, TOP_K, TOK_TILE, D), F32), pltpu.SemaphoreType.DMA((2,))],
        compiler_params=_cparams(("arbitrary",), VMEM_LIMIT),
        name="combine_norm",
    )(dest_flat, dest_flat, h2, gates, g, y)


def _rope_tables(S, n_meta_rows):
    half = HEAD_DIM // 2
    inv_freq = ROPE_THETA ** (-jnp.arange(half, dtype=F32) / half)
    lane = np.arange(LANES)
    fidx = lane % half
    sign = np.where((lane % HEAD_DIM) < half, -1.0, 1.0).astype(np.float32)
    pos_seq = N_META_TOK + jnp.arange(S)
    tail = jnp.arange(ROW_TILE)
    pos_tail = jnp.where(tail < n_meta_rows, tail % N_META_TOK, 0)
    pos = jnp.concatenate([pos_seq, pos_tail]).astype(F32)
    ang = (pos[:, None] * inv_freq[None, :])[:, fidx]
    return jnp.cos(ang), jnp.sin(ang) * sign[None, :]


def kernel(x, meta_tokens, attn_norm_g, w_in, attn_sinks, lower_bound_logits, rec_norm_g,
           w_attn_proj, w_rec_proj, w_out, ffn_norm_g, w_router, b_router, w_gate, b_gate,
           w_up, b_up, w_down, b_down, final_norm_g):
    B, S, D = x.shape
    assert w_in.shape[0] == 1, "single-layer block"
    assert S % ROW_TILE == 0 and (B * S) % ROW_TILE == 0 and S % SEG == 0
    assert S % SCAN_SEG == 0 and B % SCAN_BATCHES == 0
    NS = B * S
    NM = B * N_META_TOK
    NT = NS + NM
    R = -(-NT // ROW_TILE) * ROW_TILE
    assert R - NS == ROW_TILE and NT % TOK_TILE == 0 and NS % TOK_TILE == 0

    x2d = x.reshape(NS, D)
    tail_rows = jnp.concatenate([jnp.tile(meta_tokens.astype(x.dtype), (B, 1)),
                                 jnp.zeros((R - NT, D), x.dtype)], axis=0)

    w0 = w_in[0]
    kw = N_KV_HEADS * HEAD_DIM
    wq, wk, wv = w0[:, :Q_W], w0[:, Q_W:Q_W + kw], w0[:, Q_W + kw:Q_W + 2 * kw]
    dup = lambda w: jnp.concatenate([w[:, :HEAD_DIM]] * 2 + [w[:, HEAD_DIM:]] * 2, axis=1)
    w_qkv = jnp.concatenate([wq, dup(wk), dup(wv)], axis=1).astype(BF16)
    w_rest = w0[:, Q_W + 2 * kw:].astype(BF16)

    cos_t, sin_t = _rope_tables(S, NM)
    u, qkv = _norm_qkv(x2d, tail_rows, attn_norm_g[0].reshape(1, D), w_qkv, cos_t, sin_t,
                       S // ROW_TILE)
    rest = _proj_rest(u, w_rest)

    attn, attn_tail = _attention(qkv, attn_sinks[0], B, S)
    rec, rec_tail = _hgrn(rest, lower_bound_logits, rec_norm_g[0].reshape(1, REC_DIM), B, S)

    wr_hi = w_router[0].astype(BF16)
    wr_lo = (w_router[0] - wr_hi.astype(F32)).astype(BF16)
    lane_pad = ((0, 0), (0, LANES - N_EXPERTS))
    wr_pad = jnp.concatenate([jnp.pad(wr_hi, lane_pad), jnp.pad(wr_lo, lane_pad)], axis=1)
    br_pad = jnp.pad(b_router[0].reshape(1, N_EXPERTS), ((0, 0), (0, LANES - N_EXPERTS)),
                     constant_values=NEG_INF)
    h2, xn, logits = _mix(attn, attn_tail, rec, rec_tail, rest, x2d, tail_rows,
                          w_attn_proj[0].astype(BF16), w_rec_proj[0].astype(BF16),
                          w_out[0].astype(BF16), ffn_norm_g[0].reshape(1, D), wr_pad, br_pad)

    dest, gates, counts = _route(logits, NT)
    cnt = counts[0, :N_EXPERTS].astype(I32)
    n_blk = -(-(NT * TOP_K + N_EXPERTS * (MOE_BLK - 1)) // MOE_BLK)
    dest_flat = dest[:NT, :TOP_K].reshape(NT * TOP_K)

    xs = _dispatch(xn, dest_flat, NT, n_blk * MOE_BLK)
    y = _moe(xs, cnt, w_gate[0], b_gate[0], w_up[0], b_up[0], w_down[0], b_down[0])
    out = _combine(h2, gates, dest_flat, y, final_norm_g.reshape(1, D), NS)
    return out.reshape(B, S, D)
```

```python
import functools

import jax
import jax.numpy as jnp
import numpy as np
from jax import lax
from jax.experimental import pallas as pl
from jax.experimental.pallas import tpu as pltpu

F32 = jnp.float32
BF16 = jnp.bfloat16
I32 = jnp.int32

N_META_TOK = 16
HEAD_DIM = 64
N_Q_HEADS = 16
N_KV_HEADS = 2
ATTN_BLK = 128
ROPE_THETA = 10000.0
N_REC_HEADS = 8
REC_DIM = 128
REC_CHUNK = 16
N_EXPERTS = 32
TOP_K = 4
SWIGLU_ALPHA = 1.702
SWIGLU_LIMIT = 7.0
NORM_EPS = 1e-5
NEG_INF = -1e30

LANES = 128
ROW_TILE = 512
MIX_TILE = 256
SEG = 256
SCAN_SEG = 256
SCAN_BATCHES = 2
MOE_BLK = 256
MOE_UNIT = 128
MOE_STAGES = 2
TOK_TILE = 128
VMEM_LIMIT = 56 * 1024 * 1024

Q_W = N_Q_HEADS * HEAD_DIM
REC_W = N_REC_HEADS * REC_DIM


def _cparams(sem, vmem=None, **kw):
    return pltpu.CompilerParams(dimension_semantics=sem, vmem_limit_bytes=vmem, **kw)


def _rms(xf, g):
    return xf * lax.rsqrt(jnp.mean(xf * xf, axis=-1, keepdims=True) + NORM_EPS) * g


def _for_rows(n, fn, unrolled):
    if unrolled:
        for r in range(n):
            fn(r)
    else:
        def body(r, carry):
            fn(r)
            return carry

        lax.fori_loop(0, n, body, 0)


def _norm_qkv_kernel(x_ref, tail_ref, g_ref, w_ref, cos_ref, sin_ref, u_ref, qkv_ref, *, n_seq_tiles):
    h = jnp.where(pl.program_id(0) < n_seq_tiles, x_ref[...], tail_ref[...])
    u = _rms(h, g_ref[...]).astype(BF16)
    u_ref[...] = u
    p = jnp.dot(u, w_ref[...], preferred_element_type=F32)
    cos = cos_ref[...]
    sin = sin_ref[...]
    lane = lax.broadcasted_iota(I32, cos.shape, 1)
    first_half = (lane % HEAD_DIM) < (HEAD_DIM // 2)
    n_q = Q_W // LANES
    for c in range(n_q + 2):
        xs = p[:, c * LANES:(c + 1) * LANES]
        swapped = jnp.where(first_half, pltpu.roll(xs, LANES - HEAD_DIM // 2, 1),
                            pltpu.roll(xs, HEAD_DIM // 2, 1))
        r = xs * cos + swapped * sin
        if c < n_q:
            r = r * (HEAD_DIM ** -0.5)
        qkv_ref[:, c * LANES:(c + 1) * LANES] = r.astype(BF16)
    v0 = (n_q + 2) * LANES
    qkv_ref[:, v0:] = p[:, v0:].astype(BF16)


def _norm_qkv(x2d, tail_rows, g, w_qkv, cos_t, sin_t, tiles_per_seq):
    NS, D = x2d.shape
    R = NS + tail_rows.shape[0]
    W = w_qkv.shape[1]
    n_seq_tiles = NS // ROW_TILE

    def tab_map(i):
        return (jnp.where(i < n_seq_tiles, i % tiles_per_seq, tiles_per_seq), 0)

    return pl.pallas_call(
        functools.partial(_norm_qkv_kernel, n_seq_tiles=n_seq_tiles),
        grid=(R // ROW_TILE,),
        in_specs=[
            pl.BlockSpec((ROW_TILE, D), lambda i: (jnp.minimum(i, n_seq_tiles - 1), 0)),
            pl.BlockSpec((ROW_TILE, D), lambda i: (0, 0)),
            pl.BlockSpec((1, D), lambda i: (0, 0)),
            pl.BlockSpec((D, W), lambda i: (0, 0)),
            pl.BlockSpec((ROW_TILE, LANES), tab_map),
            pl.BlockSpec((ROW_TILE, LANES), tab_map),
        ],
        out_specs=[
            pl.BlockSpec((ROW_TILE, D), lambda i: (i, 0)),
            pl.BlockSpec((ROW_TILE, W), lambda i: (i, 0)),
        ],
        out_shape=[jax.ShapeDtypeStruct((R, D), BF16), jax.ShapeDtypeStruct((R, W), BF16)],
        compiler_params=_cparams(("parallel",), VMEM_LIMIT),
        name="norm_qkv",
    )(x2d, tail_rows, g, w_qkv, cos_t, sin_t)


def _matmul_kernel(x_ref, w_ref, o_ref):
    o_ref[...] = jnp.dot(x_ref[...], w_ref[...], preferred_element_type=F32).astype(o_ref.dtype)


def _proj_rest(u, w):
    R, D = u.shape
    N = w.shape[1]
    tn = 2048
    tm = ROW_TILE + ROW_TILE // 2
    if R % tm:
        tm = ROW_TILE
    return pl.pallas_call(
        _matmul_kernel,
        grid=(N // tn, R // tm),
        in_specs=[pl.BlockSpec((tm, D), lambda j, i: (i, 0)),
                  pl.BlockSpec((D, tn), lambda j, i: (0, j))],
        out_specs=pl.BlockSpec((tm, tn), lambda j, i: (i, j)),
        out_shape=jax.ShapeDtypeStruct((R, N), F32),
        compiler_params=_cparams(("parallel", "parallel"), VMEM_LIMIT),
        name="proj_rest",
    )(u, w)


def _attn_core(q_ref, k_groups, v_groups, mask_fn, sink_ref, write):
    nkeys = k_groups[0].shape[0]
    rows = q_ref.shape[0]
    heads_per_group = N_Q_HEADS // N_KV_HEADS
    width = heads_per_group * rows
    lo_q = lax.broadcasted_iota(I32, (rows, LANES), 1) < HEAD_DIM
    ki = lax.broadcasted_iota(I32, (nkeys, width), 0)
    qi = lax.broadcasted_iota(I32, (nkeys, width), 1) % rows
    mask = mask_fn(ki, qi)
    top_half = lax.broadcasted_iota(I32, (LANES, rows), 0) < HEAD_DIM
    for g in range(N_KV_HEADS):
        parts = []
        for j in range(heads_per_group // 2):
            qp = q_ref[:, (heads_per_group // 2 * g + j) * LANES:(heads_per_group // 2 * g + j + 1) * LANES]
            zero_q = jnp.zeros_like(qp)
            parts += [jnp.where(lo_q, qp, zero_q), jnp.where(lo_q, zero_q, qp)]
        q_stack = jnp.concatenate(parts, axis=0)
        s = lax.dot_general(k_groups[g], q_stack, (((1,), (1,)), ((), ())),
                            preferred_element_type=F32)
        s = jnp.where(mask, s, NEG_INF)
        sk = sink_ref[:, g * width:(g + 1) * width]
        m = jnp.maximum(jnp.max(s, axis=0, keepdims=True), sk)
        e = jnp.exp(s - m)
        den = jnp.sum(e, axis=0, keepdims=True) + jnp.exp(sk - m)
        prob = (e * (1.0 / den)).astype(BF16)
        o_t = lax.dot_general(v_groups[g], prob, (((0,), (0,)), ((), ())), preferred_element_type=F32)
        for j in range(heads_per_group // 2):
            even = o_t[:, (2 * j) * rows:(2 * j + 1) * rows]
            odd = o_t[:, (2 * j + 1) * rows:(2 * j + 2) * rows]
            pair_t = jnp.where(top_half, even, odd)
            write(heads_per_group // 2 * g + j, pair_t.T)


def _attn_seq_kernel(sink_ref, q_ref, kc0, kc1, vc0, vc1, kp0, kp1, vp0, vp1,
                     km0, km1, vm0, vm1, o_ref):
    n = pl.program_id(1)
    metas = ((km0, vm0), (km1, vm1))
    curs = ((kc0, vc0), (kc1, vc1))
    prevs = ((kp0, vp0), (kp1, vp1))
    first, second = slice(0, ATTN_BLK), slice(ATTN_BLK, 2 * ATTN_BLK)
    for sub in range(2):
        no_prev = jnp.where(n > 0, 0, 2 * ATTN_BLK) if sub == 0 else 0

        def mask_fn(ki, qi, no_prev=no_prev):
            prev_ok = ((ki >= N_META_TOK) & (ki < N_META_TOK + ATTN_BLK)
                       & (ki - N_META_TOK > qi + no_prev))
            cur_ok = (ki >= N_META_TOK + ATTN_BLK) & (ki - (N_META_TOK + ATTN_BLK) <= qi)
            return (ki < N_META_TOK) | prev_ok | cur_ok

        k_groups, v_groups = [], []
        for g in range(N_KV_HEADS):
            before = ((prevs[g][0][...], prevs[g][1][...]) if sub == 0
                      else (curs[g][0][first, :], curs[g][1][first, :]))
            own = first if sub == 0 else second
            k_groups.append(jnp.concatenate([metas[g][0][...], before[0], curs[g][0][own, :]], axis=0))
            v_groups.append(jnp.concatenate([metas[g][1][...], before[1], curs[g][1][own, :]], axis=0))

        def write(pr, acc, sub=sub):
            o_ref[sub * ATTN_BLK:(sub + 1) * ATTN_BLK, pr * LANES:(pr + 1) * LANES] = acc.astype(o_ref.dtype)

        _attn_core(q_ref.at[pl.ds(sub * ATTN_BLK, ATTN_BLK), :], k_groups, v_groups, mask_fn,
                   sink_ref, write)


def _attn_meta_kernel(sink_ref, q_ref, k0, k1, v0, v1, o_ref):
    nm = q_ref.shape[0]

    def mask_fn(ki, qi):
        return ((qi // N_META_TOK) == (ki // N_META_TOK)) & ((ki % N_META_TOK) <= (qi % N_META_TOK))

    o_ref[...] = jnp.zeros_like(o_ref)

    def write(pr, acc):
        o_ref[0:nm, pr * LANES:(pr + 1) * LANES] = acc.astype(o_ref.dtype)

    _attn_core(q_ref, [k0[...], k1[...]], [v0[...], v1[...]], mask_fn, sink_ref, write)


def _attention(qkv, sinks, B, S):
    R = qkv.shape[0]
    NS = B * S
    NM = B * N_META_TOK
    nb = S // ATTN_BLK
    qc = Q_W // LANES
    sink_seq = jnp.repeat(sinks.astype(F32), ATTN_BLK).reshape(1, N_Q_HEADS * ATTN_BLK)
    sink_meta = jnp.repeat(sinks.astype(F32), NM).reshape(1, N_Q_HEADS * NM)

    pair = 2 * ATTN_BLK
    npair = S // pair

    def kv_spec(col, prev):
        if prev:
            return pl.BlockSpec((ATTN_BLK, LANES), lambda b, n: (b * nb + jnp.maximum(2 * n - 1, 0), col))
        return pl.BlockSpec((pair, LANES), lambda b, n: (b * npair + n, col))

    def meta_spec(col):
        return pl.BlockSpec((N_META_TOK, LANES), lambda b, n: (NS // N_META_TOK + b, col))

    in_specs = [pl.BlockSpec(sink_seq.shape, lambda b, n: (0, 0)),
                pl.BlockSpec((pair, Q_W), lambda b, n: (b * npair + n, 0))]
    in_specs += [kv_spec(qc, False), kv_spec(qc + 1, False), kv_spec(qc + 2, False), kv_spec(qc + 3, False)]
    in_specs += [kv_spec(qc, True), kv_spec(qc + 1, True), kv_spec(qc + 2, True), kv_spec(qc + 3, True)]
    in_specs += [meta_spec(qc), meta_spec(qc + 1), meta_spec(qc + 2), meta_spec(qc + 3)]
    attn = pl.pallas_call(
        _attn_seq_kernel,
        grid=(B, npair),
        in_specs=in_specs,
        out_specs=pl.BlockSpec((pair, Q_W), lambda b, n: (b * npair + n, 0)),
        out_shape=jax.ShapeDtypeStruct((NS, Q_W), BF16),
        compiler_params=_cparams(("parallel", "parallel"), VMEM_LIMIT),
        name="attn_seq",
    )(sink_seq, *([qkv] * 13))

    mb = NS // NM
    tail = R - NS
    blk = lambda col: pl.BlockSpec((NM, LANES), lambda i: (mb, col))
    attn_tail = pl.pallas_call(
        _attn_meta_kernel,
        grid=(1,),
        in_specs=[pl.BlockSpec(sink_meta.shape, lambda i: (0, 0)),
                  pl.BlockSpec((NM, Q_W), lambda i: (mb, 0)),
                  blk(qc), blk(qc + 1), blk(qc + 2), blk(qc + 3)],
        out_specs=pl.BlockSpec((tail, Q_W), lambda i: (0, 0)),
        out_shape=jax.ShapeDtypeStruct((tail, Q_W), BF16),
        compiler_params=_cparams(("arbitrary",), VMEM_LIMIT),
        name="attn_meta",
    )(sink_meta, qkv, qkv, qkv, qkv, qkv)
    return attn, attn_tail


def _hgrn_prep_kernel(rq_ref, rf_ref, ri_ref, lbl_ref, qd_ref, kd_ref, ke_ref, v_ref, eb_ref):
    lbl = lbl_ref[...]
    e = jnp.exp(lbl - jnp.max(lbl, axis=0, keepdims=True))
    lb = e[0:1] / jnp.sum(e, axis=0, keepdims=True)
    f = lb + (1.0 - lb) * jax.nn.sigmoid(rf_ref[...])
    logf = jnp.log(f)
    k = 1.0 - f
    tm = logf.shape[0]
    ri = lax.broadcasted_iota(I32, (tm, tm), 0)
    ci = lax.broadcasted_iota(I32, (tm, tm), 1)
    same_chunk = (ri // REC_CHUNK) == (ci // REC_CHUNK)
    sel = jnp.concatenate([jnp.where(same_chunk & (ci <= ri), 1.0, 0.0).astype(BF16),
                           jnp.where(same_chunk, 1.0, 0.0).astype(BF16)], axis=0)
    hi = logf.astype(BF16)
    rem = logf - hi.astype(F32)
    mid = rem.astype(BF16)
    lo = (rem - mid.astype(F32)).astype(BF16)
    sums = (jnp.dot(sel, hi, preferred_element_type=F32) + jnp.dot(sel, mid, preferred_element_type=F32)
            + jnp.dot(sel, lo, preferred_element_type=F32))
    b = sums[:tm]
    total = sums[tm:]
    eb = jnp.exp(b)
    eb_ref[...] = eb
    qd_ref[...] = (rq_ref[...] * eb).astype(BF16)
    kd_ref[...] = (k * jnp.exp(-b)).astype(BF16)
    ke_ref[...] = (k * jnp.exp(total - b)).astype(BF16)
    v_ref[...] = ri_ref[...].astype(BF16)


def _hgrn_prep(rest, lbl):
    R = rest.shape[0]
    W = REC_W
    col = lambda c: pl.BlockSpec((SEG, W), lambda i: (i, c))
    row_out = pl.BlockSpec((SEG, W), lambda i: (i, 0))
    return pl.pallas_call(
        _hgrn_prep_kernel,
        grid=(R // SEG,),
        in_specs=[col(0), col(1), col(2), pl.BlockSpec(lbl.shape, lambda i: (0, 0))],
        out_specs=[row_out] * 5,
        out_shape=[jax.ShapeDtypeStruct((R, W), BF16)] * 4 + [jax.ShapeDtypeStruct((R, W), F32)],
        compiler_params=_cparams(("parallel",), VMEM_LIMIT),
        name="hgrn_prep",
    )(rest, rest, rest, lbl)


def _hgrn_intra(qd, kd, v):
    n = qd.shape[0]
    ri = lax.broadcasted_iota(I32, (n, n), 0)
    ci = lax.broadcasted_iota(I32, (n, n), 1)
    keep = ((ri // REC_CHUNK) == (ci // REC_CHUNK)) & (ri >= ci)
    sc = lax.dot_general(qd, kd, (((1,), (1,)), ((), ())), preferred_element_type=F32)
    sc = jnp.where(keep, sc, 0.0)
    return jnp.dot(sc.astype(BF16), v, preferred_element_type=F32)


def _hgrn_kv_t(v, ke):
    return lax.dot_general(v, ke, (((0,), (0,)), ((), ())), preferred_element_type=F32)


def _hgrn_finish(o, g, norm_g):
    y = o * lax.rsqrt(jnp.mean(o * o, axis=-1, keepdims=True) + NORM_EPS) * norm_g
    return (y * (g * jax.nn.sigmoid(g))).astype(BF16)


def _hgrn_meta_kernel(qd_ref, kd_ref, ke_ref, v_ref, rg_ref, ng_ref, st_ref, rec_ref, *, nbatch):
    rec_ref[...] = jnp.zeros_like(rec_ref)
    nm = nbatch * REC_CHUNK
    for h in range(N_REC_HEADS):
        cols = slice(h * REC_DIM, (h + 1) * REC_DIM)
        o = _hgrn_intra(qd_ref[0:nm, cols], kd_ref[0:nm, cols], v_ref[0:nm, cols])
        rec_ref[0:nm, cols] = _hgrn_finish(o, rg_ref[0:nm, cols], ng_ref[...])
        for b in range(nbatch):
            rows = slice(b * REC_CHUNK, (b + 1) * REC_CHUNK)
            st_ref[b, h] = _hgrn_kv_t(v_ref[rows, cols], ke_ref[rows, cols])


def _hgrn_scan_kernel(*refs, gb):
    qd, kd, ke, v, eb, rg = (refs[i * gb:(i + 1) * gb] for i in range(6))
    ng_ref, st0_ref, rec_ref, st_scr, o_scr = refs[6 * gb:]
    seg = rec_ref.shape[1]
    heads = [slice(h * REC_DIM, (h + 1) * REC_DIM) for h in range(N_REC_HEADS)]

    @pl.when(pl.program_id(1) == 0)
    def _():
        st_scr[...] = st0_ref[...]

    for i in range(gb):
        for cols in heads:
            o_scr[i, :, cols] = _hgrn_intra(qd[i][:, cols], kd[i][:, cols], v[i][:, cols])

    def chunk(c, carry):
        r0 = pl.multiple_of(c * REC_CHUNK, REC_CHUNK)
        rows = pl.ds(r0, REC_CHUNK)
        for i in range(gb):
            for h, cols in enumerate(heads):
                last8 = eb[i][pl.ds(pl.multiple_of(r0 + REC_CHUNK - 8, 8), 8), cols]
                dec = last8[7:8]
                st = st_scr[i, h]
                o_scr[i, rows, cols] += lax.dot_general(qd[i][rows, cols], st.astype(BF16),
                                                        (((1,), (1,)), ((), ())),
                                                        preferred_element_type=F32)
                st_scr[i, h] = st * dec + _hgrn_kv_t(v[i][rows, cols], ke[i][rows, cols])
        return carry

    lax.fori_loop(0, seg // REC_CHUNK, chunk, 0, unroll=2)
    for i in range(gb):
        for cols in heads:
            rec_ref[i, :, cols] = _hgrn_finish(o_scr[i, :, cols], rg[i][:, cols], ng_ref[...])


def _hgrn(rest, lbl, norm_g, B, S):
    R = rest.shape[0]
    NS = B * S
    W = REC_W
    tail = R - NS
    tb = NS // tail
    qd, kd, ke, v, eb = _hgrn_prep(rest, lbl)

    tail_spec = pl.BlockSpec((tail, W), lambda i: (tb, 0))
    state, rec_tail = pl.pallas_call(
        functools.partial(_hgrn_meta_kernel, nbatch=B),
        grid=(1,),
        in_specs=[tail_spec, tail_spec, tail_spec, tail_spec,
                  pl.BlockSpec((tail, W), lambda i: (tb, 3)),
                  pl.BlockSpec((1, REC_DIM), lambda i: (0, 0))],
        out_specs=[pl.BlockSpec((B, N_REC_HEADS, REC_DIM, REC_DIM), lambda i: (0, 0, 0, 0)),
                   pl.BlockSpec((tail, W), lambda i: (0, 0))],
        out_shape=[jax.ShapeDtypeStruct((B, N_REC_HEADS, REC_DIM, REC_DIM), F32),
                   jax.ShapeDtypeStruct((tail, W), BF16)],
        compiler_params=_cparams(("arbitrary",), VMEM_LIMIT),
        name="hgrn_meta",
    )(qd, kd, ke, v, rest, norm_g)

    gb = SCAN_BATCHES
    ns = S // SCAN_SEG

    def seg_specs(col):
        return [pl.BlockSpec((SCAN_SEG, W), lambda g, s, i=i: ((g * gb + i) * ns + s, col))
                for i in range(gb)]

    rec = pl.pallas_call(
        functools.partial(_hgrn_scan_kernel, gb=gb),
        grid=(B // gb, ns),
        in_specs=seg_specs(0) * 5 + seg_specs(3)
        + [pl.BlockSpec((1, REC_DIM), lambda g, s: (0, 0)),
           pl.BlockSpec((gb, N_REC_HEADS, REC_DIM, REC_DIM), lambda g, s: (g, 0, 0, 0))],
        out_specs=pl.BlockSpec((gb, SCAN_SEG, W), lambda g, s: (g, s, 0)),
        out_shape=jax.ShapeDtypeStruct((B, S, W), BF16),
        scratch_shapes=[pltpu.VMEM((gb, N_REC_HEADS, REC_DIM, REC_DIM), F32),
                        pltpu.VMEM((gb, SCAN_SEG, W), F32)],
        compiler_params=_cparams(("parallel", "arbitrary"), VMEM_LIMIT),
        name="hgrn_scan",
    )(*([qd] * gb + [kd] * gb + [ke] * gb + [v] * gb + [eb] * gb + [rest] * gb), norm_g, state)
    return rec.reshape(NS, W), rec_tail


def _mix_kernel(attn_ref, attn_t_ref, rec_ref, rec_t_ref, ga_ref, gr_ref, x_ref, tail_ref,
                wa_ref, wr_ref, wo_ref, g_ref, wrt_ref, brt_ref, h2_ref, xn_ref, lg_ref, *, n_seq_tiles):
    is_seq = pl.program_id(0) < n_seq_tiles
    attn = jnp.where(is_seq, attn_ref[...], attn_t_ref[...])
    rec = jnp.where(is_seq, rec_ref[...], rec_t_ref[...])
    h = jnp.where(is_seq, x_ref[...], tail_ref[...])
    a = jnp.dot(attn, wa_ref[...], preferred_element_type=F32)
    r = jnp.dot(rec, wr_ref[...], preferred_element_type=F32)
    mixed = jax.nn.sigmoid(ga_ref[...]) * a + jax.nn.sigmoid(gr_ref[...]) * r
    h2 = h + jnp.dot(mixed.astype(BF16), wo_ref[...], preferred_element_type=F32)
    h2_ref[...] = h2
    xn = _rms(h2, g_ref[...])
    xn_ref[...] = xn
    xn_hi = xn.astype(BF16)
    xn_lo = (xn - xn_hi.astype(F32)).astype(BF16)
    w2 = wrt_ref[...]
    p_hi = jnp.dot(xn_hi, w2, preferred_element_type=F32)
    p_lo = jnp.dot(xn_lo, w2[:, :LANES], preferred_element_type=F32)
    lg_ref[...] = p_hi[:, :LANES] + p_hi[:, LANES:] + p_lo + brt_ref[...]


def _mix(attn, attn_tail, rec, rec_tail, rest, x2d, tail_rows, wa, wr, wo, g, w_router, b_router):
    NS, D = x2d.shape
    R = NS + tail_rows.shape[0]
    tm = MIX_TILE
    nst = NS // tm
    const = lambda shape: pl.BlockSpec(shape, lambda i: (0, 0), pipeline_mode=pl.Buffered(1))
    row = lambda w: pl.BlockSpec((tm, w), lambda i: (i, 0))
    seq = lambda w: pl.BlockSpec((tm, w), lambda i: (jnp.minimum(i, nst - 1), 0))
    tl = lambda w: pl.BlockSpec((tm, w), lambda i: (jnp.maximum(i - nst, 0), 0))
    return pl.pallas_call(
        functools.partial(_mix_kernel, n_seq_tiles=nst),
        grid=(R // tm,),
        in_specs=[seq(Q_W), tl(Q_W), seq(REC_W), tl(REC_W),
                  pl.BlockSpec((tm, D), lambda i: (i, 2)), pl.BlockSpec((tm, D), lambda i: (i, 3)),
                  seq(D), tl(D), const(wa.shape), const(wr.shape), const(wo.shape), const(g.shape),
                  const(w_router.shape), const(b_router.shape)],
        out_specs=[row(D), row(D), row(LANES)],
        out_shape=[jax.ShapeDtypeStruct((R, D), F32), jax.ShapeDtypeStruct((R, D), F32),
                   jax.ShapeDtypeStruct((R, LANES), F32)],
        compiler_params=_cparams(("parallel",), VMEM_LIMIT),
        name="mix_outproj",
    )(attn, attn_tail, rec, rec_tail, rest, rest, x2d, tail_rows, wa, wr, wo, g, w_router, b_router)


def _route_kernel(lg_ref, dest_ref, gate_ref, cnt_ref, cnt_scr, carry_scr, start_scr, *, n_tok):
    ph = pl.program_id(0)
    i = pl.program_id(1)
    tm = lg_ref.shape[0]

    @pl.when((ph == 0) & (i == 0))
    def _():
        cnt_scr[...] = jnp.zeros_like(cnt_scr)

    lane = lax.broadcasted_iota(I32, (tm, LANES), 1)
    valid = (i * tm + lax.broadcasted_iota(I32, (tm, LANES), 0)) < n_tok
    work = lg_ref[...]
    onehots, vals = [], []
    for _ in range(TOP_K):
        m = jnp.max(work, axis=-1, keepdims=True)
        idx = jnp.min(jnp.where(work == m, lane, LANES), axis=-1, keepdims=True)
        oh = lane == idx
        onehots.append(oh)
        vals.append(m)
        work = jnp.where(oh, -jnp.inf, work)
    multi = jnp.zeros((tm, LANES), F32)
    for oh in onehots:
        multi = multi + jnp.where(oh & valid, 1.0, 0.0)
    tile_cnt = jnp.sum(multi, axis=0, keepdims=True)

    @pl.when(ph == 0)
    def _():
        cnt_scr[...] += tile_cnt

    @pl.when(ph == 1)
    def _():
        @pl.when(i == 0)
        def _():
            c = cnt_scr[...]
            padded = jnp.ceil(c * (1.0 / MOE_BLK)) * MOE_BLK
            before = (lax.broadcasted_iota(I32, (LANES, LANES), 0)
                      < lax.broadcasted_iota(I32, (LANES, LANES), 1))
            start = jnp.dot(jnp.broadcast_to(padded, (8, LANES)), jnp.where(before, 1.0, 0.0),
                            preferred_element_type=F32, precision=lax.Precision.HIGHEST)
            start_scr[...] = start[0:1]
            carry_scr[...] = jnp.zeros_like(carry_scr)
            cnt_ref[...] = c

        earlier = (lax.broadcasted_iota(I32, (tm, tm), 1) < lax.broadcasted_iota(I32, (tm, tm), 0))
        prefix = jnp.dot(jnp.where(earlier, 1.0, 0.0).astype(BF16), multi.astype(BF16),
                         preferred_element_type=F32)
        base = prefix + carry_scr[...] + start_scr[...]
        den = jnp.zeros_like(vals[0])
        for v in vals:
            den = den + jnp.exp(v - vals[0])
        dest = jnp.zeros((tm, LANES), F32)
        gate = jnp.zeros((tm, LANES), F32)
        for k in range(TOP_K):
            d_k = jnp.sum(jnp.where(onehots[k], base, 0.0), axis=-1, keepdims=True)
            dest = jnp.where(lane == k, d_k, dest)
            gate = jnp.where(lane == k, jnp.exp(vals[k] - vals[0]) / den, gate)
        dest_ref[...] = dest.astype(I32)
        gate_ref[...] = gate
        carry_scr[...] += tile_cnt


def _route(logits, n_tok):
    R = logits.shape[0]
    tm = ROW_TILE
    blk = pl.BlockSpec((tm, LANES), lambda p, i: (i * p, 0))
    return pl.pallas_call(
        functools.partial(_route_kernel, n_tok=n_tok),
        grid=(2, R // tm),
        in_specs=[pl.BlockSpec((tm, LANES), lambda p, i: (i, 0))],
        out_specs=[blk, blk, pl.BlockSpec((1, LANES), lambda p, i: (0, 0))],
        out_shape=[jax.ShapeDtypeStruct((R, LANES), I32), jax.ShapeDtypeStruct((R, LANES), F32),
                   jax.ShapeDtypeStruct((1, LANES), F32)],
        scratch_shapes=[pltpu.VMEM((1, LANES), F32)] * 3,
        compiler_params=_cparams(("arbitrary", "arbitrary"), VMEM_LIMIT),
        name="route",
    )(logits)


def _dispatch_kernel(dest_ref, xn_ref, xs_ref, sem):
    def one(r):
        for k in range(TOP_K):
            pltpu.make_async_copy(xn_ref.at[pl.ds(r, 1), :],
                                  xs_ref.at[pl.ds(dest_ref[r * TOP_K + k], 1), :], sem).start()

    _for_rows(TOK_TILE, one, True)
    for _ in range(TOP_K):
        pltpu.make_async_copy(xn_ref, xs_ref.at[pl.ds(0, TOK_TILE), :], sem).wait()


def _dispatch(xn, dest_flat, n_tok, n_slots):
    D = xn.shape[1]
    return pl.pallas_call(
        _dispatch_kernel,
        grid=(n_tok // TOK_TILE,),
        in_specs=[pl.BlockSpec((TOK_TILE * TOP_K,), lambda i: (i,), memory_space=pltpu.SMEM),
                  pl.BlockSpec((TOK_TILE, D), lambda i: (i, 0))],
        out_specs=pl.BlockSpec(memory_space=pl.ANY),
        out_shape=jax.ShapeDtypeStruct((n_slots, D), F32),
        scratch_shapes=[pltpu.SemaphoreType.DMA],
        compiler_params=_cparams(("arbitrary",), VMEM_LIMIT),
        name="dispatch",
    )(dest_flat, xn)


class _WeightStream:
    def __init__(self, w_refs, stage, wbuf, sems, unit_rows):
        self.w_refs, self.stage, self.wbuf, self.sems = w_refs, stage, wbuf, sems
        self.unit_rows = unit_rows
        self.n_units = w_refs[0].shape[1] // unit_rows
        self.n_stage = stage.shape[0]

    def _copy(self, e, u, m):
        rows = pl.ds(pl.multiple_of(u * self.unit_rows, self.unit_rows), self.unit_rows)
        st = u % self.n_stage
        return pltpu.make_async_copy(self.w_refs[m].at[e, rows, :], self.stage.at[st, m],
                                     self.sems.at[st, m])

    def start(self, e, u):
        for m in range(len(self.w_refs)):
            self._copy(e, u, m).start(priority=1)

    def finish(self, e, u, slot):
        rows = pl.ds(pl.multiple_of(u * self.unit_rows, self.unit_rows), self.unit_rows)
        for m in range(len(self.w_refs)):
            self._copy(e, u, m).wait()
            self.wbuf[slot, m, rows, :] = self.stage[u % self.n_stage, m].astype(BF16)

    def prime(self, e):
        for u in range(self.n_stage):
            self.start(e, u)

    def convert(self, e, slot, lo, hi):
        def body(u, carry):
            self.finish(e, u, slot)

            @pl.when(u + self.n_stage < self.n_units)
            def _():
                self.start(e, u + self.n_stage)

            return carry

        lax.fori_loop(lo, hi, body, 0)


def _moe_step(plan, stream, compute):
    be_ref, nxt_ref, slot_ref, first_ref, ulo_ref, uhi_ref, nu_ref = plan
    blk = pl.program_id(0)

    @pl.when(blk < nu_ref[0])
    def _():
        e = be_ref[blk]
        ne = nxt_ref[blk]
        slot = slot_ref[blk]

        @pl.when(blk == 0)
        def _():
            stream.prime(e)
            stream.convert(e, slot, 0, stream.n_units)

        @pl.when((first_ref[blk] == 1) & (ne >= 0))
        def _():
            stream.prime(ne)

        compute(slot, e)

        @pl.when(ne >= 0)
        def _():
            stream.convert(ne, 1 - slot, ulo_ref[blk], uhi_ref[blk])


def _row_variants(n_valid, fn):
    half = MOE_BLK // 2

    @pl.when(n_valid > half)
    def _():
        fn(MOE_BLK)

    @pl.when(n_valid <= half)
    def _():
        fn(half)


def _moe_up_kernel(be_ref, nxt_ref, slot_ref, first_ref, ulo_ref, uhi_ref, nu_ref, nval_ref,
                   xs_ref, wg_ref, wu_ref, bg_ref, bu_ref, act_ref, wbuf, stage, sems):
    stream = _WeightStream((wg_ref, wu_ref), stage, wbuf, sems, MOE_UNIT)
    half = act_ref.shape[1] // 2

    def compute(slot, e):
        def rows_fn(rows):
            xb = xs_ref[0:rows, :].astype(BF16)
            for c in range(2):
                cols = slice(c * half, (c + 1) * half)
                gate = jnp.dot(xb, wbuf[slot, 0, :, cols], preferred_element_type=F32) + bg_ref[e][:, cols]
                up = jnp.dot(xb, wbuf[slot, 1, :, cols], preferred_element_type=F32) + bu_ref[e][:, cols]
                gate = jnp.minimum(gate, SWIGLU_LIMIT)
                up = jnp.clip(up, -SWIGLU_LIMIT, SWIGLU_LIMIT)
                act = gate * jax.nn.sigmoid(SWIGLU_ALPHA * gate) * (up + 1.0)
                act_ref[0:rows, cols] = act.astype(act_ref.dtype)
            if rows < MOE_BLK:
                act_ref[rows:, :] = jnp.zeros((MOE_BLK - rows, act_ref.shape[1]), act_ref.dtype)

        _row_variants(nval_ref[pl.program_id(0)], rows_fn)

    _moe_step((be_ref, nxt_ref, slot_ref, first_ref, ulo_ref, uhi_ref, nu_ref), stream, compute)


def _moe_down_kernel(be_ref, nxt_ref, slot_ref, first_ref, ulo_ref, uhi_ref, nu_ref, nval_ref,
                     act_ref, wd_ref, bd_ref, y_ref, wbuf, stage, sems):
    stream = _WeightStream((wd_ref,), stage, wbuf, sems, MOE_UNIT)

    def compute(slot, e):
        def rows_fn(rows):
            y_ref[0:rows, :] = (jnp.dot(act_ref[0:rows, :], wbuf[slot, 0], preferred_element_type=F32)
                                + bd_ref[e])
            if rows < MOE_BLK:
                y_ref[rows:, :] = jnp.zeros((MOE_BLK - rows, y_ref.shape[1]), y_ref.dtype)

        _row_variants(nval_ref[pl.program_id(0)], rows_fn)

    _moe_step((be_ref, nxt_ref, slot_ref, first_ref, ulo_ref, uhi_ref, nu_ref), stream, compute)


def _moe_plan(cnt, n_blk, n_units):
    E = cnt.shape[0]
    nblk_e = (cnt + MOE_BLK - 1) // MOE_BLK
    blk_end = jnp.cumsum(nblk_e)
    blk_first = blk_end - nblk_e
    j = jnp.arange(n_blk, dtype=I32)
    be = jnp.minimum(jnp.sum((blk_end[None, :] <= j[:, None]).astype(I32), axis=1), E - 1)
    ids = jnp.arange(E, dtype=I32)
    own = be[:, None] == ids[None, :]

    def per_block(table):
        return jnp.sum(jnp.where(own, table[None, :], 0), axis=1)

    k = j - per_block(blk_first)
    nb = jnp.maximum(per_block(nblk_e), 1)
    nonempty = nblk_e > 0
    later = (ids[None, :] > ids[:, None]) & nonempty[None, :]
    nxt_e = jnp.min(jnp.where(later, ids[None, :], E), axis=1)
    nxt_e = jnp.where(nxt_e >= E, -1, nxt_e)
    slot_e = (jnp.cumsum(nonempty.astype(I32)) - 1) % 2
    n_valid = jnp.clip(per_block(cnt) - k * MOE_BLK, 0, MOE_BLK)
    plan = (be, per_block(nxt_e), per_block(slot_e), (k == 0).astype(I32),
            (k * n_units) // nb, ((k + 1) * n_units) // nb, blk_end[-1:].astype(I32), n_valid)
    return tuple(a.astype(I32) for a in plan)


def _moe(xs, cnt, w_gate, b_gate, w_up, b_up, w_down, b_down):
    P, D = xs.shape
    E, _, F = w_gate.shape
    n_blk = P // MOE_BLK
    n_stage = MOE_STAGES
    clamp = lambda blk, nu: jnp.maximum(jnp.minimum(blk, nu[0] - 1), 0)
    row_map = lambda blk, be, nx, sl, fi, lo, hi, nu, nv: (clamp(blk, nu), 0)
    hbm = pl.BlockSpec(memory_space=pl.ANY)

    def bias_spec(width):
        return pl.BlockSpec((E, 1, width), lambda blk, *plan: (0, 0, 0), pipeline_mode=pl.Buffered(1))

    assert D == F, "one streaming plan serves all three expert matrices"
    plan = _moe_plan(cnt, n_blk, D // MOE_UNIT)
    act = pl.pallas_call(
        _moe_up_kernel,
        grid_spec=pltpu.PrefetchScalarGridSpec(
            num_scalar_prefetch=8,
            grid=(n_blk,),
            in_specs=[pl.BlockSpec((MOE_BLK, D), row_map), hbm, hbm, bias_spec(F), bias_spec(F)],
            out_specs=pl.BlockSpec((MOE_BLK, F), row_map),
            scratch_shapes=[pltpu.VMEM((2, 2, D, F), BF16),
                            pltpu.VMEM((n_stage, 2, MOE_UNIT, F), F32),
                            pltpu.SemaphoreType.DMA((n_stage, 2))],
        ),
        out_shape=jax.ShapeDtypeStruct((P, F), BF16),
        compiler_params=_cparams(("arbitrary",), VMEM_LIMIT),
        name="moe_up",
    )(*plan, xs, w_gate, w_up, b_gate.reshape(E, 1, F), b_up.reshape(E, 1, F))

    return pl.pallas_call(
        _moe_down_kernel,
        grid_spec=pltpu.PrefetchScalarGridSpec(
            num_scalar_prefetch=8,
            grid=(n_blk,),
            in_specs=[pl.BlockSpec((MOE_BLK, F), row_map), hbm, bias_spec(D)],
            out_specs=pl.BlockSpec((MOE_BLK, D), row_map),
            scratch_shapes=[pltpu.VMEM((2, 1, F, D), BF16),
                            pltpu.VMEM((n_stage, 1, MOE_UNIT, D), F32),
                            pltpu.SemaphoreType.DMA((n_stage, 1))],
        ),
        out_shape=jax.ShapeDtypeStruct((P, D), F32),
        compiler_params=_cparams(("arbitrary",), VMEM_LIMIT),
        name="moe_down",
    )(*plan, act, w_down, b_down.reshape(E, 1, D))


def _combine_kernel(dest_ref, dnext_ref, h2_ref, gate_ref, g_ref, y_ref, o_ref, ybuf, sem):
    i = pl.program_id(0)
    slot = i % 2

    def issue(dref, s, unrolled):
        def one(r):
            for k in range(TOP_K):
                pltpu.make_async_copy(y_ref.at[pl.ds(dref[r * TOP_K + k], 1), :],
                                      ybuf.at[s, k, pl.ds(r, 1), :], sem.at[s]).start()

        _for_rows(TOK_TILE, one, unrolled)

    @pl.when(i == 0)
    def _():
        issue(dest_ref, 0, False)

    for s in range(2):
        @pl.when((slot == s) & (i + 1 < pl.num_programs(0)))
        def _():
            issue(dnext_ref, 1 - s, True)

    for k in range(TOP_K):
        pltpu.make_async_copy(y_ref.at[pl.ds(0, TOK_TILE), :], ybuf.at[slot, k], sem.at[slot]).wait()
    gates = gate_ref[...]
    acc = h2_ref[...]
    for k in range(TOP_K):
        acc = acc + gates[:, k:k + 1] * ybuf[slot, k]
    o_ref[...] = _rms(acc, g_ref[...])


def _combine(h2, gates, dest_flat, y, g, n_seq):
    D = h2.shape[1]
    n_tiles = n_seq // TOK_TILE
    return pl.pallas_call(
        _combine_kernel,
        grid=(n_tiles,),
        in_specs=[pl.BlockSpec((TOK_TILE * TOP_K,), lambda i: (i,), memory_space=pltpu.SMEM),
                  pl.BlockSpec((TOK_TILE * TOP_K,), lambda i: (jnp.minimum(i + 1, n_tiles - 1),),
                               memory_space=pltpu.SMEM),
                  pl.BlockSpec((TOK_TILE, D), lambda i: (i, 0)),
                  pl.BlockSpec((TOK_TILE, LANES), lambda i: (i, 0)),
                  pl.BlockSpec((1, D), lambda i: (0, 0)),
                  pl.BlockSpec(memory_space=pl.ANY)],
        out_specs=pl.BlockSpec((TOK_TILE, D), lambda i: (i, 0)),
        out_shape=jax.ShapeDtypeStruct((n_seq, D), F32),
        scratch_shapes=[pltpu.VMEM((2, TOP_K, TOK_TILE, D), F32), pltpu.SemaphoreType.DMA((2,))],
        compiler_params=_cparams(("arbitrary",), VMEM_LIMIT),
        name="combine_norm",
    )(dest_flat, dest_flat, h2, gates, g, y)


def _rope_tables(S, n_meta_rows):
    half = HEAD_DIM // 2
    inv_freq = ROPE_THETA ** (-jnp.arange(half, dtype=F32) / half)
    lane = np.arange(LANES)
    fidx = lane % half
    sign = np.where((lane % HEAD_DIM) < half, -1.0, 1.0).astype(np.float32)
    pos_seq = N_META_TOK + jnp.arange(S)
    tail = jnp.arange(ROW_TILE)
    pos_tail = jnp.where(tail < n_meta_rows, tail % N_META_TOK, 0)
    pos = jnp.concatenate([pos_seq, pos_tail]).astype(F32)
    ang = (pos[:, None] * inv_freq[None, :])[:, fidx]
    return jnp.cos(ang), jnp.sin(ang) * sign[None, :]


def kernel(x, meta_tokens, attn_norm_g, w_in, attn_sinks, lower_bound_logits, rec_norm_g,
           w_attn_proj, w_rec_proj, w_out, ffn_norm_g, w_router, b_router, w_gate, b_gate,
           w_up, b_up, w_down, b_down, final_norm_g):
    B, S, D = x.shape
    assert w_in.shape[0] == 1, "single-layer block"
    assert S % ROW_TILE == 0 and (B * S) % ROW_TILE == 0 and S % SEG == 0
    assert S % SCAN_SEG == 0 and B % SCAN_BATCHES == 0
    NS = B * S
    NM = B * N_META_TOK
    NT = NS + NM
    R = -(-NT // ROW_TILE) * ROW_TILE
    assert R - NS == ROW_TILE and NT % TOK_TILE == 0 and NS % TOK_TILE == 0

    x2d = x.reshape(NS, D)
    tail_rows = jnp.concatenate([jnp.tile(meta_tokens.astype(x.dtype), (B, 1)),
                                 jnp.zeros((R - NT, D), x.dtype)], axis=0)

    w0 = w_in[0]
    kw = N_KV_HEADS * HEAD_DIM
    wq, wk, wv = w0[:, :Q_W], w0[:, Q_W:Q_W + kw], w0[:, Q_W + kw:Q_W + 2 * kw]
    dup = lambda w: jnp.concatenate([w[:, :HEAD_DIM]] * 2 + [w[:, HEAD_DIM:]] * 2, axis=1)
    w_qkv = jnp.concatenate([wq, dup(wk), dup(wv)], axis=1).astype(BF16)
    w_rest = w0[:, Q_W + 2 * kw:].astype(BF16)

    cos_t, sin_t = _rope_tables(S, NM)
    u, qkv = _norm_qkv(x2d, tail_rows, attn_norm_g[0].reshape(1, D), w_qkv, cos_t, sin_t,
                       S // ROW_TILE)
    rest = _proj_rest(u, w_rest)

    attn, attn_tail = _attention(qkv, attn_sinks[0], B, S)
    rec, rec_tail = _hgrn(rest, lower_bound_logits, rec_norm_g[0].reshape(1, REC_DIM), B, S)

    wr_hi = w_router[0].astype(BF16)
    wr_lo = (w_router[0] - wr_hi.astype(F32)).astype(BF16)
    lane_pad = ((0, 0), (0, LANES - N_EXPERTS))
    wr_pad = jnp.concatenate([jnp.pad(wr_hi, lane_pad), jnp.pad(wr_lo, lane_pad)], axis=1)
    br_pad = jnp.pad(b_router[0].reshape(1, N_EXPERTS), ((0, 0), (0, LANES - N_EXPERTS)),
                     constant_values=NEG_INF)
    h2, xn, logits = _mix(attn, attn_tail, rec, rec_tail, rest, x2d, tail_rows,
                          w_attn_proj[0].astype(BF16), w_rec_proj[0].astype(BF16),
                          w_out[0].astype(BF16), ffn_norm_g[0].reshape(1, D), wr_pad, br_pad)

    dest, gates, counts = _route(logits, NT)
    cnt = counts[0, :N_EXPERTS].astype(I32)
    n_blk = -(-(NT * TOP_K + N_EXPERTS * (MOE_BLK - 1)) // MOE_BLK)
    dest_flat = dest[:NT, :TOP_K].reshape(NT * TOP_K)

    xs = _dispatch(xn, dest_flat, NT, n_blk * MOE_BLK)
    y = _moe(xs, cnt, w_gate[0], b_gate[0], w_up[0], b_up[0], w_down[0], b_down[0])
    out = _combine(h2, gates, dest_flat, y, final_norm_g.reshape(1, D), NS)
    return out.reshape(B, S, D)
```

```python
import functools

import jax
import jax.numpy as jnp
import numpy as np
from jax import lax
from jax.experimental import pallas as pl
from jax.experimental.pallas import tpu as pltpu

F32 = jnp.float32
BF16 = jnp.bfloat16
I32 = jnp.int32

N_META_TOK = 16
HEAD_DIM = 64
N_Q_HEADS = 16
N_KV_HEADS = 2
ATTN_BLK = 128
ROPE_THETA = 10000.0
N_REC_HEADS = 8
REC_DIM = 128
REC_CHUNK = 16
N_EXPERTS = 32
TOP_K = 4
SWIGLU_ALPHA = 1.702
SWIGLU_LIMIT = 7.0
NORM_EPS = 1e-5
NEG_INF = -1e30

LANES = 128
ROW_TILE = 512
MIX_TILE = 256
SEG = 256
SCAN_SEG = 256
SCAN_BATCHES = 2
MOE_BLK = 256
MOE_UNIT = 128
MOE_STAGES = 2
TOK_TILE = 128
VMEM_LIMIT = 56 * 1024 * 1024

Q_W = N_Q_HEADS * HEAD_DIM
REC_W = N_REC_HEADS * REC_DIM


def _cparams(sem, vmem=None, **kw):
    return pltpu.CompilerParams(dimension_semantics=sem, vmem_limit_bytes=vmem, **kw)


def _rms(xf, g):
    return xf * lax.rsqrt(jnp.mean(xf * xf, axis=-1, keepdims=True) + NORM_EPS) * g


def _for_rows(n, fn, unrolled):
    if unrolled:
        for r in range(n):
            fn(r)
    else:
        def body(r, carry):
            fn(r)
            return carry

        lax.fori_loop(0, n, body, 0)


def _norm_qkv_kernel(x_ref, tail_ref, g_ref, w_ref, cos_ref, sin_ref, u_ref, qkv_ref, *, n_seq_tiles):
    h = jnp.where(pl.program_id(0) < n_seq_tiles, x_ref[...], tail_ref[...])
    u = _rms(h, g_ref[...]).astype(BF16)
    u_ref[...] = u
    p = jnp.dot(u, w_ref[...], preferred_element_type=F32)
    cos = cos_ref[...]
    sin = sin_ref[...]
    lane = lax.broadcasted_iota(I32, cos.shape, 1)
    first_half = (lane % HEAD_DIM) < (HEAD_DIM // 2)
    n_q = Q_W // LANES
    for c in range(n_q + 2):
        xs = p[:, c * LANES:(c + 1) * LANES]
        swapped = jnp.where(first_half, pltpu.roll(xs, LANES - HEAD_DIM // 2, 1),
                            pltpu.roll(xs, HEAD_DIM // 2, 1))
        r = xs * cos + swapped * sin
        if c < n_q:
            r = r * (HEAD_DIM ** -0.5)
        qkv_ref[:, c * LANES:(c + 1) * LANES] = r.astype(BF16)
    v0 = (n_q + 2) * LANES
    qkv_ref[:, v0:] = p[:, v0:].astype(BF16)


def _norm_qkv(x2d, tail_rows, g, w_qkv, cos_t, sin_t, tiles_per_seq):
    NS, D = x2d.shape
    R = NS + tail_rows.shape[0]
    W = w_qkv.shape[1]
    n_seq_tiles = NS // ROW_TILE

    def tab_map(i):
        return (jnp.where(i < n_seq_tiles, i % tiles_per_seq, tiles_per_seq), 0)

    return pl.pallas_call(
        functools.partial(_norm_qkv_kernel, n_seq_tiles=n_seq_tiles),
        grid=(R // ROW_TILE,),
        in_specs=[
            pl.BlockSpec((ROW_TILE, D), lambda i: (jnp.minimum(i, n_seq_tiles - 1), 0)),
            pl.BlockSpec((ROW_TILE, D), lambda i: (0, 0)),
            pl.BlockSpec((1, D), lambda i: (0, 0)),
            pl.BlockSpec((D, W), lambda i: (0, 0)),
            pl.BlockSpec((ROW_TILE, LANES), tab_map),
            pl.BlockSpec((ROW_TILE, LANES), tab_map),
        ],
        out_specs=[
            pl.BlockSpec((ROW_TILE, D), lambda i: (i, 0)),
            pl.BlockSpec((ROW_TILE, W), lambda i: (i, 0)),
        ],
        out_shape=[jax.ShapeDtypeStruct((R, D), BF16), jax.ShapeDtypeStruct((R, W), BF16)],
        compiler_params=_cparams(("parallel",), VMEM_LIMIT),
        name="norm_qkv",
    )(x2d, tail_rows, g, w_qkv, cos_t, sin_t)


def _matmul_kernel(x_ref, w_ref, o_ref):
    o_ref[...] = jnp.dot(x_ref[...], w_ref[...], preferred_element_type=F32).astype(o_ref.dtype)


def _proj_rest(u, w):
    R, D = u.shape
    N = w.shape[1]
    tn = 2048
    tm = ROW_TILE + ROW_TILE // 2
    if R % tm:
        tm = ROW_TILE
    return pl.pallas_call(
        _matmul_kernel,
        grid=(N // tn, R // tm),
        in_specs=[pl.BlockSpec((tm, D), lambda j, i: (i, 0)),
                  pl.BlockSpec((D, tn), lambda j, i: (0, j))],
        out_specs=pl.BlockSpec((tm, tn), lambda j, i: (i, j)),
        out_shape=jax.ShapeDtypeStruct((R, N), F32),
        compiler_params=_cparams(("parallel", "parallel"), VMEM_LIMIT),
        name="proj_rest",
    )(u, w)


def _attn_core(q_ref, k_groups, v_groups, mask_fn, sink_ref, write):
    nkeys = k_groups[0].shape[0]
    rows = q_ref.shape[0]
    heads_per_group = N_Q_HEADS // N_KV_HEADS
    width = heads_per_group * rows
    lo_q = lax.broadcasted_iota(I32, (rows, LANES), 1) < HEAD_DIM
    ki = lax.broadcasted_iota(I32, (nkeys, width), 0)
    qi = lax.broadcasted_iota(I32, (nkeys, width), 1) % rows
    mask = mask_fn(ki, qi)
    top_half = lax.broadcasted_iota(I32, (LANES, rows), 0) < HEAD_DIM
    for g in range(N_KV_HEADS):
        parts = []
        for j in range(heads_per_group // 2):
            qp = q_ref[:, (heads_per_group // 2 * g + j) * LANES:(heads_per_group // 2 * g + j + 1) * LANES]
            zero_q = jnp.zeros_like(qp)
            parts += [jnp.where(lo_q, qp, zero_q), jnp.where(lo_q, zero_q, qp)]
        q_stack = jnp.concatenate(parts, axis=0)
        s = lax.dot_general(k_groups[g], q_stack, (((1,), (1,)), ((), ())),
                            preferred_element_type=F32)
        s = jnp.where(mask, s, NEG_INF)
        sk = sink_ref[:, g * width:(g + 1) * width]
        m = jnp.maximum(jnp.max(s, axis=0, keepdims=True), sk)
        e = jnp.exp(s - m)
        den = jnp.sum(e, axis=0, keepdims=True) + jnp.exp(sk - m)
        prob = (e * (1.0 / den)).astype(BF16)
        o_t = lax.dot_general(v_groups[g], prob, (((0,), (0,)), ((), ())), preferred_element_type=F32)
        for j in range(heads_per_group // 2):
            even = o_t[:, (2 * j) * rows:(2 * j + 1) * rows]
            odd = o_t[:, (2 * j + 1) * rows:(2 * j + 2) * rows]
            pair_t = jnp.where(top_half, even, odd)
            write(heads_per_group // 2 * g + j, pair_t.T)


def _attn_seq_kernel(sink_ref, q_ref, kc0, kc1, vc0, vc1, kp0, kp1, vp0, vp1,
                     km0, km1, vm0, vm1, o_ref):
    n = pl.program_id(1)
    metas = ((km0, vm0), (km1, vm1))
    curs = ((kc0, vc0), (kc1, vc1))
    prevs = ((kp0, vp0), (kp1, vp1))
    first, second = slice(0, ATTN_BLK), slice(ATTN_BLK, 2 * ATTN_BLK)
    for sub in range(2):
        no_prev = jnp.where(n > 0, 0, 2 * ATTN_BLK) if sub == 0 else 0

        def mask_fn(ki, qi, no_prev=no_prev):
            prev_ok = ((ki >= N_META_TOK) & (ki < N_META_TOK + ATTN_BLK)
                       & (ki - N_META_TOK > qi + no_prev))
            cur_ok = (ki >= N_META_TOK + ATTN_BLK) & (ki - (N_META_TOK + ATTN_BLK) <= qi)
            return (ki < N_META_TOK) | prev_ok | cur_ok

        k_groups, v_groups = [], []
        for g in range(N_KV_HEADS):
            before = ((prevs[g][0][...], prevs[g][1][...]) if sub == 0
                      else (curs[g][0][first, :], curs[g][1][first, :]))
            own = first if sub == 0 else second
            k_groups.append(jnp.concatenate([metas[g][0][...], before[0], curs[g][0][own, :]], axis=0))
            v_groups.append(jnp.concatenate([metas[g][1][...], before[1], curs[g][1][own, :]], axis=0))

        def write(pr, acc, sub=sub):
            o_ref[sub * ATTN_BLK:(sub + 1) * ATTN_BLK, pr * LANES:(pr + 1) * LANES] = acc.astype(o_ref.dtype)

        _attn_core(q_ref.at[pl.ds(sub * ATTN_BLK, ATTN_BLK), :], k_groups, v_groups, mask_fn,
                   sink_ref, write)


def _attn_meta_kernel(sink_ref, q_ref, k0, k1, v0, v1, o_ref):
    nm = q_ref.shape[0]

    def mask_fn(ki, qi):
        return ((qi // N_META_TOK) == (ki // N_META_TOK)) & ((ki % N_META_TOK) <= (qi % N_META_TOK))

    o_ref[...] = jnp.zeros_like(o_ref)

    def write(pr, acc):
        o_ref[0:nm, pr * LANES:(pr + 1) * LANES] = acc.astype(o_ref.dtype)

    _attn_core(q_ref, [k0[...], k1[...]], [v0[...], v1[...]], mask_fn, sink_ref, write)


def _attention(qkv, sinks, B, S):
    R = qkv.shape[0]
    NS = B * S
    NM = B * N_META_TOK
    nb = S // ATTN_BLK
    qc = Q_W // LANES
    sink_seq = jnp.repeat(sinks.astype(F32), ATTN_BLK).reshape(1, N_Q_HEADS * ATTN_BLK)
    sink_meta = jnp.repeat(sinks.astype(F32), NM).reshape(1, N_Q_HEADS * NM)

    pair = 2 * ATTN_BLK
    npair = S // pair

    def kv_spec(col, prev):
        if prev:
            return pl.BlockSpec((ATTN_BLK, LANES), lambda b, n: (b * nb + jnp.maximum(2 * n - 1, 0), col))
        return pl.BlockSpec((pair, LANES), lambda b, n: (b * npair + n, col))

    def meta_spec(col):
        return pl.BlockSpec((N_META_TOK, LANES), lambda b, n: (NS // N_META_TOK + b, col))

    in_specs = [pl.BlockSpec(sink_seq.shape, lambda b, n: (0, 0)),
                pl.BlockSpec((pair, Q_W), lambda b, n: (b * npair + n, 0))]
    in_specs += [kv_spec(qc, False), kv_spec(qc + 1, False), kv_spec(qc + 2, False), kv_spec(qc + 3, False)]
    in_specs += [kv_spec(qc, True), kv_spec(qc + 1, True), kv_spec(qc + 2, True), kv_spec(qc + 3, True)]
    in_specs += [meta_spec(qc), meta_spec(qc + 1), meta_spec(qc + 2), meta_spec(qc + 3)]
    attn = pl.pallas_call(
        _attn_seq_kernel,
        grid=(B, npair),
        in_specs=in_specs,
        out_specs=pl.BlockSpec((pair, Q_W), lambda b, n: (b * npair + n, 0)),
        out_shape=jax.ShapeDtypeStruct((NS, Q_W), BF16),
        compiler_params=_cparams(("parallel", "parallel"), VMEM_LIMIT),
        name="attn_seq",
    )(sink_seq, *([qkv] * 13))

    mb = NS // NM
    tail = R - NS
    blk = lambda col: pl.BlockSpec((NM, LANES), lambda i: (mb, col))
    attn_tail = pl.pallas_call(
        _attn_meta_kernel,
        grid=(1,),
        in_specs=[pl.BlockSpec(sink_meta.shape, lambda i: (0, 0)),
                  pl.BlockSpec((NM, Q_W), lambda i: (mb, 0)),
                  blk(qc), blk(qc + 1), blk(qc + 2), blk(qc + 3)],
        out_specs=pl.BlockSpec((tail, Q_W), lambda i: (0, 0)),
        out_shape=jax.ShapeDtypeStruct((tail, Q_W), BF16),
        compiler_params=_cparams(("arbitrary",), VMEM_LIMIT),
        name="attn_meta",
    )(sink_meta, qkv, qkv, qkv, qkv, qkv)
    return attn, attn_tail


def _hgrn_prep_kernel(rq_ref, rf_ref, ri_ref, lbl_ref, qd_ref, kd_ref, ke_ref, v_ref, eb_ref):
    lbl = lbl_ref[...]
    e = jnp.exp(lbl - jnp.max(lbl, axis=0, keepdims=True))
    lb = e[0:1] / jnp.sum(e, axis=0, keepdims=True)
    f = lb + (1.0 - lb) * jax.nn.sigmoid(rf_ref[...])
    logf = jnp.log(f)
    k = 1.0 - f
    tm = logf.shape[0]
    ri = lax.broadcasted_iota(I32, (tm, tm), 0)
    ci = lax.broadcasted_iota(I32, (tm, tm), 1)
    same_chunk = (ri // REC_CHUNK) == (ci // REC_CHUNK)
    sel = jnp.concatenate([jnp.where(same_chunk & (ci <= ri), 1.0, 0.0).astype(BF16),
                           jnp.where(same_chunk, 1.0, 0.0).astype(BF16)], axis=0)
    hi = logf.astype(BF16)
    rem = logf - hi.astype(F32)
    mid = rem.astype(BF16)
    lo = (rem - mid.astype(F32)).astype(BF16)
    sums = (jnp.dot(sel, hi, preferred_element_type=F32) + jnp.dot(sel, mid, preferred_element_type=F32)
            + jnp.dot(sel, lo, preferred_element_type=F32))
    b = sums[:tm]
    total = sums[tm:]
    eb = jnp.exp(b)
    eb_ref[...] = eb
    qd_ref[...] = (rq_ref[...] * eb).astype(BF16)
    kd_ref[...] = (k * jnp.exp(-b)).astype(BF16)
    ke_ref[...] = (k * jnp.exp(total - b)).astype(BF16)
    v_ref[...] = ri_ref[...].astype(BF16)


def _hgrn_prep(rest, lbl):
    R = rest.shape[0]
    W = REC_W
    col = lambda c: pl.BlockSpec((SEG, W), lambda i: (i, c))
    row_out = pl.BlockSpec((SEG, W), lambda i: (i, 0))
    return pl.pallas_call(
        _hgrn_prep_kernel,
        grid=(R // SEG,),
        in_specs=[col(0), col(1), col(2), pl.BlockSpec(lbl.shape, lambda i: (0, 0))],
        out_specs=[row_out] * 5,
        out_shape=[jax.ShapeDtypeStruct((R, W), BF16)] * 4 + [jax.ShapeDtypeStruct((R, W), F32)],
        compiler_params=_cparams(("parallel",), VMEM_LIMIT),
        name="hgrn_prep",
    )(rest, rest, rest, lbl)


def _hgrn_intra(qd, kd, v):
    n = qd.shape[0]
    ri = lax.broadcasted_iota(I32, (n, n), 0)
    ci = lax.broadcasted_iota(I32, (n, n), 1)
    keep = ((ri // REC_CHUNK) == (ci // REC_CHUNK)) & (ri >= ci)
    sc = lax.dot_general(qd, kd, (((1,), (1,)), ((), ())), preferred_element_type=F32)
    sc = jnp.where(keep, sc, 0.0)
    return jnp.dot(sc.astype(BF16), v, preferred_element_type=F32)


def _hgrn_kv_t(v, ke):
    return lax.dot_general(v, ke, (((0,), (0,)), ((), ())), preferred_element_type=F32)


def _hgrn_finish(o, g, norm_g):
    y = o * lax.rsqrt(jnp.mean(o * o, axis=-1, keepdims=True) + NORM_EPS) * norm_g
    return (y * (g * jax.nn.sigmoid(g))).astype(BF16)


def _hgrn_meta_kernel(qd_ref, kd_ref, ke_ref, v_ref, rg_ref, ng_ref, st_ref, rec_ref, *, nbatch):
    rec_ref[...] = jnp.zeros_like(rec_ref)
    nm = nbatch * REC_CHUNK
    for h in range(N_REC_HEADS):
        cols = slice(h * REC_DIM, (h + 1) * REC_DIM)
        o = _hgrn_intra(qd_ref[0:nm, cols], kd_ref[0:nm, cols], v_ref[0:nm, cols])
        rec_ref[0:nm, cols] = _hgrn_finish(o, rg_ref[0:nm, cols], ng_ref[...])
        for b in range(nbatch):
            rows = slice(b * REC_CHUNK, (b + 1) * REC_CHUNK)
            st_ref[b, h] = _hgrn_kv_t(v_ref[rows, cols], ke_ref[rows, cols])


def _hgrn_scan_kernel(*refs, gb):
    qd, kd, ke, v, eb, rg = (refs[i * gb:(i + 1) * gb] for i in range(6))
    ng_ref, st0_ref, rec_ref, st_scr, o_scr = refs[6 * gb:]
    seg = rec_ref.shape[1]
    heads = [slice(h * REC_DIM, (h + 1) * REC_DIM) for h in range(N_REC_HEADS)]

    @pl.when(pl.program_id(1) == 0)
    def _():
        st_scr[...] = st0_ref[...]

    for i in range(gb):
        for cols in heads:
            o_scr[i, :, cols] = _hgrn_intra(qd[i][:, cols], kd[i][:, cols], v[i][:, cols])

    def chunk(c, carry):
        r0 = pl.multiple_of(c * REC_CHUNK, REC_CHUNK)
        rows = pl.ds(r0, REC_CHUNK)
        for i in range(gb):
            for h, cols in enumerate(heads):
                last8 = eb[i][pl.ds(pl.multiple_of(r0 + REC_CHUNK - 8, 8), 8), cols]
                dec = last8[7:8]
                st = st_scr[i, h]
                o_scr[i, rows, cols] += lax.dot_general(qd[i][rows, cols], st.astype(BF16),
                                                        (((1,), (1,)), ((), ())),
                                                        preferred_element_type=F32)
                st_scr[i, h] = st * dec + _hgrn_kv_t(v[i][rows, cols], ke[i][rows, cols])
        return carry

    lax.fori_loop(0, seg // REC_CHUNK, chunk, 0, unroll=2)
    for i in range(gb):
        for cols in heads:
            rec_ref[i, :, cols] = _hgrn_finish(o_scr[i, :, cols], rg[i][:, cols], ng_ref[...])


def _hgrn(rest, lbl, norm_g, B, S):
    R = rest.shape[0]
    NS = B * S
    W = REC_W
    tail = R - NS
    tb = NS // tail
    qd, kd, ke, v, eb = _hgrn_prep(rest, lbl)

    tail_spec = pl.BlockSpec((tail, W), lambda i: (tb, 0))
    state, rec_tail = pl.pallas_call(
        functools.partial(_hgrn_meta_kernel, nbatch=B),
        grid=(1,),
        in_specs=[tail_spec, tail_spec, tail_spec, tail_spec,
                  pl.BlockSpec((tail, W), lambda i: (tb, 3)),
                  pl.BlockSpec((1, REC_DIM), lambda i: (0, 0))],
        out_specs=[pl.BlockSpec((B, N_REC_HEADS, REC_DIM, REC_DIM), lambda i: (0, 0, 0, 0)),
                   pl.BlockSpec((tail, W), lambda i: (0, 0))],
        out_shape=[jax.ShapeDtypeStruct((B, N_REC_HEADS, REC_DIM, REC_DIM), F32),
                   jax.ShapeDtypeStruct((tail, W), BF16)],
        compiler_params=_cparams(("arbitrary",), VMEM_LIMIT),
        name="hgrn_meta",
    )(qd, kd, ke, v, rest, norm_g)

    gb = SCAN_BATCHES
    ns = S // SCAN_SEG

    def seg_specs(col):
        return [pl.BlockSpec((SCAN_SEG, W), lambda g, s, i=i: ((g * gb + i) * ns + s, col))
                for i in range(gb)]

    rec = pl.pallas_call(
        functools.partial(_hgrn_scan_kernel, gb=gb),
        grid=(B // gb, ns),
        in_specs=seg_specs(0) * 5 + seg_specs(3)
        + [pl.BlockSpec((1, REC_DIM), lambda g, s: (0, 0)),
           pl.BlockSpec((gb, N_REC_HEADS, REC_DIM, REC_DIM), lambda g, s: (g, 0, 0, 0))],
        out_specs=pl.BlockSpec((gb, SCAN_SEG, W), lambda g, s: (g, s, 0)),
        out_shape=jax.ShapeDtypeStruct((B, S, W), BF16),
        scratch_shapes=[pltpu.VMEM((gb, N_REC_HEADS, REC_DIM, REC_DIM), F32),
                        pltpu.VMEM((gb, SCAN_SEG, W), F32)],
        compiler_params=_cparams(("parallel", "arbitrary"), VMEM_LIMIT),
        name="hgrn_scan",
    )(*([qd] * gb + [kd] * gb + [ke] * gb + [v] * gb + [eb] * gb + [rest] * gb), norm_g, state)
    return rec.reshape(NS, W), rec_tail


def _mix_kernel(attn_ref, attn_t_ref, rec_ref, rec_t_ref, ga_ref, gr_ref, x_ref, tail_ref,
                wa_ref, wr_ref, wo_ref, g_ref, wrt_ref, brt_ref, h2_ref, xn_ref, lg_ref, *, n_seq_tiles):
    is_seq = pl.program_id(0) < n_seq_tiles
    attn = jnp.where(is_seq, attn_ref[...], attn_t_ref[...])
    rec = jnp.where(is_seq, rec_ref[...], rec_t_ref[...])
    h = jnp.where(is_seq, x_ref[...], tail_ref[...])
    a = jnp.dot(attn, wa_ref[...], preferred_element_type=F32)
    r = jnp.dot(rec, wr_ref[...], preferred_element_type=F32)
    mixed = jax.nn.sigmoid(ga_ref[...]) * a + jax.nn.sigmoid(gr_ref[...]) * r
    h2 = h + jnp.dot(mixed.astype(BF16), wo_ref[...], preferred_element_type=F32)
    h2_ref[...] = h2
    xn = _rms(h2, g_ref[...])
    xn_ref[...] = xn
    xn_hi = xn.astype(BF16)
    xn_lo = (xn - xn_hi.astype(F32)).astype(BF16)
    w2 = wrt_ref[...]
    p_hi = jnp.dot(xn_hi, w2, preferred_element_type=F32)
    p_lo = jnp.dot(xn_lo, w2[:, :LANES], preferred_element_type=F32)
    lg_ref[...] = p_hi[:, :LANES] + p_hi[:, LANES:] + p_lo + brt_ref[...]


def _mix(attn, attn_tail, rec, rec_tail, rest, x2d, tail_rows, wa, wr, wo, g, w_router, b_router):
    NS, D = x2d.shape
    R = NS + tail_rows.shape[0]
    tm = MIX_TILE
    nst = NS // tm
    const = lambda shape: pl.BlockSpec(shape, lambda i: (0, 0), pipeline_mode=pl.Buffered(1))
    row = lambda w: pl.BlockSpec((tm, w), lambda i: (i, 0))
    seq = lambda w: pl.BlockSpec((tm, w), lambda i: (jnp.minimum(i, nst - 1), 0))
    tl = lambda w: pl.BlockSpec((tm, w), lambda i: (jnp.maximum(i - nst, 0), 0))
    return pl.pallas_call(
        functools.partial(_mix_kernel, n_seq_tiles=nst),
        grid=(R // tm,),
        in_specs=[seq(Q_W), tl(Q_W), seq(REC_W), tl(REC_W),
                  pl.BlockSpec((tm, D), lambda i: (i, 2)), pl.BlockSpec((tm, D), lambda i: (i, 3)),
                  seq(D), tl(D), const(wa.shape), const(wr.shape), const(wo.shape), const(g.shape),
                  const(w_router.shape), const(b_router.shape)],
        out_specs=[row(D), row(D), row(LANES)],
        out_shape=[jax.ShapeDtypeStruct((R, D), F32), jax.ShapeDtypeStruct((R, D), F32),
                   jax.ShapeDtypeStruct((R, LANES), F32)],
        compiler_params=_cparams(("parallel",), VMEM_LIMIT),
        name="mix_outproj",
    )(attn, attn_tail, rec, rec_tail, rest, rest, x2d, tail_rows, wa, wr, wo, g, w_router, b_router)


def _route_kernel(lg_ref, dest_ref, gate_ref, cnt_ref, cnt_scr, carry_scr, start_scr, *, n_tok):
    ph = pl.program_id(0)
    i = pl.program_id(1)
    tm = lg_ref.shape[0]

    @pl.when((ph == 0) & (i == 0))
    def _():
        cnt_scr[...] = jnp.zeros_like(cnt_scr)

    lane = lax.broadcasted_iota(I32, (tm, LANES), 1)
    valid = (i * tm + lax.broadcasted_iota(I32, (tm, LANES), 0)) < n_tok
    work = lg_ref[...]
    onehots, vals = [], []
    for _ in range(TOP_K):
        m = jnp.max(work, axis=-1, keepdims=True)
        idx = jnp.min(jnp.where(work == m, lane, LANES), axis=-1, keepdims=True)
        oh = lane == idx
        onehots.append(oh)
        vals.append(m)
        work = jnp.where(oh, -jnp.inf, work)
    multi = jnp.zeros((tm, LANES), F32)
    for oh in onehots:
        multi = multi + jnp.where(oh & valid, 1.0, 0.0)
    tile_cnt = jnp.sum(multi, axis=0, keepdims=True)

    @pl.when(ph == 0)
    def _():
        cnt_scr[...] += tile_cnt

    @pl.when(ph == 1)
    def _():
        @pl.when(i == 0)
        def _():
            c = cnt_scr[...]
            padded = jnp.ceil(c * (1.0 / MOE_BLK)) * MOE_BLK
            before = (lax.broadcasted_iota(I32, (LANES, LANES), 0)
                      < lax.broadcasted_iota(I32, (LANES, LANES), 1))
            start = jnp.dot(jnp.broadcast_to(padded, (8, LANES)), jnp.where(before, 1.0, 0.0),
                            preferred_element_type=F32, precision=lax.Precision.HIGHEST)
            start_scr[...] = start[0:1]
            carry_scr[...] = jnp.zeros_like(carry_scr)
            cnt_ref[...] = c

        earlier = (lax.broadcasted_iota(I32, (tm, tm), 1) < lax.broadcasted_iota(I32, (tm, tm), 0))
        prefix = jnp.dot(jnp.where(earlier, 1.0, 0.0).astype(BF16), multi.astype(BF16),
                         preferred_element_type=F32)
        base = prefix + carry_scr[...] + start_scr[...]
        den = jnp.zeros_like(vals[0])
        for v in vals:
            den = den + jnp.exp(v - vals[0])
        dest = jnp.zeros((tm, LANES), F32)
        gate = jnp.zeros((tm, LANES), F32)
        for k in range(TOP_K):
            d_k = jnp.sum(jnp.where(onehots[k], base, 0.0), axis=-1, keepdims=True)
            dest = jnp.where(lane == k, d_k, dest)
            gate = jnp.where(lane == k, jnp.exp(vals[k] - vals[0]) / den, gate)
        dest_ref[...] = dest.astype(I32)
        gate_ref[...] = gate
        carry_scr[...] += tile_cnt


def _route(logits, n_tok):
    R = logits.shape[0]
    tm = ROW_TILE
    blk = pl.BlockSpec((tm, LANES), lambda p, i: (i * p, 0))
    return pl.pallas_call(
        functools.partial(_route_kernel, n_tok=n_tok),
        grid=(2, R // tm),
        in_specs=[pl.BlockSpec((tm, LANES), lambda p, i: (i, 0))],
        out_specs=[blk, blk, pl.BlockSpec((1, LANES), lambda p, i: (0, 0))],
        out_shape=[jax.ShapeDtypeStruct((R, LANES), I32), jax.ShapeDtypeStruct((R, LANES), F32),
                   jax.ShapeDtypeStruct((1, LANES), F32)],
        scratch_shapes=[pltpu.VMEM((1, LANES), F32)] * 3,
        compiler_params=_cparams(("arbitrary", "arbitrary"), VMEM_LIMIT),
        name="route",
    )(logits)


def _dispatch_kernel(pad_lo_ref, pad_len_ref, nu_ref, dest_ref, xn_ref, xs_ref, zero_scr, sem, *, n_blk):
    i = pl.program_id(0)
    last = pl.num_programs(0) - 1

    @pl.when(i < last)
    def _():
        def one(r):
            for k in range(TOP_K):
                pltpu.make_async_copy(xn_ref.at[pl.ds(r, 1), :],
                                      xs_ref.at[pl.ds(dest_ref[r * TOP_K + k], 1), :],
                                      sem).start(priority=k % 2)

        _for_rows(TOK_TILE, one, True)
        for _ in range(TOP_K):
            pltpu.make_async_copy(xn_ref, xs_ref.at[pl.ds(0, TOK_TILE), :], sem).wait()

    @pl.when(i == last)
    def _():
        zero_scr[...] = jnp.zeros_like(zero_scr)

        def block_copy(b):
            rows = pl.ds(pl.multiple_of(b * MOE_BLK, MOE_BLK), MOE_BLK)
            return pltpu.make_async_copy(zero_scr, xs_ref.at[rows, :], sem)

        def row_copy(e, r):
            return pltpu.make_async_copy(zero_scr.at[pl.ds(0, 1), :],
                                         xs_ref.at[pl.ds(pad_lo_ref[e] + r, 1), :], sem)

        def sweep(act):
            def blocks(b, carry):
                act(block_copy(b))
                return carry

            lax.fori_loop(nu_ref[0], n_blk, blocks, 0)

            def experts(e, carry):
                def rows(r, c):
                    act(row_copy(e, r))
                    return c

                return lax.fori_loop(0, pad_len_ref[e], rows, carry)

            lax.fori_loop(0, pad_lo_ref.shape[0], experts, 0)

        sweep(lambda cp: cp.start())
        sweep(lambda cp: cp.wait())


def _dispatch(xn, dest_flat, cnt, n_tok, n_blk):
    D = xn.shape[1]
    n_tiles = n_tok // TOK_TILE
    padded = ((cnt + MOE_BLK - 1) // MOE_BLK) * MOE_BLK
    pad_end = jnp.cumsum(padded)
    pad_lo = (pad_end - padded + cnt).astype(I32)
    pad_len = (padded - cnt).astype(I32)
    n_used = (pad_end[-1:] // MOE_BLK).astype(I32)
    tile = lambda i, *_: (jnp.minimum(i, n_tiles - 1),)
    return pl.pallas_call(
        functools.partial(_dispatch_kernel, n_blk=n_blk),
        grid_spec=pltpu.PrefetchScalarGridSpec(
            num_scalar_prefetch=3,
            grid=(n_tiles + 1,),
            in_specs=[pl.BlockSpec((TOK_TILE * TOP_K,), tile, memory_space=pltpu.SMEM),
                      pl.BlockSpec((TOK_TILE, D), lambda i, *_: (jnp.minimum(i, n_tiles - 1), 0))],
            out_specs=pl.BlockSpec(memory_space=pl.ANY),
            scratch_shapes=[pltpu.VMEM((MOE_BLK, D), F32), pltpu.SemaphoreType.DMA],
        ),
        out_shape=jax.ShapeDtypeStruct((n_blk * MOE_BLK, D), F32),
        compiler_params=_cparams(("arbitrary",), VMEM_LIMIT),
        name="dispatch",
    )(pad_lo, pad_len, n_used, dest_flat, xn)


class _WeightStream:
    def __init__(self, w_refs, stage, wbuf, sems, unit_rows):
        self.w_refs, self.stage, self.wbuf, self.sems = w_refs, stage, wbuf, sems
        self.unit_rows = unit_rows
        self.n_units = w_refs[0].shape[1] // unit_rows
        self.n_stage = stage.shape[0]

    def _copy(self, e, u, m):
        rows = pl.ds(pl.multiple_of(u * self.unit_rows, self.unit_rows), self.unit_rows)
        st = u % self.n_stage
        return pltpu.make_async_copy(self.w_refs[m].at[e, rows, :], self.stage.at[st, m],
                                     self.sems.at[st, m])

    def start(self, e, u):
        for m in range(len(self.w_refs)):
            self._copy(e, u, m).start(priority=1)

    def finish(self, e, u, slot):
        rows = pl.ds(pl.multiple_of(u * self.unit_rows, self.unit_rows), self.unit_rows)
        for m in range(len(self.w_refs)):
            self._copy(e, u, m).wait()
            self.wbuf[slot, m, rows, :] = self.stage[u % self.n_stage, m].astype(BF16)

    def prime(self, e):
        for u in range(self.n_stage):
            self.start(e, u)

    def convert(self, e, slot, lo, hi):
        def body(u, carry):
            self.finish(e, u, slot)

            @pl.when(u + self.n_stage < self.n_units)
            def _():
                self.start(e, u + self.n_stage)

            return carry

        lax.fori_loop(lo, hi, body, 0)


def _moe_step(plan, stream, compute, out_ref):
    be_ref, nxt_ref, slot_ref, first_ref, ulo_ref, uhi_ref, nu_ref = plan
    blk = pl.program_id(0)

    @pl.when(blk >= nu_ref[0])
    def _():
        out_ref[...] = jnp.zeros_like(out_ref)

    @pl.when(blk < nu_ref[0])
    def _():
        e = be_ref[blk]
        ne = nxt_ref[blk]
        slot = slot_ref[blk]

        @pl.when(blk == 0)
        def _():
            stream.prime(e)
            stream.convert(e, slot, 0, stream.n_units)

        @pl.when((first_ref[blk] == 1) & (ne >= 0))
        def _():
            stream.prime(ne)

        compute(slot, e)

        @pl.when(ne >= 0)
        def _():
            stream.convert(ne, 1 - slot, ulo_ref[blk], uhi_ref[blk])


def _row_variants(n_valid, fn):
    half = MOE_BLK // 2

    @pl.when(n_valid > half)
    def _():
        fn(MOE_BLK)

    @pl.when(n_valid <= half)
    def _():
        fn(half)


def _moe_up_kernel(be_ref, nxt_ref, slot_ref, first_ref, ulo_ref, uhi_ref, nu_ref, nval_ref,
                   xs_ref, wg_ref, wu_ref, bg_ref, bu_ref, act_ref, wbuf, stage, sems):
    stream = _WeightStream((wg_ref, wu_ref), stage, wbuf, sems, MOE_UNIT)
    half = act_ref.shape[1] // 2

    def compute(slot, e):
        def rows_fn(rows):
            xb = xs_ref[0:rows, :].astype(BF16)
            for c in range(2):
                cols = slice(c * half, (c + 1) * half)
                gate = jnp.dot(xb, wbuf[slot, 0, :, cols], preferred_element_type=F32) + bg_ref[e][:, cols]
                up = jnp.dot(xb, wbuf[slot, 1, :, cols], preferred_element_type=F32) + bu_ref[e][:, cols]
                gate = jnp.minimum(gate, SWIGLU_LIMIT)
                up = jnp.clip(up, -SWIGLU_LIMIT, SWIGLU_LIMIT)
                act = gate * jax.nn.sigmoid(SWIGLU_ALPHA * gate) * (up + 1.0)
                act_ref[0:rows, cols] = act.astype(act_ref.dtype)
            if rows < MOE_BLK:
                act_ref[rows:, :] = jnp.zeros((MOE_BLK - rows, act_ref.shape[1]), act_ref.dtype)

        _row_variants(nval_ref[pl.program_id(0)], rows_fn)

    _moe_step((be_ref, nxt_ref, slot_ref, first_ref, ulo_ref, uhi_ref, nu_ref), stream, compute, act_ref)


def _moe_down_kernel(be_ref, nxt_ref, slot_ref, first_ref, ulo_ref, uhi_ref, nu_ref, nval_ref,
                     act_ref, wd_ref, bd_ref, y_ref, wbuf, stage, sems):
    stream = _WeightStream((wd_ref,), stage, wbuf, sems, MOE_UNIT)

    def compute(slot, e):
        def rows_fn(rows):
            y_ref[0:rows, :] = (jnp.dot(act_ref[0:rows, :], wbuf[slot, 0], preferred_element_type=F32)
                                + bd_ref[e])
            if rows < MOE_BLK:
                y_ref[rows:, :] = jnp.zeros((MOE_BLK - rows, y_ref.shape[1]), y_ref.dtype)

        _row_variants(nval_ref[pl.program_id(0)], rows_fn)

    _moe_step((be_ref, nxt_ref, slot_ref, first_ref, ulo_ref, uhi_ref, nu_ref), stream, compute, y_ref)


def _moe_plan(cnt, n_blk, n_units):
    E = cnt.shape[0]
    nblk_e = (cnt + MOE_BLK - 1) // MOE_BLK
    blk_end = jnp.cumsum(nblk_e)
    blk_first = blk_end - nblk_e
    j = jnp.arange(n_blk, dtype=I32)
    be = jnp.minimum(jnp.sum((blk_end[None, :] <= j[:, None]).astype(I32), axis=1), E - 1)
    ids = jnp.arange(E, dtype=I32)
    own = be[:, None] == ids[None, :]

    def per_block(table):
        return jnp.sum(jnp.where(own, table[None, :], 0), axis=1)

    k = j - per_block(blk_first)
    nb = jnp.maximum(per_block(nblk_e), 1)
    nonempty = nblk_e > 0
    later = (ids[None, :] > ids[:, None]) & nonempty[None, :]
    nxt_e = jnp.min(jnp.where(later, ids[None, :], E), axis=1)
    nxt_e = jnp.where(nxt_e >= E, -1, nxt_e)
    slot_e = (jnp.cumsum(nonempty.astype(I32)) - 1) % 2
    n_valid = jnp.clip(per_block(cnt) - k * MOE_BLK, 0, MOE_BLK)
    plan = (be, per_block(nxt_e), per_block(slot_e), (k == 0).astype(I32),
            (k * n_units) // nb, ((k + 1) * n_units) // nb, blk_end[-1:].astype(I32), n_valid)
    return tuple(a.astype(I32) for a in plan)


def _moe(xs, cnt, w_gate, b_gate, w_up, b_up, w_down, b_down):
    P, D = xs.shape
    E, _, F = w_gate.shape
    n_blk = P // MOE_BLK
    n_stage = MOE_STAGES
    clamp = lambda blk, nu: jnp.maximum(jnp.minimum(blk, nu[0] - 1), 0)
    row_map = lambda blk, be, nx, sl, fi, lo, hi, nu, nv: (clamp(blk, nu), 0)
    out_map = lambda blk, *plan: (blk, 0)
    hbm = pl.BlockSpec(memory_space=pl.ANY)

    def bias_spec(width):
        return pl.BlockSpec((E, 1, width), lambda blk, *plan: (0, 0, 0), pipeline_mode=pl.Buffered(1))

    assert D == F, "one streaming plan serves all three expert matrices"
    plan = _moe_plan(cnt, n_blk, D // MOE_UNIT)
    act = pl.pallas_call(
        _moe_up_kernel,
        grid_spec=pltpu.PrefetchScalarGridSpec(
            num_scalar_prefetch=8,
            grid=(n_blk,),
            in_specs=[pl.BlockSpec((MOE_BLK, D), row_map), hbm, hbm, bias_spec(F), bias_spec(F)],
            out_specs=pl.BlockSpec((MOE_BLK, F), out_map),
            scratch_shapes=[pltpu.VMEM((2, 2, D, F), BF16),
                            pltpu.VMEM((n_stage, 2, MOE_UNIT, F), F32),
                            pltpu.SemaphoreType.DMA((n_stage, 2))],
        ),
        out_shape=jax.ShapeDtypeStruct((P, F), BF16),
        compiler_params=_cparams(("arbitrary",), VMEM_LIMIT),
        name="moe_up",
    )(*plan, xs, w_gate, w_up, b_gate.reshape(E, 1, F), b_up.reshape(E, 1, F))

    return pl.pallas_call(
        _moe_down_kernel,
        grid_spec=pltpu.PrefetchScalarGridSpec(
            num_scalar_prefetch=8,
            grid=(n_blk,),
            in_specs=[pl.BlockSpec((MOE_BLK, F), row_map), hbm, bias_spec(D)],
            out_specs=pl.BlockSpec((MOE_BLK, D), out_map),
            scratch_shapes=[pltpu.VMEM((2, 1, F, D), BF16),
                            pltpu.VMEM((n_stage, 1, MOE_UNIT, D), F32),
                            pltpu.SemaphoreType.DMA((n_stage, 1))],
        ),
        out_shape=jax.ShapeDtypeStruct((P, D), F32),
        compiler_params=_cparams(("arbitrary",), VMEM_LIMIT),
        name="moe_down",
    )(*plan, act, w_down, b_down.reshape(E, 1, D))


def _combine_kernel(dest_ref, dnext_ref, h2_ref, gate_ref, g_ref, y_ref, o_ref, ybuf, sem):
    i = pl.program_id(0)
    slot = i % 2

    def issue(dref, s, unrolled):
        def one(r):
            for k in range(TOP_K):
                pltpu.make_async_copy(y_ref.at[pl.ds(dref[r * TOP_K + k], 1), :],
                                      ybuf.at[s, k, pl.ds(r, 1), :],
                                      sem.at[s]).start(priority=k % 2)

        _for_rows(TOK_TILE, one, unrolled)

    @pl.when(i == 0)
    def _():
        issue(dest_ref, 0, False)

    for s in range(2):
        @pl.when((slot == s) & (i + 1 < pl.num_programs(0)))
        def _():
            issue(dnext_ref, 1 - s, True)

    for k in range(TOP_K):
        pltpu.make_async_copy(y_ref.at[pl.ds(0, TOK_TILE), :], ybuf.at[slot, k], sem.at[slot]).wait()
    gates = gate_ref[...]
    acc = h2_ref[...]
    for k in range(TOP_K):
        acc = acc + gates[:, k:k + 1] * ybuf[slot, k]
    o_ref[...] = _rms(acc, g_ref[...])


def _combine(h2, gates, dest_flat, y, g, n_seq):
    D = h2.shape[1]
    n_tiles = n_seq // TOK_TILE
    return pl.pallas_call(
        _combine_kernel,
        grid=(n_tiles,),
        in_specs=[pl.BlockSpec((TOK_TILE * TOP_K,), lambda i: (i,), memory_space=pltpu.SMEM),
                  pl.BlockSpec((TOK_TILE * TOP_K,), lambda i: (jnp.minimum(i + 1, n_tiles - 1),),
                               memory_space=pltpu.SMEM),
                  pl.BlockSpec((TOK_TILE, D), lambda i: (i, 0)),
                  pl.BlockSpec((TOK_TILE, LANES), lambda i: (i, 0)),
                  pl.BlockSpec((1, D), lambda i: (0, 0)),
                  pl.BlockSpec(memory_space=pl.ANY)],
        out_specs=pl.BlockSpec((TOK_TILE, D), lambda i: (i, 0)),
        out_shape=jax.ShapeDtypeStruct((n_seq, D), F32),
        scratch_shapes=[pltpu.VMEM((2, TOP_K, TOK_TILE, D), F32), pltpu.SemaphoreType.DMA((2,))],
        compiler_params=_cparams(("arbitrary",), VMEM_LIMIT),
        name="combine_norm",
    )(dest_flat, dest_flat, h2, gates, g, y)


def _rope_tables(S, n_meta_rows):
    half = HEAD_DIM // 2
    inv_freq = ROPE_THETA ** (-jnp.arange(half, dtype=F32) / half)
    lane = np.arange(LANES)
    fidx = lane % half
    sign = np.where((lane % HEAD_DIM) < half, -1.0, 1.0).astype(np.float32)
    pos_seq = N_META_TOK + jnp.arange(S)
    tail = jnp.arange(ROW_TILE)
    pos_tail = jnp.where(tail < n_meta_rows, tail % N_META_TOK, 0)
    pos = jnp.concatenate([pos_seq, pos_tail]).astype(F32)
    ang = (pos[:, None] * inv_freq[None, :])[:, fidx]
    return jnp.cos(ang), jnp.sin(ang) * sign[None, :]


def kernel(x, meta_tokens, attn_norm_g, w_in, attn_sinks, lower_bound_logits, rec_norm_g,
           w_attn_proj, w_rec_proj, w_out, ffn_norm_g, w_router, b_router, w_gate, b_gate,
           w_up, b_up, w_down, b_down, final_norm_g):
    B, S, D = x.shape
    assert w_in.shape[0] == 1, "single-layer block"
    assert S % ROW_TILE == 0 and (B * S) % ROW_TILE == 0 and S % SEG == 0
    assert S % SCAN_SEG == 0 and B % SCAN_BATCHES == 0
    NS = B * S
    NM = B * N_META_TOK
    NT = NS + NM
    R = -(-NT // ROW_TILE) * ROW_TILE
    assert R - NS == ROW_TILE and NT % TOK_TILE == 0 and NS % TOK_TILE == 0

    x2d = x.reshape(NS, D)
    tail_rows = jnp.concatenate([jnp.tile(meta_tokens.astype(x.dtype), (B, 1)),
                                 jnp.zeros((R - NT, D), x.dtype)], axis=0)

    w0 = w_in[0]
    kw = N_KV_HEADS * HEAD_DIM
    wq, wk, wv = w0[:, :Q_W], w0[:, Q_W:Q_W + kw], w0[:, Q_W + kw:Q_W + 2 * kw]
    dup = lambda w: jnp.concatenate([w[:, :HEAD_DIM]] * 2 + [w[:, HEAD_DIM:]] * 2, axis=1)
    w_qkv = jnp.concatenate([wq, dup(wk), dup(wv)], axis=1).astype(BF16)
    w_rest = w0[:, Q_W + 2 * kw:].astype(BF16)

    cos_t, sin_t = _rope_tables(S, NM)
    u, qkv = _norm_qkv(x2d, tail_rows, attn_norm_g[0].reshape(1, D), w_qkv, cos_t, sin_t,
                       S // ROW_TILE)
    rest = _proj_rest(u, w_rest)

    attn, attn_tail = _attention(qkv, attn_sinks[0], B, S)
    rec, rec_tail = _hgrn(rest, lower_bound_logits, rec_norm_g[0].reshape(1, REC_DIM), B, S)

    wr_hi = w_router[0].astype(BF16)
    wr_lo = (w_router[0] - wr_hi.astype(F32)).astype(BF16)
    lane_pad = ((0, 0), (0, LANES - N_EXPERTS))
    wr_pad = jnp.concatenate([jnp.pad(wr_hi, lane_pad), jnp.pad(wr_lo, lane_pad)], axis=1)
    br_pad = jnp.pad(b_router[0].reshape(1, N_EXPERTS), ((0, 0), (0, LANES - N_EXPERTS)),
                     constant_values=NEG_INF)
    h2, xn, logits = _mix(attn, attn_tail, rec, rec_tail, rest, x2d, tail_rows,
                          w_attn_proj[0].astype(BF16), w_rec_proj[0].astype(BF16),
                          w_out[0].astype(BF16), ffn_norm_g[0].reshape(1, D), wr_pad, br_pad)

    dest, gates, counts = _route(logits, NT)
    cnt = counts[0, :N_EXPERTS].astype(I32)
    n_blk = -(-(NT * TOP_K + N_EXPERTS * (MOE_BLK - 1)) // MOE_BLK)
    dest_flat = dest[:NT, :TOP_K].reshape(NT * TOP_K)

    xs = _dispatch(xn, dest_flat, cnt, NT, n_blk)
    y = _moe(xs, cnt, w_gate[0], b_gate[0], w_up[0], b_up[0], w_down[0], b_down[0])
    out = _combine(h2, gates, dest_flat, y, final_norm_g.reshape(1, D), NS)
    return out.reshape(B, S, D)
```

```python
import functools

import jax
import jax.numpy as jnp
import numpy as np
from jax import lax
from jax.experimental import pallas as pl
from jax.experimental.pallas import tpu as pltpu

F32 = jnp.float32
BF16 = jnp.bfloat16
I32 = jnp.int32

N_META_TOK = 16
HEAD_DIM = 64
N_Q_HEADS = 16
N_KV_HEADS = 2
ATTN_BLK = 128
ROPE_THETA = 10000.0
N_REC_HEADS = 8
REC_DIM = 128
REC_CHUNK = 16
N_EXPERTS = 32
TOP_K = 4
SWIGLU_ALPHA = 1.702
SWIGLU_LIMIT = 7.0
NORM_EPS = 1e-5
NEG_INF = -1e30

LANES = 128
ROW_TILE = 512
MIX_TILE = 256
SEG = 256
SCAN_SEG = 256
SCAN_BATCHES = 2
MOE_BLK = 256
MOE_UNIT = 128
MOE_STAGES = 2
TOK_TILE = 128
VMEM_LIMIT = 56 * 1024 * 1024

Q_W = N_Q_HEADS * HEAD_DIM
REC_W = N_REC_HEADS * REC_DIM


def _cparams(sem, vmem=None, **kw):
    return pltpu.CompilerParams(dimension_semantics=sem, vmem_limit_bytes=vmem, **kw)


def _rms(xf, g):
    return xf * lax.rsqrt(jnp.mean(xf * xf, axis=-1, keepdims=True) + NORM_EPS) * g


def _for_rows(n, fn, unrolled):
    if unrolled:
        for r in range(n):
            fn(r)
    else:
        def body(r, carry):
            fn(r)
            return carry

        lax.fori_loop(0, n, body, 0)


def _norm_qkv_kernel(x_ref, tail_ref, g_ref, w_ref, cos_ref, sin_ref, u_ref, qkv_ref, *, n_seq_tiles):
    h = jnp.where(pl.program_id(0) < n_seq_tiles, x_ref[...], tail_ref[...])
    u = _rms(h, g_ref[...]).astype(BF16)
    u_ref[...] = u
    p = jnp.dot(u, w_ref[...], preferred_element_type=F32)
    cos = cos_ref[...]
    sin = sin_ref[...]
    lane = lax.broadcasted_iota(I32, cos.shape, 1)
    first_half = (lane % HEAD_DIM) < (HEAD_DIM // 2)
    n_q = Q_W // LANES
    for c in range(n_q + 2):
        xs = p[:, c * LANES:(c + 1) * LANES]
        swapped = jnp.where(first_half, pltpu.roll(xs, LANES - HEAD_DIM // 2, 1),
                            pltpu.roll(xs, HEAD_DIM // 2, 1))
        r = xs * cos + swapped * sin
        if c < n_q:
            r = r * (HEAD_DIM ** -0.5)
        qkv_ref[:, c * LANES:(c + 1) * LANES] = r.astype(BF16)
    v0 = (n_q + 2) * LANES
    qkv_ref[:, v0:] = p[:, v0:].astype(BF16)


def _norm_qkv(x2d, tail_rows, g, w_qkv, cos_t, sin_t, tiles_per_seq):
    NS, D = x2d.shape
    R = NS + tail_rows.shape[0]
    W = w_qkv.shape[1]
    n_seq_tiles = NS // ROW_TILE

    def tab_map(i):
        return (jnp.where(i < n_seq_tiles, i % tiles_per_seq, tiles_per_seq), 0)

    return pl.pallas_call(
        functools.partial(_norm_qkv_kernel, n_seq_tiles=n_seq_tiles),
        grid=(R // ROW_TILE,),
        in_specs=[
            pl.BlockSpec((ROW_TILE, D), lambda i: (jnp.minimum(i, n_seq_tiles - 1), 0)),
            pl.BlockSpec((ROW_TILE, D), lambda i: (0, 0)),
            pl.BlockSpec((1, D), lambda i: (0, 0)),
            pl.BlockSpec((D, W), lambda i: (0, 0)),
            pl.BlockSpec((ROW_TILE, LANES), tab_map),
            pl.BlockSpec((ROW_TILE, LANES), tab_map),
        ],
        out_specs=[
            pl.BlockSpec((ROW_TILE, D), lambda i: (i, 0)),
            pl.BlockSpec((ROW_TILE, W), lambda i: (i, 0)),
        ],
        out_shape=[jax.ShapeDtypeStruct((R, D), BF16), jax.ShapeDtypeStruct((R, W), BF16)],
        compiler_params=_cparams(("parallel",), VMEM_LIMIT),
        name="norm_qkv",
    )(x2d, tail_rows, g, w_qkv, cos_t, sin_t)


def _matmul_kernel(x_ref, w_ref, o_ref):
    o_ref[...] = jnp.dot(x_ref[...], w_ref[...], preferred_element_type=F32).astype(o_ref.dtype)


def _proj_rest(u, w):
    R, D = u.shape
    N = w.shape[1]
    tn = 2048
    tm = ROW_TILE + ROW_TILE // 2
    if R % tm:
        tm = ROW_TILE
    return pl.pallas_call(
        _matmul_kernel,
        grid=(N // tn, R // tm),
        in_specs=[pl.BlockSpec((tm, D), lambda j, i: (i, 0)),
                  pl.BlockSpec((D, tn), lambda j, i: (0, j))],
        out_specs=pl.BlockSpec((tm, tn), lambda j, i: (i, j)),
        out_shape=jax.ShapeDtypeStruct((R, N), F32),
        compiler_params=_cparams(("parallel", "parallel"), VMEM_LIMIT),
        name="proj_rest",
    )(u, w)


def _attn_core(q_ref, k_groups, v_groups, mask_fn, sink_ref, write):
    nkeys = k_groups[0].shape[0]
    rows = q_ref.shape[0]
    heads_per_group = N_Q_HEADS // N_KV_HEADS
    width = heads_per_group * rows
    lo_q = lax.broadcasted_iota(I32, (rows, LANES), 1) < HEAD_DIM
    ki = lax.broadcasted_iota(I32, (nkeys, width), 0)
    qi = lax.broadcasted_iota(I32, (nkeys, width), 1) % rows
    mask = mask_fn(ki, qi)
    top_half = lax.broadcasted_iota(I32, (LANES, rows), 0) < HEAD_DIM
    for g in range(N_KV_HEADS):
        parts = []
        for j in range(heads_per_group // 2):
            qp = q_ref[:, (heads_per_group // 2 * g + j) * LANES:(heads_per_group // 2 * g + j + 1) * LANES]
            zero_q = jnp.zeros_like(qp)
            parts += [jnp.where(lo_q, qp, zero_q), jnp.where(lo_q, zero_q, qp)]
        q_stack = jnp.concatenate(parts, axis=0)
        s = lax.dot_general(k_groups[g], q_stack, (((1,), (1,)), ((), ())),
                            preferred_element_type=F32)
        s = jnp.where(mask, s, NEG_INF)
        sk = sink_ref[:, g * width:(g + 1) * width]
        m = jnp.maximum(jnp.max(s, axis=0, keepdims=True), sk)
        e = jnp.exp(s - m)
        den = jnp.sum(e, axis=0, keepdims=True) + jnp.exp(sk - m)
        prob = (e * (1.0 / den)).astype(BF16)
        o_t = lax.dot_general(v_groups[g], prob, (((0,), (0,)), ((), ())), preferred_element_type=F32)
        for j in range(heads_per_group // 2):
            even = o_t[:, (2 * j) * rows:(2 * j + 1) * rows]
            odd = o_t[:, (2 * j + 1) * rows:(2 * j + 2) * rows]
            pair_t = jnp.where(top_half, even, odd)
            write(heads_per_group // 2 * g + j, pair_t.T)


def _attn_seq_kernel(sink_ref, q_ref, kc0, kc1, vc0, vc1, kp0, kp1, vp0, vp1,
                     km0, km1, vm0, vm1, o_ref):
    n = pl.program_id(1)
    metas = ((km0, vm0), (km1, vm1))
    curs = ((kc0, vc0), (kc1, vc1))
    prevs = ((kp0, vp0), (kp1, vp1))
    first, second = slice(0, ATTN_BLK), slice(ATTN_BLK, 2 * ATTN_BLK)
    for sub in range(2):
        no_prev = jnp.where(n > 0, 0, 2 * ATTN_BLK) if sub == 0 else 0

        def mask_fn(ki, qi, no_prev=no_prev):
            prev_ok = ((ki >= N_META_TOK) & (ki < N_META_TOK + ATTN_BLK)
                       & (ki - N_META_TOK > qi + no_prev))
            cur_ok = (ki >= N_META_TOK + ATTN_BLK) & (ki - (N_META_TOK + ATTN_BLK) <= qi)
            return (ki < N_META_TOK) | prev_ok | cur_ok

        k_groups, v_groups = [], []
        for g in range(N_KV_HEADS):
            before = ((prevs[g][0][...], prevs[g][1][...]) if sub == 0
                      else (curs[g][0][first, :], curs[g][1][first, :]))
            own = first if sub == 0 else second
            k_groups.append(jnp.concatenate([metas[g][0][...], before[0], curs[g][0][own, :]], axis=0))
            v_groups.append(jnp.concatenate([metas[g][1][...], before[1], curs[g][1][own, :]], axis=0))

        def write(pr, acc, sub=sub):
            o_ref[sub * ATTN_BLK:(sub + 1) * ATTN_BLK, pr * LANES:(pr + 1) * LANES] = acc.astype(o_ref.dtype)

        _attn_core(q_ref.at[pl.ds(sub * ATTN_BLK, ATTN_BLK), :], k_groups, v_groups, mask_fn,
                   sink_ref, write)


def _attn_meta_kernel(sink_ref, q_ref, k0, k1, v0, v1, o_ref):
    nm = q_ref.shape[0]

    def mask_fn(ki, qi):
        return ((qi // N_META_TOK) == (ki // N_META_TOK)) & ((ki % N_META_TOK) <= (qi % N_META_TOK))

    o_ref[...] = jnp.zeros_like(o_ref)

    def write(pr, acc):
        o_ref[0:nm, pr * LANES:(pr + 1) * LANES] = acc.astype(o_ref.dtype)

    _attn_core(q_ref, [k0[...], k1[...]], [v0[...], v1[...]], mask_fn, sink_ref, write)


def _attention(qkv, sinks, B, S):
    R = qkv.shape[0]
    NS = B * S
    NM = B * N_META_TOK
    nb = S // ATTN_BLK
    qc = Q_W // LANES
    sink_seq = jnp.repeat(sinks.astype(F32), ATTN_BLK).reshape(1, N_Q_HEADS * ATTN_BLK)
    sink_meta = jnp.repeat(sinks.astype(F32), NM).reshape(1, N_Q_HEADS * NM)

    pair = 2 * ATTN_BLK
    npair = S // pair

    def kv_spec(col, prev):
        if prev:
            return pl.BlockSpec((ATTN_BLK, LANES), lambda b, n: (b * nb + jnp.maximum(2 * n - 1, 0), col))
        return pl.BlockSpec((pair, LANES), lambda b, n: (b * npair + n, col))

    def meta_spec(col):
        return pl.BlockSpec((N_META_TOK, LANES), lambda b, n: (NS // N_META_TOK + b, col))

    in_specs = [pl.BlockSpec(sink_seq.shape, lambda b, n: (0, 0)),
                pl.BlockSpec((pair, Q_W), lambda b, n: (b * npair + n, 0))]
    in_specs += [kv_spec(qc, False), kv_spec(qc + 1, False), kv_spec(qc + 2, False), kv_spec(qc + 3, False)]
    in_specs += [kv_spec(qc, True), kv_spec(qc + 1, True), kv_spec(qc + 2, True), kv_spec(qc + 3, True)]
    in_specs += [meta_spec(qc), meta_spec(qc + 1), meta_spec(qc + 2), meta_spec(qc + 3)]
    attn = pl.pallas_call(
        _attn_seq_kernel,
        grid=(B, npair),
        in_specs=in_specs,
        out_specs=pl.BlockSpec((pair, Q_W), lambda b, n: (b * npair + n, 0)),
        out_shape=jax.ShapeDtypeStruct((NS, Q_W), BF16),
        compiler_params=_cparams(("parallel", "parallel"), VMEM_LIMIT),
        name="attn_seq",
    )(sink_seq, *([qkv] * 13))

    mb = NS // NM
    tail = R - NS
    blk = lambda col: pl.BlockSpec((NM, LANES), lambda i: (mb, col))
    attn_tail = pl.pallas_call(
        _attn_meta_kernel,
        grid=(1,),
        in_specs=[pl.BlockSpec(sink_meta.shape, lambda i: (0, 0)),
                  pl.BlockSpec((NM, Q_W), lambda i: (mb, 0)),
                  blk(qc), blk(qc + 1), blk(qc + 2), blk(qc + 3)],
        out_specs=pl.BlockSpec((tail, Q_W), lambda i: (0, 0)),
        out_shape=jax.ShapeDtypeStruct((tail, Q_W), BF16),
        compiler_params=_cparams(("arbitrary",), VMEM_LIMIT),
        name="attn_meta",
    )(sink_meta, qkv, qkv, qkv, qkv, qkv)
    return attn, attn_tail


def _hgrn_prep_kernel(rq_ref, rf_ref, ri_ref, lbl_ref, qd_ref, kd_ref, ke_ref, v_ref, eb_ref):
    lbl = lbl_ref[...]
    e = jnp.exp(lbl - jnp.max(lbl, axis=0, keepdims=True))
    lb = e[0:1] / jnp.sum(e, axis=0, keepdims=True)
    f = lb + (1.0 - lb) * jax.nn.sigmoid(rf_ref[...])
    logf = jnp.log(f)
    k = 1.0 - f
    tm = logf.shape[0]
    ri = lax.broadcasted_iota(I32, (tm, tm), 0)
    ci = lax.broadcasted_iota(I32, (tm, tm), 1)
    same_chunk = (ri // REC_CHUNK) == (ci // REC_CHUNK)
    sel = jnp.concatenate([jnp.where(same_chunk & (ci <= ri), 1.0, 0.0).astype(BF16),
                           jnp.where(same_chunk, 1.0, 0.0).astype(BF16)], axis=0)
    hi = logf.astype(BF16)
    rem = logf - hi.astype(F32)
    mid = rem.astype(BF16)
    lo = (rem - mid.astype(F32)).astype(BF16)
    sums = (jnp.dot(sel, hi, preferred_element_type=F32) + jnp.dot(sel, mid, preferred_element_type=F32)
            + jnp.dot(sel, lo, preferred_element_type=F32))
    b = sums[:tm]
    total = sums[tm:]
    eb = jnp.exp(b)
    eb_ref[...] = eb
    qd_ref[...] = (rq_ref[...] * eb).astype(BF16)
    kd_ref[...] = (k * jnp.exp(-b)).astype(BF16)
    ke_ref[...] = (k * jnp.exp(total - b)).astype(BF16)
    v_ref[...] = ri_ref[...].astype(BF16)


def _hgrn_prep(rest, lbl):
    R = rest.shape[0]
    W = REC_W
    col = lambda c: pl.BlockSpec((SEG, W), lambda i: (i, c))
    row_out = pl.BlockSpec((SEG, W), lambda i: (i, 0))
    return pl.pallas_call(
        _hgrn_prep_kernel,
        grid=(R // SEG,),
        in_specs=[col(0), col(1), col(2), pl.BlockSpec(lbl.shape, lambda i: (0, 0))],
        out_specs=[row_out] * 5,
        out_shape=[jax.ShapeDtypeStruct((R, W), BF16)] * 4 + [jax.ShapeDtypeStruct((R, W), F32)],
        compiler_params=_cparams(("parallel",), VMEM_LIMIT),
        name="hgrn_prep",
    )(rest, rest, rest, lbl)


def _hgrn_intra(qd, kd, v):
    n = qd.shape[0]
    ri = lax.broadcasted_iota(I32, (n, n), 0)
    ci = lax.broadcasted_iota(I32, (n, n), 1)
    keep = ((ri // REC_CHUNK) == (ci // REC_CHUNK)) & (ri >= ci)
    sc = lax.dot_general(qd, kd, (((1,), (1,)), ((), ())), preferred_element_type=F32)
    sc = jnp.where(keep, sc, 0.0)
    return jnp.dot(sc.astype(BF16), v, preferred_element_type=F32)


def _hgrn_kv_t(v, ke):
    return lax.dot_general(v, ke, (((0,), (0,)), ((), ())), preferred_element_type=F32)


def _hgrn_finish(o, g, norm_g):
    y = o * lax.rsqrt(jnp.mean(o * o, axis=-1, keepdims=True) + NORM_EPS) * norm_g
    return (y * (g * jax.nn.sigmoid(g))).astype(BF16)


def _hgrn_meta_kernel(qd_ref, kd_ref, ke_ref, v_ref, rg_ref, ng_ref, st_ref, rec_ref, *, nbatch):
    rec_ref[...] = jnp.zeros_like(rec_ref)
    nm = nbatch * REC_CHUNK
    for h in range(N_REC_HEADS):
        cols = slice(h * REC_DIM, (h + 1) * REC_DIM)
        o = _hgrn_intra(qd_ref[0:nm, cols], kd_ref[0:nm, cols], v_ref[0:nm, cols])
        rec_ref[0:nm, cols] = _hgrn_finish(o, rg_ref[0:nm, cols], ng_ref[...])
        for b in range(nbatch):
            rows = slice(b * REC_CHUNK, (b + 1) * REC_CHUNK)
            st_ref[b, h] = _hgrn_kv_t(v_ref[rows, cols], ke_ref[rows, cols])


def _hgrn_scan_kernel(*refs, gb):
    qd, kd, ke, v, eb, rg = (refs[i * gb:(i + 1) * gb] for i in range(6))
    ng_ref, st0_ref, rec_ref, st_scr, o_scr = refs[6 * gb:]
    seg = rec_ref.shape[1]
    heads = [slice(h * REC_DIM, (h + 1) * REC_DIM) for h in range(N_REC_HEADS)]

    @pl.when(pl.program_id(1) == 0)
    def _():
        st_scr[...] = st0_ref[...]

    for i in range(gb):
        for cols in heads:
            o_scr[i, :, cols] = _hgrn_intra(qd[i][:, cols], kd[i][:, cols], v[i][:, cols])

    def chunk(c, carry):
        r0 = pl.multiple_of(c * REC_CHUNK, REC_CHUNK)
        rows = pl.ds(r0, REC_CHUNK)
        for i in range(gb):
            for h, cols in enumerate(heads):
                last8 = eb[i][pl.ds(pl.multiple_of(r0 + REC_CHUNK - 8, 8), 8), cols]
                dec = last8[7:8]
                st = st_scr[i, h]
                o_scr[i, rows, cols] += lax.dot_general(qd[i][rows, cols], st.astype(BF16),
                                                        (((1,), (1,)), ((), ())),
                                                        preferred_element_type=F32)
                st_scr[i, h] = st * dec + _hgrn_kv_t(v[i][rows, cols], ke[i][rows, cols])
        return carry

    lax.fori_loop(0, seg // REC_CHUNK, chunk, 0, unroll=8)
    for i in range(gb):
        for cols in heads:
            rec_ref[i, :, cols] = _hgrn_finish(o_scr[i, :, cols], rg[i][:, cols], ng_ref[...])


def _hgrn(rest, lbl, norm_g, B, S):
    R = rest.shape[0]
    NS = B * S
    W = REC_W
    tail = R - NS
    tb = NS // tail
    qd, kd, ke, v, eb = _hgrn_prep(rest, lbl)

    tail_spec = pl.BlockSpec((tail, W), lambda i: (tb, 0))
    state, rec_tail = pl.pallas_call(
        functools.partial(_hgrn_meta_kernel, nbatch=B),
        grid=(1,),
        in_specs=[tail_spec, tail_spec, tail_spec, tail_spec,
                  pl.BlockSpec((tail, W), lambda i: (tb, 3)),
                  pl.BlockSpec((1, REC_DIM), lambda i: (0, 0))],
        out_specs=[pl.BlockSpec((B, N_REC_HEADS, REC_DIM, REC_DIM), lambda i: (0, 0, 0, 0)),
                   pl.BlockSpec((tail, W), lambda i: (0, 0))],
        out_shape=[jax.ShapeDtypeStruct((B, N_REC_HEADS, REC_DIM, REC_DIM), F32),
                   jax.ShapeDtypeStruct((tail, W), BF16)],
        compiler_params=_cparams(("arbitrary",), VMEM_LIMIT),
        name="hgrn_meta",
    )(qd, kd, ke, v, rest, norm_g)

    gb = SCAN_BATCHES
    ns = S // SCAN_SEG

    def seg_specs(col):
        return [pl.BlockSpec((SCAN_SEG, W), lambda g, s, i=i: ((g * gb + i) * ns + s, col))
                for i in range(gb)]

    rec = pl.pallas_call(
        functools.partial(_hgrn_scan_kernel, gb=gb),
        grid=(B // gb, ns),
        in_specs=seg_specs(0) * 5 + seg_specs(3)
        + [pl.BlockSpec((1, REC_DIM), lambda g, s: (0, 0)),
           pl.BlockSpec((gb, N_REC_HEADS, REC_DIM, REC_DIM), lambda g, s: (g, 0, 0, 0))],
        out_specs=pl.BlockSpec((gb, SCAN_SEG, W), lambda g, s: (g, s, 0)),
        out_shape=jax.ShapeDtypeStruct((B, S, W), BF16),
        scratch_shapes=[pltpu.VMEM((gb, N_REC_HEADS, REC_DIM, REC_DIM), F32),
                        pltpu.VMEM((gb, SCAN_SEG, W), F32)],
        compiler_params=_cparams(("parallel", "arbitrary"), VMEM_LIMIT),
        name="hgrn_scan",
    )(*([qd] * gb + [kd] * gb + [ke] * gb + [v] * gb + [eb] * gb + [rest] * gb), norm_g, state)
    return rec.reshape(NS, W), rec_tail


def _mix_kernel(attn_ref, attn_t_ref, rec_ref, rec_t_ref, ga_ref, gr_ref, x_ref, tail_ref,
                wa_ref, wr_ref, wo_ref, g_ref, wrt_ref, brt_ref, h2_ref, xn_ref, lg_ref, *, n_seq_tiles):
    is_seq = pl.program_id(0) < n_seq_tiles
    attn = jnp.where(is_seq, attn_ref[...], attn_t_ref[...])
    rec = jnp.where(is_seq, rec_ref[...], rec_t_ref[...])
    h = jnp.where(is_seq, x_ref[...], tail_ref[...])
    a = jnp.dot(attn, wa_ref[...], preferred_element_type=F32)
    r = jnp.dot(rec, wr_ref[...], preferred_element_type=F32)
    mixed = jax.nn.sigmoid(ga_ref[...]) * a + jax.nn.sigmoid(gr_ref[...]) * r
    h2 = h + jnp.dot(mixed.astype(BF16), wo_ref[...], preferred_element_type=F32)
    h2_ref[...] = h2
    xn = _rms(h2, g_ref[...])
    xn_ref[...] = xn
    xn_hi = xn.astype(BF16)
    xn_lo = (xn - xn_hi.astype(F32)).astype(BF16)
    w2 = wrt_ref[...]
    p_hi = jnp.dot(xn_hi, w2, preferred_element_type=F32)
    p_lo = jnp.dot(xn_lo, w2[:, :LANES], preferred_element_type=F32)
    lg_ref[...] = p_hi[:, :LANES] + p_hi[:, LANES:] + p_lo + brt_ref[...]


def _mix(attn, attn_tail, rec, rec_tail, rest, x2d, tail_rows, wa, wr, wo, g, w_router, b_router):
    NS, D = x2d.shape
    R = NS + tail_rows.shape[0]
    tm = MIX_TILE
    nst = NS // tm
    const = lambda shape: pl.BlockSpec(shape, lambda i: (0, 0), pipeline_mode=pl.Buffered(1))
    row = lambda w: pl.BlockSpec((tm, w), lambda i: (i, 0))
    seq = lambda w: pl.BlockSpec((tm, w), lambda i: (jnp.minimum(i, nst - 1), 0))
    tl = lambda w: pl.BlockSpec((tm, w), lambda i: (jnp.maximum(i - nst, 0), 0))
    return pl.pallas_call(
        functools.partial(_mix_kernel, n_seq_tiles=nst),
        grid=(R // tm,),
        in_specs=[seq(Q_W), tl(Q_W), seq(REC_W), tl(REC_W),
                  pl.BlockSpec((tm, D), lambda i: (i, 2)), pl.BlockSpec((tm, D), lambda i: (i, 3)),
                  seq(D), tl(D), const(wa.shape), const(wr.shape), const(wo.shape), const(g.shape),
                  const(w_router.shape), const(b_router.shape)],
        out_specs=[row(D), row(D), row(LANES)],
        out_shape=[jax.ShapeDtypeStruct((R, D), F32), jax.ShapeDtypeStruct((R, D), F32),
                   jax.ShapeDtypeStruct((R, LANES), F32)],
        compiler_params=_cparams(("parallel",), VMEM_LIMIT),
        name="mix_outproj",
    )(attn, attn_tail, rec, rec_tail, rest, rest, x2d, tail_rows, wa, wr, wo, g, w_router, b_router)


def _route_kernel(lg_ref, dest_ref, gate_ref, cnt_ref, cnt_scr, carry_scr, start_scr, *, n_tok):
    ph = pl.program_id(0)
    i = pl.program_id(1)
    tm = lg_ref.shape[0]

    @pl.when((ph == 0) & (i == 0))
    def _():
        cnt_scr[...] = jnp.zeros_like(cnt_scr)

    lane = lax.broadcasted_iota(I32, (tm, LANES), 1)
    valid = (i * tm + lax.broadcasted_iota(I32, (tm, LANES), 0)) < n_tok
    work = lg_ref[...]
    onehots, vals = [], []
    for _ in range(TOP_K):
        m = jnp.max(work, axis=-1, keepdims=True)
        idx = jnp.min(jnp.where(work == m, lane, LANES), axis=-1, keepdims=True)
        oh = lane == idx
        onehots.append(oh)
        vals.append(m)
        work = jnp.where(oh, -jnp.inf, work)
    multi = jnp.zeros((tm, LANES), F32)
    for oh in onehots:
        multi = multi + jnp.where(oh & valid, 1.0, 0.0)
    tile_cnt = jnp.sum(multi, axis=0, keepdims=True)

    @pl.when(ph == 0)
    def _():
        cnt_scr[...] += tile_cnt

    @pl.when(ph == 1)
    def _():
        @pl.when(i == 0)
        def _():
            c = cnt_scr[...]
            padded = jnp.ceil(c * (1.0 / MOE_BLK)) * MOE_BLK
            before = (lax.broadcasted_iota(I32, (LANES, LANES), 0)
                      < lax.broadcasted_iota(I32, (LANES, LANES), 1))
            start = jnp.dot(jnp.broadcast_to(padded, (8, LANES)), jnp.where(before, 1.0, 0.0),
                            preferred_element_type=F32, precision=lax.Precision.HIGHEST)
            start_scr[...] = start[0:1]
            carry_scr[...] = jnp.zeros_like(carry_scr)
            cnt_ref[...] = c

        earlier = (lax.broadcasted_iota(I32, (tm, tm), 1) < lax.broadcasted_iota(I32, (tm, tm), 0))
        prefix = jnp.dot(jnp.where(earlier, 1.0, 0.0).astype(BF16), multi.astype(BF16),
                         preferred_element_type=F32)
        base = prefix + carry_scr[...] + start_scr[...]
        den = jnp.zeros_like(vals[0])
        for v in vals:
            den = den + jnp.exp(v - vals[0])
        dest = jnp.zeros((tm, LANES), F32)
        gate = jnp.zeros((tm, LANES), F32)
        for k in range(TOP_K):
            d_k = jnp.sum(jnp.where(onehots[k], base, 0.0), axis=-1, keepdims=True)
            dest = jnp.where(lane == k, d_k, dest)
            gate = jnp.where(lane == k, jnp.exp(vals[k] - vals[0]) / den, gate)
        dest_ref[...] = dest.astype(I32)
        gate_ref[...] = gate
        carry_scr[...] += tile_cnt


def _route(logits, n_tok):
    R = logits.shape[0]
    tm = ROW_TILE
    blk = pl.BlockSpec((tm, LANES), lambda p, i: (i * p, 0))
    return pl.pallas_call(
        functools.partial(_route_kernel, n_tok=n_tok),
        grid=(2, R // tm),
        in_specs=[pl.BlockSpec((tm, LANES), lambda p, i: (i, 0))],
        out_specs=[blk, blk, pl.BlockSpec((1, LANES), lambda p, i: (0, 0))],
        out_shape=[jax.ShapeDtypeStruct((R, LANES), I32), jax.ShapeDtypeStruct((R, LANES), F32),
                   jax.ShapeDtypeStruct((1, LANES), F32)],
        scratch_shapes=[pltpu.VMEM((1, LANES), F32)] * 3,
        compiler_params=_cparams(("arbitrary", "arbitrary"), VMEM_LIMIT),
        name="route",
    )(logits)


def _dispatch_kernel(pad_lo_ref, pad_len_ref, nu_ref, dest_ref, xn_ref, xs_ref, zero_scr, sem, *, n_blk):
    i = pl.program_id(0)
    last = pl.num_programs(0) - 1

    @pl.when(i < last)
    def _():
        def one(r):
            for k in range(TOP_K):
                pltpu.make_async_copy(xn_ref.at[pl.ds(r, 1), :],
                                      xs_ref.at[pl.ds(dest_ref[r * TOP_K + k], 1), :], sem).start()

        _for_rows(TOK_TILE, one, True)
        for _ in range(TOP_K):
            pltpu.make_async_copy(xn_ref, xs_ref.at[pl.ds(0, TOK_TILE), :], sem).wait()

    @pl.when(i == last)
    def _():
        zero_scr[...] = jnp.zeros_like(zero_scr)

        def block_copy(b):
            rows = pl.ds(pl.multiple_of(b * MOE_BLK, MOE_BLK), MOE_BLK)
            return pltpu.make_async_copy(zero_scr, xs_ref.at[rows, :], sem)

        def sweep(act):
            def blocks(b, carry):
                act(block_copy(b))
                return carry

            lax.fori_loop(nu_ref[0], n_blk, blocks, 0)

            def experts(e, carry):
                lo = pad_lo_ref[e]
                n_pad = pad_len_ref[e]
                head = jnp.minimum((-lo) & 7, n_pad)

                def rows(r, c):
                    act(pltpu.make_async_copy(zero_scr.at[pl.ds(0, 1), :],
                                              xs_ref.at[pl.ds(lo + r, 1), :], sem))
                    return c

                lax.fori_loop(0, head, rows, 0)
                off = lo + head
                left = n_pad - head
                size = MOE_BLK // 2
                while size >= 8:
                    @pl.when((left & size) != 0)
                    def _(off=off, size=size):
                        act(pltpu.make_async_copy(zero_scr.at[pl.ds(0, size), :],
                                                  xs_ref.at[pl.ds(pl.multiple_of(off, 8), size), :], sem))

                    off = off + (left & size)
                    size //= 2
                return carry

            lax.fori_loop(0, pad_lo_ref.shape[0], experts, 0)

        sweep(lambda cp: cp.start())
        sweep(lambda cp: cp.wait())


def _dispatch(xn, dest_flat, cnt, n_tok, n_blk):
    D = xn.shape[1]
    n_tiles = n_tok // TOK_TILE
    padded = ((cnt + MOE_BLK - 1) // MOE_BLK) * MOE_BLK
    pad_end = jnp.cumsum(padded)
    pad_lo = (pad_end - padded + cnt).astype(I32)
    pad_len = (padded - cnt).astype(I32)
    n_used = (pad_end[-1:] // MOE_BLK).astype(I32)
    tile = lambda i, *_: (jnp.minimum(i, n_tiles - 1),)
    return pl.pallas_call(
        functools.partial(_dispatch_kernel, n_blk=n_blk),
        grid_spec=pltpu.PrefetchScalarGridSpec(
            num_scalar_prefetch=3,
            grid=(n_tiles + 1,),
            in_specs=[pl.BlockSpec((TOK_TILE * TOP_K,), tile, memory_space=pltpu.SMEM),
                      pl.BlockSpec((TOK_TILE, D), lambda i, *_: (jnp.minimum(i, n_tiles - 1), 0))],
            out_specs=pl.BlockSpec(memory_space=pl.ANY),
            scratch_shapes=[pltpu.VMEM((MOE_BLK, D), F32), pltpu.SemaphoreType.DMA],
        ),
        out_shape=jax.ShapeDtypeStruct((n_blk * MOE_BLK, D), F32),
        compiler_params=_cparams(("arbitrary",), VMEM_LIMIT),
        name="dispatch",
    )(pad_lo, pad_len, n_used, dest_flat, xn)


class _WeightStream:
    def __init__(self, w_refs, stage, wbuf, sems, unit_rows):
        self.w_refs, self.stage, self.wbuf, self.sems = w_refs, stage, wbuf, sems
        self.unit_rows = unit_rows
        self.n_units = w_refs[0].shape[1] // unit_rows
        self.n_stage = stage.shape[0]

    def _copy(self, e, u, m):
        rows = pl.ds(pl.multiple_of(u * self.unit_rows, self.unit_rows), self.unit_rows)
        st = u % self.n_stage
        return pltpu.make_async_copy(self.w_refs[m].at[e, rows, :], self.stage.at[st, m],
                                     self.sems.at[st, m])

    def start(self, e, u):
        for m in range(len(self.w_refs)):
            self._copy(e, u, m).start(priority=1)

    def finish(self, e, u, slot):
        rows = pl.ds(pl.multiple_of(u * self.unit_rows, self.unit_rows), self.unit_rows)
        for m in range(len(self.w_refs)):
            self._copy(e, u, m).wait()
            self.wbuf[slot, m, rows, :] = self.stage[u % self.n_stage, m].astype(BF16)

    def prime(self, e):
        for u in range(self.n_stage):
            self.start(e, u)

    def convert(self, e, slot, lo, hi):
        def body(u, carry):
            self.finish(e, u, slot)

            @pl.when(u + self.n_stage < self.n_units)
            def _():
                self.start(e, u + self.n_stage)

            return carry

        lax.fori_loop(lo, hi, body, 0)


def _moe_step(plan, stream, compute, out_ref):
    be_ref, nxt_ref, slot_ref, first_ref, ulo_ref, uhi_ref, nu_ref = plan
    blk = pl.program_id(0)

    @pl.when(blk >= nu_ref[0])
    def _():
        out_ref[...] = jnp.zeros_like(out_ref)

    @pl.when(blk < nu_ref[0])
    def _():
        e = be_ref[blk]
        ne = nxt_ref[blk]
        slot = slot_ref[blk]

        @pl.when(blk == 0)
        def _():
            stream.prime(e)
            stream.convert(e, slot, 0, stream.n_units)

        @pl.when((first_ref[blk] == 1) & (ne >= 0))
        def _():
            stream.prime(ne)

        compute(slot, e)

        @pl.when(ne >= 0)
        def _():
            stream.convert(ne, 1 - slot, ulo_ref[blk], uhi_ref[blk])


def _row_variants(n_valid, fn):
    half = MOE_BLK // 2

    @pl.when(n_valid > half)
    def _():
        fn(MOE_BLK)

    @pl.when(n_valid <= half)
    def _():
        fn(half)


def _moe_up_kernel(be_ref, nxt_ref, slot_ref, first_ref, ulo_ref, uhi_ref, nu_ref, nval_ref,
                   xs_ref, wg_ref, wu_ref, bg_ref, bu_ref, act_ref, wbuf, stage, sems):
    stream = _WeightStream((wg_ref, wu_ref), stage, wbuf, sems, MOE_UNIT)
    half = act_ref.shape[1] // 2

    def compute(slot, e):
        def rows_fn(rows):
            xb = xs_ref[0:rows, :].astype(BF16)
            for c in range(2):
                cols = slice(c * half, (c + 1) * half)
                gate = jnp.dot(xb, wbuf[slot, 0, :, cols], preferred_element_type=F32) + bg_ref[e][:, cols]
                up = jnp.dot(xb, wbuf[slot, 1, :, cols], preferred_element_type=F32) + bu_ref[e][:, cols]
                gate = jnp.minimum(gate, SWIGLU_LIMIT)
                up = jnp.clip(up, -SWIGLU_LIMIT, SWIGLU_LIMIT)
                act = gate * jax.nn.sigmoid(SWIGLU_ALPHA * gate) * (up + 1.0)
                act_ref[0:rows, cols] = act.astype(act_ref.dtype)
            if rows < MOE_BLK:
                act_ref[rows:, :] = jnp.zeros((MOE_BLK - rows, act_ref.shape[1]), act_ref.dtype)

        _row_variants(nval_ref[pl.program_id(0)], rows_fn)

    _moe_step((be_ref, nxt_ref, slot_ref, first_ref, ulo_ref, uhi_ref, nu_ref), stream, compute, act_ref)


def _moe_down_kernel(be_ref, nxt_ref, slot_ref, first_ref, ulo_ref, uhi_ref, nu_ref, nval_ref,
                     act_ref, wd_ref, bd_ref, y_ref, wbuf, stage, sems):
    stream = _WeightStream((wd_ref,), stage, wbuf, sems, MOE_UNIT)

    def compute(slot, e):
        def rows_fn(rows):
            y_ref[0:rows, :] = (jnp.dot(act_ref[0:rows, :], wbuf[slot, 0], preferred_element_type=F32)
                                + bd_ref[e])
            if rows < MOE_BLK:
                y_ref[rows:, :] = jnp.zeros((MOE_BLK - rows, y_ref.shape[1]), y_ref.dtype)

        _row_variants(nval_ref[pl.program_id(0)], rows_fn)

    _moe_step((be_ref, nxt_ref, slot_ref, first_ref, ulo_ref, uhi_ref, nu_ref), stream, compute, y_ref)


def _moe_plan(cnt, n_blk, n_units):
    E = cnt.shape[0]
    nblk_e = (cnt + MOE_BLK - 1) // MOE_BLK
    blk_end = jnp.cumsum(nblk_e)
    blk_first = blk_end - nblk_e
    j = jnp.arange(n_blk, dtype=I32)
    be = jnp.minimum(jnp.sum((blk_end[None, :] <= j[:, None]).astype(I32), axis=1), E - 1)
    ids = jnp.arange(E, dtype=I32)
    own = be[:, None] == ids[None, :]

    def per_block(table):
        return jnp.sum(jnp.where(own, table[None, :], 0), axis=1)

    k = j - per_block(blk_first)
    nb = jnp.maximum(per_block(nblk_e), 1)
    nonempty = nblk_e > 0
    later = (ids[None, :] > ids[:, None]) & nonempty[None, :]
    nxt_e = jnp.min(jnp.where(later, ids[None, :], E), axis=1)
    nxt_e = jnp.where(nxt_e >= E, -1, nxt_e)
    slot_e = (jnp.cumsum(nonempty.astype(I32)) - 1) % 2
    n_valid = jnp.clip(per_block(cnt) - k * MOE_BLK, 0, MOE_BLK)
    plan = (be, per_block(nxt_e), per_block(slot_e), (k == 0).astype(I32),
            (k * n_units) // nb, ((k + 1) * n_units) // nb, blk_end[-1:].astype(I32), n_valid)
    return tuple(a.astype(I32) for a in plan)


def _moe(xs, cnt, w_gate, b_gate, w_up, b_up, w_down, b_down):
    P, D = xs.shape
    E, _, F = w_gate.shape
    n_blk = P // MOE_BLK
    n_stage = MOE_STAGES
    clamp = lambda blk, nu: jnp.maximum(jnp.minimum(blk, nu[0] - 1), 0)
    row_map = lambda blk, be, nx, sl, fi, lo, hi, nu, nv: (clamp(blk, nu), 0)
    out_map = lambda blk, *plan: (blk, 0)
    hbm = pl.BlockSpec(memory_space=pl.ANY)

    def bias_spec(width):
        return pl.BlockSpec((E, 1, width), lambda blk, *plan: (0, 0, 0), pipeline_mode=pl.Buffered(1))

    assert D == F, "one streaming plan serves all three expert matrices"
    plan = _moe_plan(cnt, n_blk, D // MOE_UNIT)
    act = pl.pallas_call(
        _moe_up_kernel,
        grid_spec=pltpu.PrefetchScalarGridSpec(
            num_scalar_prefetch=8,
            grid=(n_blk,),
            in_specs=[pl.BlockSpec((MOE_BLK, D), row_map), hbm, hbm, bias_spec(F), bias_spec(F)],
            out_specs=pl.BlockSpec((MOE_BLK, F), out_map),
            scratch_shapes=[pltpu.VMEM((2, 2, D, F), BF16),
                            pltpu.VMEM((n_stage, 2, MOE_UNIT, F), F32),
                            pltpu.SemaphoreType.DMA((n_stage, 2))],
        ),
        out_shape=jax.ShapeDtypeStruct((P, F), BF16),
        compiler_params=_cparams(("arbitrary",), VMEM_LIMIT),
        name="moe_up",
    )(*plan, xs, w_gate, w_up, b_gate.reshape(E, 1, F), b_up.reshape(E, 1, F))

    return pl.pallas_call(
        _moe_down_kernel,
        grid_spec=pltpu.PrefetchScalarGridSpec(
            num_scalar_prefetch=8,
            grid=(n_blk,),
            in_specs=[pl.BlockSpec((MOE_BLK, F), row_map), hbm, bias_spec(D)],
            out_specs=pl.BlockSpec((MOE_BLK, D), out_map),
            scratch_shapes=[pltpu.VMEM((2, 1, F, D), BF16),
                            pltpu.VMEM((n_stage, 1, MOE_UNIT, D), F32),
                            pltpu.SemaphoreType.DMA((n_stage, 1))],
        ),
        out_shape=jax.ShapeDtypeStruct((P, D), F32),
        compiler_params=_cparams(("arbitrary",), VMEM_LIMIT),
        name="moe_down",
    )(*plan, act, w_down, b_down.reshape(E, 1, D))


def _combine_kernel(dest_ref, dnext_ref, h2_ref, gate_ref, g_ref, y_ref, o_ref, ybuf, sem):
    i = pl.program_id(0)
    slot = i % 2

    def issue(dref, s, unrolled):
        def one(r):
            for k in range(TOP_K):
                pltpu.make_async_copy(y_ref.at[pl.ds(dref[r * TOP_K + k], 1), :],
                                      ybuf.at[s, k, pl.ds(r, 1), :], sem.at[s]).start()

        _for_rows(TOK_TILE, one, unrolled)

    @pl.when(i == 0)
    def _():
        issue(dest_ref, 0, False)

    for s in range(2):
        @pl.when((slot == s) & (i + 1 < pl.num_programs(0)))
        def _():
            issue(dnext_ref, 1 - s, True)

    for k in range(TOP_K):
        pltpu.make_async_copy(y_ref.at[pl.ds(0, TOK_TILE), :], ybuf.at[slot, k], sem.at[slot]).wait()
    gates = gate_ref[...]
    acc = h2_ref[...]
    for k in range(TOP_K):
        acc = acc + gates[:, k:k + 1] * ybuf[slot, k]
    o_ref[...] = _rms(acc, g_ref[...])


def _combine(h2, gates, dest_flat, y, g, n_seq):
    D = h2.shape[1]
    n_tiles = n_seq // TOK_TILE
    return pl.pallas_call(
        _combine_kernel,
        grid=(n_tiles,),
        in_specs=[pl.BlockSpec((TOK_TILE * TOP_K,), lambda i: (i,), memory_space=pltpu.SMEM),
                  pl.BlockSpec((TOK_TILE * TOP_K,), lambda i: (jnp.minimum(i + 1, n_tiles - 1),),
                               memory_space=pltpu.SMEM),
                  pl.BlockSpec((TOK_TILE, D), lambda i: (i, 0)),
                  pl.BlockSpec((TOK_TILE, LANES), lambda i: (i, 0)),
                  pl.BlockSpec((1, D), lambda i: (0, 0)),
                  pl.BlockSpec(memory_space=pl.ANY)],
        out_specs=pl.BlockSpec((TOK_TILE, D), lambda i: (i, 0)),
        out_shape=jax.ShapeDtypeStruct((n_seq, D), F32),
        scratch_shapes=[pltpu.VMEM((2, TOP_K, TOK_TILE, D), F32), pltpu.SemaphoreType.DMA((2,))],
        compiler_params=_cparams(("arbitrary",), VMEM_LIMIT),
        name="combine_norm",
    )(dest_flat, dest_flat, h2, gates, g, y)


def _rope_tables(S, n_meta_rows):
    half = HEAD_DIM // 2
    inv_freq = ROPE_THETA ** (-jnp.arange(half, dtype=F32) / half)
    lane = np.arange(LANES)
    fidx = lane % half
    sign = np.where((lane % HEAD_DIM) < half, -1.0, 1.0).astype(np.float32)
    pos_seq = N_META_TOK + jnp.arange(S)
    tail = jnp.arange(ROW_TILE)
    pos_tail = jnp.where(tail < n_meta_rows, tail % N_META_TOK, 0)
    pos = jnp.concatenate([pos_seq, pos_tail]).astype(F32)
    ang = (pos[:, None] * inv_freq[None, :])[:, fidx]
    return jnp.cos(ang), jnp.sin(ang) * sign[None, :]


def kernel(x, meta_tokens, attn_norm_g, w_in, attn_sinks, lower_bound_logits, rec_norm_g,
           w_attn_proj, w_rec_proj, w_out, ffn_norm_g, w_router, b_router, w_gate, b_gate,
           w_up, b_up, w_down, b_down, final_norm_g):
    B, S, D = x.shape
    assert w_in.shape[0] == 1, "single-layer block"
    assert S % ROW_TILE == 0 and (B * S) % ROW_TILE == 0 and S % SEG == 0
    assert S % SCAN_SEG == 0 and B % SCAN_BATCHES == 0
    NS = B * S
    NM = B * N_META_TOK
    NT = NS + NM
    R = -(-NT // ROW_TILE) * ROW_TILE
    assert R - NS == ROW_TILE and NT % TOK_TILE == 0 and NS % TOK_TILE == 0

    x2d = x.reshape(NS, D)
    tail_rows = jnp.concatenate([jnp.tile(meta_tokens.astype(x.dtype), (B, 1)),
                                 jnp.zeros((R - NT, D), x.dtype)], axis=0)

    w0 = w_in[0]
    kw = N_KV_HEADS * HEAD_DIM
    wq, wk, wv = w0[:, :Q_W], w0[:, Q_W:Q_W + kw], w0[:, Q_W + kw:Q_W + 2 * kw]
    dup = lambda w: jnp.concatenate([w[:, :HEAD_DIM]] * 2 + [w[:, HEAD_DIM:]] * 2, axis=1)
    w_qkv = jnp.concatenate([wq, dup(wk), dup(wv)], axis=1).astype(BF16)
    w_rest = w0[:, Q_W + 2 * kw:].astype(BF16)

    cos_t, sin_t = _rope_tables(S, NM)
    u, qkv = _norm_qkv(x2d, tail_rows, attn_norm_g[0].reshape(1, D), w_qkv, cos_t, sin_t,
                       S // ROW_TILE)
    rest = _proj_rest(u, w_rest)

    attn, attn_tail = _attention(qkv, attn_sinks[0], B, S)
    rec, rec_tail = _hgrn(rest, lower_bound_logits, rec_norm_g[0].reshape(1, REC_DIM), B, S)

    wr_hi = w_router[0].astype(BF16)
    wr_lo = (w_router[0] - wr_hi.astype(F32)).astype(BF16)
    lane_pad = ((0, 0), (0, LANES - N_EXPERTS))
    wr_pad = jnp.concatenate([jnp.pad(wr_hi, lane_pad), jnp.pad(wr_lo, lane_pad)], axis=1)
    br_pad = jnp.pad(b_router[0].reshape(1, N_EXPERTS), ((0, 0), (0, LANES - N_EXPERTS)),
                     constant_values=NEG_INF)
    h2, xn, logits = _mix(attn, attn_tail, rec, rec_tail, rest, x2d, tail_rows,
                          w_attn_proj[0].astype(BF16), w_rec_proj[0].astype(BF16),
                          w_out[0].astype(BF16), ffn_norm_g[0].reshape(1, D), wr_pad, br_pad)

    dest, gates, counts = _route(logits, NT)
    cnt = counts[0, :N_EXPERTS].astype(I32)
    n_blk = -(-(NT * TOP_K + N_EXPERTS * (MOE_BLK - 1)) // MOE_BLK)
    dest_flat = dest[:NT, :TOP_K].reshape(NT * TOP_K)

    xs = _dispatch(xn, dest_flat, cnt, NT, n_blk)
    y = _moe(xs, cnt, w_gate[0], b_gate[0], w_up[0], b_up[0], w_down[0], b_down[0])
    out = _combine(h2, gates, dest_flat, y, final_norm_g.reshape(1, D), NS)
    return out.reshape(B, S, D)
```

```python
import functools

import jax
import jax.numpy as jnp
import numpy as np
from jax import lax
from jax.experimental import pallas as pl
from jax.experimental.pallas import tpu as pltpu

F32 = jnp.float32
BF16 = jnp.bfloat16
I32 = jnp.int32

N_META_TOK = 16
HEAD_DIM = 64
N_Q_HEADS = 16
N_KV_HEADS = 2
ATTN_BLK = 128
ROPE_THETA = 10000.0
N_REC_HEADS = 8
REC_DIM = 128
REC_CHUNK = 16
N_EXPERTS = 32
TOP_K = 4
SWIGLU_ALPHA = 1.702
SWIGLU_LIMIT = 7.0
NORM_EPS = 1e-5
NEG_INF = -1e30

LANES = 128
ROW_TILE = 512
MIX_TILE = 256
SEG = 256
SCAN_SEG = 256
SCAN_BATCHES = 2
MOE_BLK = 256
MOE_UNIT = 128
MOE_STAGES = 2
TOK_TILE = 128
VMEM_LIMIT = 56 * 1024 * 1024

Q_W = N_Q_HEADS * HEAD_DIM
REC_W = N_REC_HEADS * REC_DIM


def _cparams(sem, vmem=None, **kw):
    return pltpu.CompilerParams(dimension_semantics=sem, vmem_limit_bytes=vmem, **kw)


def _rms(xf, g):
    return xf * lax.rsqrt(jnp.mean(xf * xf, axis=-1, keepdims=True) + NORM_EPS) * g


def _for_rows(n, fn, unrolled):
    if unrolled:
        for r in range(n):
            fn(r)
    else:
        def body(r, carry):
            fn(r)
            return carry

        lax.fori_loop(0, n, body, 0)


def _norm_qkv_kernel(x_ref, tail_ref, g_ref, w_ref, cos_ref, sin_ref, u_ref, qkv_ref, *, n_seq_tiles):
    h = jnp.where(pl.program_id(0) < n_seq_tiles, x_ref[...], tail_ref[...])
    u = _rms(h, g_ref[...]).astype(BF16)
    u_ref[...] = u
    p = jnp.dot(u, w_ref[...], preferred_element_type=F32)
    cos = cos_ref[...]
    sin = sin_ref[...]
    lane = lax.broadcasted_iota(I32, cos.shape, 1)
    first_half = (lane % HEAD_DIM) < (HEAD_DIM // 2)
    n_q = Q_W // LANES
    for c in range(n_q + 2):
        xs = p[:, c * LANES:(c + 1) * LANES]
        swapped = jnp.where(first_half, pltpu.roll(xs, LANES - HEAD_DIM // 2, 1),
                            pltpu.roll(xs, HEAD_DIM // 2, 1))
        r = xs * cos + swapped * sin
        if c < n_q:
            r = r * (HEAD_DIM ** -0.5)
        qkv_ref[:, c * LANES:(c + 1) * LANES] = r.astype(BF16)
    v0 = (n_q + 2) * LANES
    qkv_ref[:, v0:] = p[:, v0:].astype(BF16)


def _norm_qkv(x2d, tail_rows, g, w_qkv, cos_t, sin_t, tiles_per_seq):
    NS, D = x2d.shape
    R = NS + tail_rows.shape[0]
    W = w_qkv.shape[1]
    n_seq_tiles = NS // ROW_TILE

    def tab_map(i):
        return (jnp.where(i < n_seq_tiles, i % tiles_per_seq, tiles_per_seq), 0)

    return pl.pallas_call(
        functools.partial(_norm_qkv_kernel, n_seq_tiles=n_seq_tiles),
        grid=(R // ROW_TILE,),
        in_specs=[
            pl.BlockSpec((ROW_TILE, D), lambda i: (jnp.minimum(i, n_seq_tiles - 1), 0)),
            pl.BlockSpec((ROW_TILE, D), lambda i: (0, 0)),
            pl.BlockSpec((1, D), lambda i: (0, 0)),
            pl.BlockSpec((D, W), lambda i: (0, 0)),
            pl.BlockSpec((ROW_TILE, LANES), tab_map),
            pl.BlockSpec((ROW_TILE, LANES), tab_map),
        ],
        out_specs=[
            pl.BlockSpec((ROW_TILE, D), lambda i: (i, 0)),
            pl.BlockSpec((ROW_TILE, W), lambda i: (i, 0)),
        ],
        out_shape=[jax.ShapeDtypeStruct((R, D), BF16), jax.ShapeDtypeStruct((R, W), BF16)],
        compiler_params=_cparams(("parallel",), VMEM_LIMIT),
        name="norm_qkv",
    )(x2d, tail_rows, g, w_qkv, cos_t, sin_t)


def _matmul_kernel(x_ref, w_ref, o_ref):
    o_ref[...] = jnp.dot(x_ref[...], w_ref[...], preferred_element_type=F32).astype(o_ref.dtype)


def _proj_rest(u, w):
    R, D = u.shape
    N = w.shape[1]
    tn = 2048
    tm = ROW_TILE + ROW_TILE // 2
    if R % tm:
        tm = ROW_TILE
    return pl.pallas_call(
        _matmul_kernel,
        grid=(N // tn, R // tm),
        in_specs=[pl.BlockSpec((tm, D), lambda j, i: (i, 0)),
                  pl.BlockSpec((D, tn), lambda j, i: (0, j))],
        out_specs=pl.BlockSpec((tm, tn), lambda j, i: (i, j)),
        out_shape=jax.ShapeDtypeStruct((R, N), F32),
        compiler_params=_cparams(("parallel", "parallel"), VMEM_LIMIT),
        name="proj_rest",
    )(u, w)


def _attn_core(q_ref, k_groups, v_groups, mask_fn, sink_ref, write):
    nkeys = k_groups[0].shape[0]
    rows = q_ref.shape[0]
    heads_per_group = N_Q_HEADS // N_KV_HEADS
    width = heads_per_group * rows
    lo_q = lax.broadcasted_iota(I32, (rows, LANES), 1) < HEAD_DIM
    ki = lax.broadcasted_iota(I32, (nkeys, width), 0)
    qi = lax.broadcasted_iota(I32, (nkeys, width), 1) % rows
    mask = mask_fn(ki, qi)
    top_half = lax.broadcasted_iota(I32, (LANES, rows), 0) < HEAD_DIM
    for g in range(N_KV_HEADS):
        parts = []
        for j in range(heads_per_group // 2):
            qp = q_ref[:, (heads_per_group // 2 * g + j) * LANES:(heads_per_group // 2 * g + j + 1) * LANES]
            zero_q = jnp.zeros_like(qp)
            parts += [jnp.where(lo_q, qp, zero_q), jnp.where(lo_q, zero_q, qp)]
        q_stack = jnp.concatenate(parts, axis=0)
        s = lax.dot_general(k_groups[g], q_stack, (((1,), (1,)), ((), ())),
                            preferred_element_type=F32)
        s = jnp.where(mask, s, NEG_INF)
        sk = sink_ref[:, g * width:(g + 1) * width]
        m = jnp.maximum(jnp.max(s, axis=0, keepdims=True), sk)
        e = jnp.exp(s - m)
        den = jnp.sum(e, axis=0, keepdims=True) + jnp.exp(sk - m)
        prob = (e * (1.0 / den)).astype(BF16)
        o_t = lax.dot_general(v_groups[g], prob, (((0,), (0,)), ((), ())), preferred_element_type=F32)
        for j in range(heads_per_group // 2):
            even = o_t[:, (2 * j) * rows:(2 * j + 1) * rows]
            odd = o_t[:, (2 * j + 1) * rows:(2 * j + 2) * rows]
            pair_t = jnp.where(top_half, even, odd)
            write(heads_per_group // 2 * g + j, pair_t.T)


def _attn_seq_kernel(sink_ref, q_ref, kc0, kc1, vc0, vc1, kp0, kp1, vp0, vp1,
                     km0, km1, vm0, vm1, o_ref):
    n = pl.program_id(1)
    metas = ((km0, vm0), (km1, vm1))
    curs = ((kc0, vc0), (kc1, vc1))
    prevs = ((kp0, vp0), (kp1, vp1))
    first, second = slice(0, ATTN_BLK), slice(ATTN_BLK, 2 * ATTN_BLK)
    for sub in range(2):
        no_prev = jnp.where(n > 0, 0, 2 * ATTN_BLK) if sub == 0 else 0

        def mask_fn(ki, qi, no_prev=no_prev):
            prev_ok = ((ki >= N_META_TOK) & (ki < N_META_TOK + ATTN_BLK)
                       & (ki - N_META_TOK > qi + no_prev))
            cur_ok = (ki >= N_META_TOK + ATTN_BLK) & (ki - (N_META_TOK + ATTN_BLK) <= qi)
            return (ki < N_META_TOK) | prev_ok | cur_ok

        k_groups, v_groups = [], []
        for g in range(N_KV_HEADS):
            before = ((prevs[g][0][...], prevs[g][1][...]) if sub == 0
                      else (curs[g][0][first, :], curs[g][1][first, :]))
            own = first if sub == 0 else second
            k_groups.append(jnp.concatenate([metas[g][0][...], before[0], curs[g][0][own, :]], axis=0))
            v_groups.append(jnp.concatenate([metas[g][1][...], before[1], curs[g][1][own, :]], axis=0))

        def write(pr, acc, sub=sub):
            o_ref[sub * ATTN_BLK:(sub + 1) * ATTN_BLK, pr * LANES:(pr + 1) * LANES] = acc.astype(o_ref.dtype)

        _attn_core(q_ref.at[pl.ds(sub * ATTN_BLK, ATTN_BLK), :], k_groups, v_groups, mask_fn,
                   sink_ref, write)


def _attn_meta_kernel(sink_ref, q_ref, k0, k1, v0, v1, o_ref):
    nm = q_ref.shape[0]

    def mask_fn(ki, qi):
        return ((qi // N_META_TOK) == (ki // N_META_TOK)) & ((ki % N_META_TOK) <= (qi % N_META_TOK))

    o_ref[...] = jnp.zeros_like(o_ref)

    def write(pr, acc):
        o_ref[0:nm, pr * LANES:(pr + 1) * LANES] = acc.astype(o_ref.dtype)

    _attn_core(q_ref, [k0[...], k1[...]], [v0[...], v1[...]], mask_fn, sink_ref, write)


def _attention(qkv, sinks, B, S):
    R = qkv.shape[0]
    NS = B * S
    NM = B * N_META_TOK
    nb = S // ATTN_BLK
    qc = Q_W // LANES
    sink_seq = jnp.repeat(sinks.astype(F32), ATTN_BLK).reshape(1, N_Q_HEADS * ATTN_BLK)
    sink_meta = jnp.repeat(sinks.astype(F32), NM).reshape(1, N_Q_HEADS * NM)

    pair = 2 * ATTN_BLK
    npair = S // pair

    def kv_spec(col, prev):
        if prev:
            return pl.BlockSpec((ATTN_BLK, LANES), lambda b, n: (b * nb + jnp.maximum(2 * n - 1, 0), col))
        return pl.BlockSpec((pair, LANES), lambda b, n: (b * npair + n, col))

    def meta_spec(col):
        return pl.BlockSpec((N_META_TOK, LANES), lambda b, n: (NS // N_META_TOK + b, col))

    in_specs = [pl.BlockSpec(sink_seq.shape, lambda b, n: (0, 0)),
                pl.BlockSpec((pair, Q_W), lambda b, n: (b * npair + n, 0))]
    in_specs += [kv_spec(qc, False), kv_spec(qc + 1, False), kv_spec(qc + 2, False), kv_spec(qc + 3, False)]
    in_specs += [kv_spec(qc, True), kv_spec(qc + 1, True), kv_spec(qc + 2, True), kv_spec(qc + 3, True)]
    in_specs += [meta_spec(qc), meta_spec(qc + 1), meta_spec(qc + 2), meta_spec(qc + 3)]
    attn = pl.pallas_call(
        _attn_seq_kernel,
        grid=(B, npair),
        in_specs=in_specs,
        out_specs=pl.BlockSpec((pair, Q_W), lambda b, n: (b * npair + n, 0)),
        out_shape=jax.ShapeDtypeStruct((NS, Q_W), BF16),
        compiler_params=_cparams(("parallel", "parallel"), VMEM_LIMIT),
        name="attn_seq",
    )(sink_seq, *([qkv] * 13))

    mb = NS // NM
    tail = R - NS
    blk = lambda col: pl.BlockSpec((NM, LANES), lambda i: (mb, col))
    attn_tail = pl.pallas_call(
        _attn_meta_kernel,
        grid=(1,),
        in_specs=[pl.BlockSpec(sink_meta.shape, lambda i: (0, 0)),
                  pl.BlockSpec((NM, Q_W), lambda i: (mb, 0)),
                  blk(qc), blk(qc + 1), blk(qc + 2), blk(qc + 3)],
        out_specs=pl.BlockSpec((tail, Q_W), lambda i: (0, 0)),
        out_shape=jax.ShapeDtypeStruct((tail, Q_W), BF16),
        compiler_params=_cparams(("arbitrary",), VMEM_LIMIT),
        name="attn_meta",
    )(sink_meta, qkv, qkv, qkv, qkv, qkv)
    return attn, attn_tail


def _hgrn_prep_kernel(rq_ref, rf_ref, ri_ref, lbl_ref, qd_ref, kd_ref, ke_ref, v_ref, eb_ref):
    lbl = lbl_ref[...]
    e = jnp.exp(lbl - jnp.max(lbl, axis=0, keepdims=True))
    lb = e[0:1] / jnp.sum(e, axis=0, keepdims=True)
    f = lb + (1.0 - lb) * jax.nn.sigmoid(rf_ref[...])
    logf = jnp.log(f)
    k = 1.0 - f
    tm = logf.shape[0]
    ri = lax.broadcasted_iota(I32, (tm, tm), 0)
    ci = lax.broadcasted_iota(I32, (tm, tm), 1)
    same_chunk = (ri // REC_CHUNK) == (ci // REC_CHUNK)
    sel = jnp.concatenate([jnp.where(same_chunk & (ci <= ri), 1.0, 0.0).astype(BF16),
                           jnp.where(same_chunk, 1.0, 0.0).astype(BF16)], axis=0)
    hi = logf.astype(BF16)
    rem = logf - hi.astype(F32)
    mid = rem.astype(BF16)
    lo = (rem - mid.astype(F32)).astype(BF16)
    sums = (jnp.dot(sel, hi, preferred_element_type=F32) + jnp.dot(sel, mid, preferred_element_type=F32)
            + jnp.dot(sel, lo, preferred_element_type=F32))
    b = sums[:tm]
    total = sums[tm:]
    eb = jnp.exp(b)
    eb_ref[...] = eb
    qd_ref[...] = (rq_ref[...] * eb).astype(BF16)
    kd_ref[...] = (k * jnp.exp(-b)).astype(BF16)
    ke_ref[...] = (k * jnp.exp(total - b)).astype(BF16)
    v_ref[...] = ri_ref[...].astype(BF16)


def _hgrn_prep(rest, lbl):
    R = rest.shape[0]
    W = REC_W
    col = lambda c: pl.BlockSpec((SEG, W), lambda i: (i, c))
    row_out = pl.BlockSpec((SEG, W), lambda i: (i, 0))
    return pl.pallas_call(
        _hgrn_prep_kernel,
        grid=(R // SEG,),
        in_specs=[col(0), col(1), col(2), pl.BlockSpec(lbl.shape, lambda i: (0, 0))],
        out_specs=[row_out] * 5,
        out_shape=[jax.ShapeDtypeStruct((R, W), BF16)] * 4 + [jax.ShapeDtypeStruct((R, W), F32)],
        compiler_params=_cparams(("parallel",), VMEM_LIMIT),
        name="hgrn_prep",
    )(rest, rest, rest, lbl)


def _hgrn_intra(qd, kd, v):
    n = qd.shape[0]
    ri = lax.broadcasted_iota(I32, (n, n), 0)
    ci = lax.broadcasted_iota(I32, (n, n), 1)
    keep = ((ri // REC_CHUNK) == (ci // REC_CHUNK)) & (ri >= ci)
    sc = lax.dot_general(qd, kd, (((1,), (1,)), ((), ())), preferred_element_type=F32)
    sc = jnp.where(keep, sc, 0.0)
    return jnp.dot(sc.astype(BF16), v, preferred_element_type=F32)


def _hgrn_kv_t(v, ke):
    return lax.dot_general(v, ke, (((0,), (0,)), ((), ())), preferred_element_type=F32)


def _hgrn_finish(o, g, norm_g):
    y = o * lax.rsqrt(jnp.mean(o * o, axis=-1, keepdims=True) + NORM_EPS) * norm_g
    return (y * (g * jax.nn.sigmoid(g))).astype(BF16)


def _hgrn_meta_kernel(qd_ref, kd_ref, ke_ref, v_ref, rg_ref, ng_ref, st_ref, rec_ref, *, nbatch):
    rec_ref[...] = jnp.zeros_like(rec_ref)
    nm = nbatch * REC_CHUNK
    for h in range(N_REC_HEADS):
        cols = slice(h * REC_DIM, (h + 1) * REC_DIM)
        o = _hgrn_intra(qd_ref[0:nm, cols], kd_ref[0:nm, cols], v_ref[0:nm, cols])
        rec_ref[0:nm, cols] = _hgrn_finish(o, rg_ref[0:nm, cols], ng_ref[...])
        for b in range(nbatch):
            rows = slice(b * REC_CHUNK, (b + 1) * REC_CHUNK)
            st_ref[b, h] = _hgrn_kv_t(v_ref[rows, cols], ke_ref[rows, cols])


def _hgrn_scan_kernel(*refs, gb):
    qd, kd, ke, v, eb, rg = (refs[i * gb:(i + 1) * gb] for i in range(6))
    ng_ref, st0_ref, rec_ref, st_scr, o_scr = refs[6 * gb:]
    seg = rec_ref.shape[1]
    heads = [slice(h * REC_DIM, (h + 1) * REC_DIM) for h in range(N_REC_HEADS)]

    @pl.when(pl.program_id(1) == 0)
    def _():
        st_scr[...] = st0_ref[...]

    for i in range(gb):
        for cols in heads:
            o_scr[i, :, cols] = _hgrn_intra(qd[i][:, cols], kd[i][:, cols], v[i][:, cols])

    def chunk(c, carry):
        r0 = pl.multiple_of(c * REC_CHUNK, REC_CHUNK)
        rows = pl.ds(r0, REC_CHUNK)
        for i in range(gb):
            for h, cols in enumerate(heads):
                last8 = eb[i][pl.ds(pl.multiple_of(r0 + REC_CHUNK - 8, 8), 8), cols]
                dec = last8[7:8]
                st = st_scr[i, h]
                o_scr[i, rows, cols] += lax.dot_general(qd[i][rows, cols], st.astype(BF16),
                                                        (((1,), (1,)), ((), ())),
                                                        preferred_element_type=F32)
                st_scr[i, h] = st * dec + _hgrn_kv_t(v[i][rows, cols], ke[i][rows, cols])
        return carry

    lax.fori_loop(0, seg // REC_CHUNK, chunk, 0, unroll=True)
    for i in range(gb):
        for cols in heads:
            rec_ref[i, :, cols] = _hgrn_finish(o_scr[i, :, cols], rg[i][:, cols], ng_ref[...])


def _hgrn(rest, lbl, norm_g, B, S):
    R = rest.shape[0]
    NS = B * S
    W = REC_W
    tail = R - NS
    tb = NS // tail
    qd, kd, ke, v, eb = _hgrn_prep(rest, lbl)

    tail_spec = pl.BlockSpec((tail, W), lambda i: (tb, 0))
    state, rec_tail = pl.pallas_call(
        functools.partial(_hgrn_meta_kernel, nbatch=B),
        grid=(1,),
        in_specs=[tail_spec, tail_spec, tail_spec, tail_spec,
                  pl.BlockSpec((tail, W), lambda i: (tb, 3)),
                  pl.BlockSpec((1, REC_DIM), lambda i: (0, 0))],
        out_specs=[pl.BlockSpec((B, N_REC_HEADS, REC_DIM, REC_DIM), lambda i: (0, 0, 0, 0)),
                   pl.BlockSpec((tail, W), lambda i: (0, 0))],
        out_shape=[jax.ShapeDtypeStruct((B, N_REC_HEADS, REC_DIM, REC_DIM), F32),
                   jax.ShapeDtypeStruct((tail, W), BF16)],
        compiler_params=_cparams(("arbitrary",), VMEM_LIMIT),
        name="hgrn_meta",
    )(qd, kd, ke, v, rest, norm_g)

    gb = SCAN_BATCHES
    ns = S // SCAN_SEG

    def seg_specs(col):
        return [pl.BlockSpec((SCAN_SEG, W), lambda g, s, i=i: ((g * gb + i) * ns + s, col))
                for i in range(gb)]

    rec = pl.pallas_call(
        functools.partial(_hgrn_scan_kernel, gb=gb),
        grid=(B // gb, ns),
        in_specs=seg_specs(0) * 5 + seg_specs(3)
        + [pl.BlockSpec((1, REC_DIM), lambda g, s: (0, 0)),
           pl.BlockSpec((gb, N_REC_HEADS, REC_DIM, REC_DIM), lambda g, s: (g, 0, 0, 0))],
        out_specs=pl.BlockSpec((gb, SCAN_SEG, W), lambda g, s: (g, s, 0)),
        out_shape=jax.ShapeDtypeStruct((B, S, W), BF16),
        scratch_shapes=[pltpu.VMEM((gb, N_REC_HEADS, REC_DIM, REC_DIM), F32),
                        pltpu.VMEM((gb, SCAN_SEG, W), F32)],
        compiler_params=_cparams(("parallel", "arbitrary"), VMEM_LIMIT),
        name="hgrn_scan",
    )(*([qd] * gb + [kd] * gb + [ke] * gb + [v] * gb + [eb] * gb + [rest] * gb), norm_g, state)
    return rec.reshape(NS, W), rec_tail


def _mix_kernel(attn_ref, attn_t_ref, rec_ref, rec_t_ref, ga_ref, gr_ref, x_ref, tail_ref,
                wa_ref, wr_ref, wo_ref, g_ref, wrt_ref, brt_ref, h2_ref, xn_ref, lg_ref, *, n_seq_tiles):
    is_seq = pl.program_id(0) < n_seq_tiles
    attn = jnp.where(is_seq, attn_ref[...], attn_t_ref[...])
    rec = jnp.where(is_seq, rec_ref[...], rec_t_ref[...])
    h = jnp.where(is_seq, x_ref[...], tail_ref[...])
    a = jnp.dot(attn, wa_ref[...], preferred_element_type=F32)
    r = jnp.dot(rec, wr_ref[...], preferred_element_type=F32)
    mixed = jax.nn.sigmoid(ga_ref[...]) * a + jax.nn.sigmoid(gr_ref[...]) * r
    h2 = h + jnp.dot(mixed.astype(BF16), wo_ref[...], preferred_element_type=F32)
    h2_ref[...] = h2
    xn = _rms(h2, g_ref[...])
    xn_ref[...] = xn
    xn_hi = xn.astype(BF16)
    xn_lo = (xn - xn_hi.astype(F32)).astype(BF16)
    w2 = wrt_ref[...]
    p_hi = jnp.dot(xn_hi, w2, preferred_element_type=F32)
    p_lo = jnp.dot(xn_lo, w2[:, :LANES], preferred_element_type=F32)
    lg_ref[...] = p_hi[:, :LANES] + p_hi[:, LANES:] + p_lo + brt_ref[...]


def _mix(attn, attn_tail, rec, rec_tail, rest, x2d, tail_rows, wa, wr, wo, g, w_router, b_router):
    NS, D = x2d.shape
    R = NS + tail_rows.shape[0]
    tm = MIX_TILE
    nst = NS // tm
    const = lambda shape: pl.BlockSpec(shape, lambda i: (0, 0), pipeline_mode=pl.Buffered(1))
    row = lambda w: pl.BlockSpec((tm, w), lambda i: (i, 0))
    seq = lambda w: pl.BlockSpec((tm, w), lambda i: (jnp.minimum(i, nst - 1), 0))
    tl = lambda w: pl.BlockSpec((tm, w), lambda i: (jnp.maximum(i - nst, 0), 0))
    return pl.pallas_call(
        functools.partial(_mix_kernel, n_seq_tiles=nst),
        grid=(R // tm,),
        in_specs=[seq(Q_W), tl(Q_W), seq(REC_W), tl(REC_W),
                  pl.BlockSpec((tm, D), lambda i: (i, 2)), pl.BlockSpec((tm, D), lambda i: (i, 3)),
                  seq(D), tl(D), const(wa.shape), const(wr.shape), const(wo.shape), const(g.shape),
                  const(w_router.shape), const(b_router.shape)],
        out_specs=[row(D), row(D), row(LANES)],
        out_shape=[jax.ShapeDtypeStruct((R, D), F32), jax.ShapeDtypeStruct((R, D), F32),
                   jax.ShapeDtypeStruct((R, LANES), F32)],
        compiler_params=_cparams(("parallel",), VMEM_LIMIT),
        name="mix_outproj",
    )(attn, attn_tail, rec, rec_tail, rest, rest, x2d, tail_rows, wa, wr, wo, g, w_router, b_router)


def _route_kernel(lg_ref, dest_ref, gate_ref, cnt_ref, cnt_scr, carry_scr, start_scr, *, n_tok):
    ph = pl.program_id(0)
    i = pl.program_id(1)
    tm = lg_ref.shape[0]

    @pl.when((ph == 0) & (i == 0))
    def _():
        cnt_scr[...] = jnp.zeros_like(cnt_scr)

    lane = lax.broadcasted_iota(I32, (tm, LANES), 1)
    valid = (i * tm + lax.broadcasted_iota(I32, (tm, LANES), 0)) < n_tok
    work = lg_ref[...]
    onehots, vals = [], []
    for _ in range(TOP_K):
        m = jnp.max(work, axis=-1, keepdims=True)
        idx = jnp.min(jnp.where(work == m, lane, LANES), axis=-1, keepdims=True)
        oh = lane == idx
        onehots.append(oh)
        vals.append(m)
        work = jnp.where(oh, -jnp.inf, work)
    multi = jnp.zeros((tm, LANES), F32)
    for oh in onehots:
        multi = multi + jnp.where(oh & valid, 1.0, 0.0)
    tile_cnt = jnp.sum(multi, axis=0, keepdims=True)

    @pl.when(ph == 0)
    def _():
        cnt_scr[...] += tile_cnt

    @pl.when(ph == 1)
    def _():
        @pl.when(i == 0)
        def _():
            c = cnt_scr[...]
            padded = jnp.ceil(c * (1.0 / MOE_BLK)) * MOE_BLK
            before = (lax.broadcasted_iota(I32, (LANES, LANES), 0)
                      < lax.broadcasted_iota(I32, (LANES, LANES), 1))
            start = jnp.dot(jnp.broadcast_to(padded, (8, LANES)), jnp.where(before, 1.0, 0.0),
                            preferred_element_type=F32, precision=lax.Precision.HIGHEST)
            start_scr[...] = start[0:1]
            carry_scr[...] = jnp.zeros_like(carry_scr)
            cnt_ref[...] = c

        earlier = (lax.broadcasted_iota(I32, (tm, tm), 1) < lax.broadcasted_iota(I32, (tm, tm), 0))
        prefix = jnp.dot(jnp.where(earlier, 1.0, 0.0).astype(BF16), multi.astype(BF16),
                         preferred_element_type=F32)
        base = prefix + carry_scr[...] + start_scr[...]
        den = jnp.zeros_like(vals[0])
        for v in vals:
            den = den + jnp.exp(v - vals[0])
        dest = jnp.zeros((tm, LANES), F32)
        gate = jnp.zeros((tm, LANES), F32)
        for k in range(TOP_K):
            d_k = jnp.sum(jnp.where(onehots[k], base, 0.0), axis=-1, keepdims=True)
            dest = jnp.where(lane == k, d_k, dest)
            gate = jnp.where(lane == k, jnp.exp(vals[k] - vals[0]) / den, gate)
        dest_ref[...] = dest.astype(I32)
        gate_ref[...] = gate
        carry_scr[...] += tile_cnt


def _route(logits, n_tok):
    R = logits.shape[0]
    tm = ROW_TILE
    blk = pl.BlockSpec((tm, LANES), lambda p, i: (i * p, 0))
    return pl.pallas_call(
        functools.partial(_route_kernel, n_tok=n_tok),
        grid=(2, R // tm),
        in_specs=[pl.BlockSpec((tm, LANES), lambda p, i: (i, 0))],
        out_specs=[blk, blk, pl.BlockSpec((1, LANES), lambda p, i: (0, 0))],
        out_shape=[jax.ShapeDtypeStruct((R, LANES), I32), jax.ShapeDtypeStruct((R, LANES), F32),
                   jax.ShapeDtypeStruct((1, LANES), F32)],
        scratch_shapes=[pltpu.VMEM((1, LANES), F32)] * 3,
        compiler_params=_cparams(("arbitrary", "arbitrary"), VMEM_LIMIT),
        name="route",
    )(logits)


def _dispatch_kernel(pad_lo_ref, pad_len_ref, nu_ref, dest_ref, xn_ref, xs_ref, zero_scr, sem, *, n_blk):
    i = pl.program_id(0)
    last = pl.num_programs(0) - 1

    @pl.when(i < last)
    def _():
        def one(r):
            for k in range(TOP_K):
                pltpu.make_async_copy(xn_ref.at[pl.ds(r, 1), :],
                                      xs_ref.at[pl.ds(dest_ref[r * TOP_K + k], 1), :], sem).start()

        _for_rows(TOK_TILE, one, True)
        for _ in range(TOP_K):
            pltpu.make_async_copy(xn_ref, xs_ref.at[pl.ds(0, TOK_TILE), :], sem).wait()

    @pl.when(i == last)
    def _():
        zero_scr[...] = jnp.zeros_like(zero_scr)

        def block_copy(b):
            rows = pl.ds(pl.multiple_of(b * MOE_BLK, MOE_BLK), MOE_BLK)
            return pltpu.make_async_copy(zero_scr, xs_ref.at[rows, :], sem)

        def sweep(act):
            def blocks(b, carry):
                act(block_copy(b))
                return carry

            lax.fori_loop(nu_ref[0], n_blk, blocks, 0)

            def experts(e, carry):
                lo = pad_lo_ref[e]
                n_pad = pad_len_ref[e]
                head = jnp.minimum((-lo) & 7, n_pad)

                def rows(r, c):
                    act(pltpu.make_async_copy(zero_scr.at[pl.ds(0, 1), :],
                                              xs_ref.at[pl.ds(lo + r, 1), :], sem))
                    return c

                lax.fori_loop(0, head, rows, 0)
                off = lo + head
                left = n_pad - head
                size = MOE_BLK // 2
                while size >= 8:
                    @pl.when((left & size) != 0)
                    def _(off=off, size=size):
                        act(pltpu.make_async_copy(zero_scr.at[pl.ds(0, size), :],
                                                  xs_ref.at[pl.ds(pl.multiple_of(off, 8), size), :], sem))

                    off = off + (left & size)
                    size //= 2
                return carry

            lax.fori_loop(0, pad_lo_ref.shape[0], experts, 0)

        sweep(lambda cp: cp.start())
        sweep(lambda cp: cp.wait())


def _dispatch(xn, dest_flat, cnt, n_tok, n_blk):
    D = xn.shape[1]
    n_tiles = n_tok // TOK_TILE
    padded = ((cnt + MOE_BLK - 1) // MOE_BLK) * MOE_BLK
    pad_end = jnp.cumsum(padded)
    pad_lo = (pad_end - padded + cnt).astype(I32)
    pad_len = (padded - cnt).astype(I32)
    n_used = (pad_end[-1:] // MOE_BLK).astype(I32)
    tile = lambda i, *_: (jnp.minimum(i, n_tiles - 1),)
    return pl.pallas_call(
        functools.partial(_dispatch_kernel, n_blk=n_blk),
        grid_spec=pltpu.PrefetchScalarGridSpec(
            num_scalar_prefetch=3,
            grid=(n_tiles + 1,),
            in_specs=[pl.BlockSpec((TOK_TILE * TOP_K,), tile, memory_space=pltpu.SMEM),
                      pl.BlockSpec((TOK_TILE, D), lambda i, *_: (jnp.minimum(i, n_tiles - 1), 0))],
            out_specs=pl.BlockSpec(memory_space=pl.ANY),
            scratch_shapes=[pltpu.VMEM((MOE_BLK, D), F32), pltpu.SemaphoreType.DMA],
        ),
        out_shape=jax.ShapeDtypeStruct((n_blk * MOE_BLK, D), F32),
        compiler_params=_cparams(("arbitrary",), VMEM_LIMIT),
        name="dispatch",
    )(pad_lo, pad_len, n_used, dest_flat, xn)


class _WeightStream:
    def __init__(self, w_refs, stage, wbuf, sems, unit_rows):
        self.w_refs, self.stage, self.wbuf, self.sems = w_refs, stage, wbuf, sems
        self.unit_rows = unit_rows
        self.n_units = w_refs[0].shape[1] // unit_rows
        self.n_stage = stage.shape[0]

    def _copy(self, e, u, m):
        rows = pl.ds(pl.multiple_of(u * self.unit_rows, self.unit_rows), self.unit_rows)
        st = u % self.n_stage
        return pltpu.make_async_copy(self.w_refs[m].at[e, rows, :], self.stage.at[st, m],
                                     self.sems.at[st, m])

    def start(self, e, u):
        for m in range(len(self.w_refs)):
            self._copy(e, u, m).start(priority=1)

    def finish(self, e, u, slot):
        rows = pl.ds(pl.multiple_of(u * self.unit_rows, self.unit_rows), self.unit_rows)
        for m in range(len(self.w_refs)):
            self._copy(e, u, m).wait()
            self.wbuf[slot, m, rows, :] = self.stage[u % self.n_stage, m].astype(BF16)

    def prime(self, e):
        for u in range(self.n_stage):
            self.start(e, u)

    def convert(self, e, slot, lo, hi):
        def body(u, carry):
            self.finish(e, u, slot)

            @pl.when(u + self.n_stage < self.n_units)
            def _():
                self.start(e, u + self.n_stage)

            return carry

        lax.fori_loop(lo, hi, body, 0)


def _moe_step(plan, stream, compute, out_ref):
    be_ref, nxt_ref, slot_ref, first_ref, ulo_ref, uhi_ref, nu_ref = plan
    blk = pl.program_id(0)

    @pl.when(blk >= nu_ref[0])
    def _():
        out_ref[...] = jnp.zeros_like(out_ref)

    @pl.when(blk < nu_ref[0])
    def _():
        e = be_ref[blk]
        ne = nxt_ref[blk]
        slot = slot_ref[blk]

        @pl.when(blk == 0)
        def _():
            stream.prime(e)
            stream.convert(e, slot, 0, stream.n_units)

        @pl.when((first_ref[blk] == 1) & (ne >= 0))
        def _():
            stream.prime(ne)

        compute(slot, e)

        @pl.when(ne >= 0)
        def _():
            stream.convert(ne, 1 - slot, ulo_ref[blk], uhi_ref[blk])


def _row_variants(n_valid, fn):
    half = MOE_BLK // 2

    @pl.when(n_valid > half)
    def _():
        fn(MOE_BLK)

    @pl.when(n_valid <= half)
    def _():
        fn(half)


def _moe_up_kernel(be_ref, nxt_ref, slot_ref, first_ref, ulo_ref, uhi_ref, nu_ref, nval_ref,
                   xs_ref, wg_ref, wu_ref, bg_ref, bu_ref, act_ref, wbuf, stage, sems):
    stream = _WeightStream((wg_ref, wu_ref), stage, wbuf, sems, MOE_UNIT)
    half = act_ref.shape[1] // 2

    def compute(slot, e):
        def rows_fn(rows):
            xb = xs_ref[0:rows, :].astype(BF16)
            for c in range(2):
                cols = slice(c * half, (c + 1) * half)
                gate = jnp.dot(xb, wbuf[slot, 0, :, cols], preferred_element_type=F32) + bg_ref[e][:, cols]
                up = jnp.dot(xb, wbuf[slot, 1, :, cols], preferred_element_type=F32) + bu_ref[e][:, cols]
                gate = jnp.minimum(gate, SWIGLU_LIMIT)
                up = jnp.clip(up, -SWIGLU_LIMIT, SWIGLU_LIMIT)
                act = gate * jax.nn.sigmoid(SWIGLU_ALPHA * gate) * (up + 1.0)
                act_ref[0:rows, cols] = act.astype(act_ref.dtype)
            if rows < MOE_BLK:
                act_ref[rows:, :] = jnp.zeros((MOE_BLK - rows, act_ref.shape[1]), act_ref.dtype)

        _row_variants(nval_ref[pl.program_id(0)], rows_fn)

    _moe_step((be_ref, nxt_ref, slot_ref, first_ref, ulo_ref, uhi_ref, nu_ref), stream, compute, act_ref)


def _moe_down_kernel(be_ref, nxt_ref, slot_ref, first_ref, ulo_ref, uhi_ref, nu_ref, nval_ref,
                     act_ref, wd_ref, bd_ref, y_ref, wbuf, stage, sems):
    stream = _WeightStream((wd_ref,), stage, wbuf, sems, MOE_UNIT)

    def compute(slot, e):
        def rows_fn(rows):
            y_ref[0:rows, :] = (jnp.dot(act_ref[0:rows, :], wbuf[slot, 0], preferred_element_type=F32)
                                + bd_ref[e])
            if rows < MOE_BLK:
                y_ref[rows:, :] = jnp.zeros((MOE_BLK - rows, y_ref.shape[1]), y_ref.dtype)

        _row_variants(nval_ref[pl.program_id(0)], rows_fn)

    _moe_step((be_ref, nxt_ref, slot_ref, first_ref, ulo_ref, uhi_ref, nu_ref), stream, compute, y_ref)


def _moe_plan(cnt, n_blk, n_units):
    E = cnt.shape[0]
    nblk_e = (cnt + MOE_BLK - 1) // MOE_BLK
    blk_end = jnp.cumsum(nblk_e)
    blk_first = blk_end - nblk_e
    j = jnp.arange(n_blk, dtype=I32)
    be = jnp.minimum(jnp.sum((blk_end[None, :] <= j[:, None]).astype(I32), axis=1), E - 1)
    ids = jnp.arange(E, dtype=I32)
    own = be[:, None] == ids[None, :]

    def per_block(table):
        return jnp.sum(jnp.where(own, table[None, :], 0), axis=1)

    k = j - per_block(blk_first)
    nb = jnp.maximum(per_block(nblk_e), 1)
    nonempty = nblk_e > 0
    later = (ids[None, :] > ids[:, None]) & nonempty[None, :]
    nxt_e = jnp.min(jnp.where(later, ids[None, :], E), axis=1)
    nxt_e = jnp.where(nxt_e >= E, -1, nxt_e)
    slot_e = (jnp.cumsum(nonempty.astype(I32)) - 1) % 2
    n_valid = jnp.clip(per_block(cnt) - k * MOE_BLK, 0, MOE_BLK)
    plan = (be, per_block(nxt_e), per_block(slot_e), (k == 0).astype(I32),
            (k * n_units) // nb, ((k + 1) * n_units) // nb, blk_end[-1:].astype(I32), n_valid)
    return tuple(a.astype(I32) for a in plan)


def _moe(xs, cnt, w_gate, b_gate, w_up, b_up, w_down, b_down):
    P, D = xs.shape
    E, _, F = w_gate.shape
    n_blk = P // MOE_BLK
    n_stage = MOE_STAGES
    clamp = lambda blk, nu: jnp.maximum(jnp.minimum(blk, nu[0] - 1), 0)
    row_map = lambda blk, be, nx, sl, fi, lo, hi, nu, nv: (clamp(blk, nu), 0)
    out_map = lambda blk, *plan: (blk, 0)
    hbm = pl.BlockSpec(memory_space=pl.ANY)

    def bias_spec(width):
        return pl.BlockSpec((E, 1, width), lambda blk, *plan: (0, 0, 0), pipeline_mode=pl.Buffered(1))

    assert D == F, "one streaming plan serves all three expert matrices"
    plan = _moe_plan(cnt, n_blk, D // MOE_UNIT)
    act = pl.pallas_call(
        _moe_up_kernel,
        grid_spec=pltpu.PrefetchScalarGridSpec(
            num_scalar_prefetch=8,
            grid=(n_blk,),
            in_specs=[pl.BlockSpec((MOE_BLK, D), row_map), hbm, hbm, bias_spec(F), bias_spec(F)],
            out_specs=pl.BlockSpec((MOE_BLK, F), out_map),
            scratch_shapes=[pltpu.VMEM((2, 2, D, F), BF16),
                            pltpu.VMEM((n_stage, 2, MOE_UNIT, F), F32),
                            pltpu.SemaphoreType.DMA((n_stage, 2))],
        ),
        out_shape=jax.ShapeDtypeStruct((P, F), BF16),
        compiler_params=_cparams(("arbitrary",), VMEM_LIMIT),
        name="moe_up",
    )(*plan, xs, w_gate, w_up, b_gate.reshape(E, 1, F), b_up.reshape(E, 1, F))

    return pl.pallas_call(
        _moe_down_kernel,
        grid_spec=pltpu.PrefetchScalarGridSpec(
            num_scalar_prefetch=8,
            grid=(n_blk,),
            in_specs=[pl.BlockSpec((MOE_BLK, F), row_map), hbm, bias_spec(D)],
            out_specs=pl.BlockSpec((MOE_BLK, D), out_map),
            scratch_shapes=[pltpu.VMEM((2, 1, F, D), BF16),
                            pltpu.VMEM((n_stage, 1, MOE_UNIT, D), F32),
                            pltpu.SemaphoreType.DMA((n_stage, 1))],
        ),
        out_shape=jax.ShapeDtypeStruct((P, D), F32),
        compiler_params=_cparams(("arbitrary",), VMEM_LIMIT),
        name="moe_down",
    )(*plan, act, w_down, b_down.reshape(E, 1, D))


def _combine_kernel(dest_ref, dnext_ref, h2_ref, gate_ref, g_ref, y_ref, o_ref, ybuf, sem):
    i = pl.program_id(0)
    slot = i % 2

    def issue(dref, s, unrolled):
        def one(r):
            for k in range(TOP_K):
                pltpu.make_async_copy(y_ref.at[pl.ds(dref[r * TOP_K + k], 1), :],
                                      ybuf.at[s, k, pl.ds(r, 1), :], sem.at[s]).start()

        _for_rows(TOK_TILE, one, unrolled)

    @pl.when(i == 0)
    def _():
        issue(dest_ref, 0, False)

    for s in range(2):
        @pl.when((slot == s) & (i + 1 < pl.num_programs(0)))
        def _():
            issue(dnext_ref, 1 - s, True)

    for k in range(TOP_K):
        pltpu.make_async_copy(y_ref.at[pl.ds(0, TOK_TILE), :], ybuf.at[slot, k], sem.at[slot]).wait()
    gates = gate_ref[...]
    acc = h2_ref[...]
    for k in range(TOP_K):
        acc = acc + gates[:, k:k + 1] * ybuf[slot, k]
    o_ref[...] = _rms(acc, g_ref[...])


def _combine(h2, gates, dest_flat, y, g, n_seq):
    D = h2.shape[1]
    n_tiles = n_seq // TOK_TILE
    return pl.pallas_call(
        _combine_kernel,
        grid=(n_tiles,),
        in_specs=[pl.BlockSpec((TOK_TILE * TOP_K,), lambda i: (i,), memory_space=pltpu.SMEM),
                  pl.BlockSpec((TOK_TILE * TOP_K,), lambda i: (jnp.minimum(i + 1, n_tiles - 1),),
                               memory_space=pltpu.SMEM),
                  pl.BlockSpec((TOK_TILE, D), lambda i: (i, 0)),
                  pl.BlockSpec((TOK_TILE, LANES), lambda i: (i, 0)),
                  pl.BlockSpec((1, D), lambda i: (0, 0)),
                  pl.BlockSpec(memory_space=pl.ANY)],
        out_specs=pl.BlockSpec((TOK_TILE, D), lambda i: (i, 0)),
        out_shape=jax.ShapeDtypeStruct((n_seq, D), F32),
        scratch_shapes=[pltpu.VMEM((2, TOP_K, TOK_TILE, D), F32), pltpu.SemaphoreType.DMA((2,))],
        compiler_params=_cparams(("arbitrary",), VMEM_LIMIT),
        name="combine_norm",
    )(dest_flat, dest_flat, h2, gates, g, y)


def _rope_tables(S, n_meta_rows):
    half = HEAD_DIM // 2
    inv_freq = ROPE_THETA ** (-jnp.arange(half, dtype=F32) / half)
    lane = np.arange(LANES)
    fidx = lane % half
    sign = np.where((lane % HEAD_DIM) < half, -1.0, 1.0).astype(np.float32)
    pos_seq = N_META_TOK + jnp.arange(S)
    tail = jnp.arange(ROW_TILE)
    pos_tail = jnp.where(tail < n_meta_rows, tail % N_META_TOK, 0)
    pos = jnp.concatenate([pos_seq, pos_tail]).astype(F32)
    ang = (pos[:, None] * inv_freq[None, :])[:, fidx]
    return jnp.cos(ang), jnp.sin(ang) * sign[None, :]


def kernel(x, meta_tokens, attn_norm_g, w_in, attn_sinks, lower_bound_logits, rec_norm_g,
           w_attn_proj, w_rec_proj, w_out, ffn_norm_g, w_router, b_router, w_gate, b_gate,
           w_up, b_up, w_down, b_down, final_norm_g):
    B, S, D = x.shape
    assert w_in.shape[0] == 1, "single-layer block"
    assert S % ROW_TILE == 0 and (B * S) % ROW_TILE == 0 and S % SEG == 0
    assert S % SCAN_SEG == 0 and B % SCAN_BATCHES == 0
    NS = B * S
    NM = B * N_META_TOK
    NT = NS + NM
    R = -(-NT // ROW_TILE) * ROW_TILE
    assert R - NS == ROW_TILE and NT % TOK_TILE == 0 and NS % TOK_TILE == 0

    x2d = x.reshape(NS, D)
    tail_rows = jnp.concatenate([jnp.tile(meta_tokens.astype(x.dtype), (B, 1)),
                                 jnp.zeros((R - NT, D), x.dtype)], axis=0)

    w0 = w_in[0]
    kw = N_KV_HEADS * HEAD_DIM
    wq, wk, wv = w0[:, :Q_W], w0[:, Q_W:Q_W + kw], w0[:, Q_W + kw:Q_W + 2 * kw]
    dup = lambda w: jnp.concatenate([w[:, :HEAD_DIM]] * 2 + [w[:, HEAD_DIM:]] * 2, axis=1)
    w_qkv = jnp.concatenate([wq, dup(wk), dup(wv)], axis=1).astype(BF16)
    w_rest = w0[:, Q_W + 2 * kw:].astype(BF16)

    cos_t, sin_t = _rope_tables(S, NM)
    u, qkv = _norm_qkv(x2d, tail_rows, attn_norm_g[0].reshape(1, D), w_qkv, cos_t, sin_t,
                       S // ROW_TILE)
    rest = _proj_rest(u, w_rest)

    attn, attn_tail = _attention(qkv, attn_sinks[0], B, S)
    rec, rec_tail = _hgrn(rest, lower_bound_logits, rec_norm_g[0].reshape(1, REC_DIM), B, S)

    wr_hi = w_router[0].astype(BF16)
    wr_lo = (w_router[0] - wr_hi.astype(F32)).astype(BF16)
    lane_pad = ((0, 0), (0, LANES - N_EXPERTS))
    wr_pad = jnp.concatenate([jnp.pad(wr_hi, lane_pad), jnp.pad(wr_lo, lane_pad)], axis=1)
    br_pad = jnp.pad(b_router[0].reshape(1, N_EXPERTS), ((0, 0), (0, LANES - N_EXPERTS)),
                     constant_values=NEG_INF)
    h2, xn, logits = _mix(attn, attn_tail, rec, rec_tail, rest, x2d, tail_rows,
                          w_attn_proj[0].astype(BF16), w_rec_proj[0].astype(BF16),
                          w_out[0].astype(BF16), ffn_norm_g[0].reshape(1, D), wr_pad, br_pad)

    dest, gates, counts = _route(logits, NT)
    cnt = counts[0, :N_EXPERTS].astype(I32)
    n_blk = -(-(NT * TOP_K + N_EXPERTS * (MOE_BLK - 1)) // MOE_BLK)
    dest_flat = dest[:NT, :TOP_K].reshape(NT * TOP_K)

    xs = _dispatch(xn, dest_flat, cnt, NT, n_blk)
    y = _moe(xs, cnt, w_gate[0], b_gate[0], w_up[0], b_up[0], w_down[0], b_down[0])
    out = _combine(h2, gates, dest_flat, y, final_norm_g.reshape(1, D), NS)
    return out.reshape(B, S, D)
```

```python
import functools

import jax
import jax.numpy as jnp
import numpy as np
from jax import lax
from jax.experimental import pallas as pl
from jax.experimental.pallas import tpu as pltpu

F32 = jnp.float32
BF16 = jnp.bfloat16
I32 = jnp.int32

N_META_TOK = 16
HEAD_DIM = 64
N_Q_HEADS = 16
N_KV_HEADS = 2
ATTN_BLK = 128
ROPE_THETA = 10000.0
N_REC_HEADS = 8
REC_DIM = 128
REC_CHUNK = 16
N_EXPERTS = 32
TOP_K = 4
SWIGLU_ALPHA = 1.702
SWIGLU_LIMIT = 7.0
NORM_EPS = 1e-5
NEG_INF = -1e30

LANES = 128
ROW_TILE = 512
MIX_TILE = 256
SEG = 256
SCAN_SEG = 256
SCAN_BATCHES = 2
MOE_BLK = 256
MOE_UNIT = 128
MOE_STAGES = 3
TOK_TILE = 128
VMEM_LIMIT = 56 * 1024 * 1024

Q_W = N_Q_HEADS * HEAD_DIM
REC_W = N_REC_HEADS * REC_DIM


def _cparams(sem, vmem=None, **kw):
    return pltpu.CompilerParams(dimension_semantics=sem, vmem_limit_bytes=vmem, **kw)


def _rms(xf, g):
    return xf * lax.rsqrt(jnp.mean(xf * xf, axis=-1, keepdims=True) + NORM_EPS) * g


def _for_rows(n, fn, unrolled):
    if unrolled:
        for r in range(n):
            fn(r)
    else:
        def body(r, carry):
            fn(r)
            return carry

        lax.fori_loop(0, n, body, 0)


def _norm_qkv_kernel(x_ref, tail_ref, g_ref, w_ref, cos_ref, sin_ref, u_ref, qkv_ref, *, n_seq_tiles):
    h = jnp.where(pl.program_id(0) < n_seq_tiles, x_ref[...], tail_ref[...])
    u = _rms(h, g_ref[...]).astype(BF16)
    u_ref[...] = u
    p = jnp.dot(u, w_ref[...], preferred_element_type=F32)
    cos = cos_ref[...]
    sin = sin_ref[...]
    lane = lax.broadcasted_iota(I32, cos.shape, 1)
    first_half = (lane % HEAD_DIM) < (HEAD_DIM // 2)
    n_q = Q_W // LANES
    for c in range(n_q + 2):
        xs = p[:, c * LANES:(c + 1) * LANES]
        swapped = jnp.where(first_half, pltpu.roll(xs, LANES - HEAD_DIM // 2, 1),
                            pltpu.roll(xs, HEAD_DIM // 2, 1))
        r = xs * cos + swapped * sin
        if c < n_q:
            r = r * (HEAD_DIM ** -0.5)
        qkv_ref[:, c * LANES:(c + 1) * LANES] = r.astype(BF16)
    v0 = (n_q + 2) * LANES
    qkv_ref[:, v0:] = p[:, v0:].astype(BF16)


def _norm_qkv(x2d, tail_rows, g, w_qkv, cos_t, sin_t, tiles_per_seq):
    NS, D = x2d.shape
    R = NS + tail_rows.shape[0]
    W = w_qkv.shape[1]
    n_seq_tiles = NS // ROW_TILE

    def tab_map(i):
        return (jnp.where(i < n_seq_tiles, i % tiles_per_seq, tiles_per_seq), 0)

    return pl.pallas_call(
        functools.partial(_norm_qkv_kernel, n_seq_tiles=n_seq_tiles),
        grid=(R // ROW_TILE,),
        in_specs=[
            pl.BlockSpec((ROW_TILE, D), lambda i: (jnp.minimum(i, n_seq_tiles - 1), 0)),
            pl.BlockSpec((ROW_TILE, D), lambda i: (0, 0)),
            pl.BlockSpec((1, D), lambda i: (0, 0)),
            pl.BlockSpec((D, W), lambda i: (0, 0)),
            pl.BlockSpec((ROW_TILE, LANES), tab_map),
            pl.BlockSpec((ROW_TILE, LANES), tab_map),
        ],
        out_specs=[
            pl.BlockSpec((ROW_TILE, D), lambda i: (i, 0)),
            pl.BlockSpec((ROW_TILE, W), lambda i: (i, 0)),
        ],
        out_shape=[jax.ShapeDtypeStruct((R, D), BF16), jax.ShapeDtypeStruct((R, W), BF16)],
        compiler_params=_cparams(("parallel",), VMEM_LIMIT),
        name="norm_qkv",
    )(x2d, tail_rows, g, w_qkv, cos_t, sin_t)


def _matmul_kernel(x_ref, w_ref, o_ref):
    o_ref[...] = jnp.dot(x_ref[...], w_ref[...], preferred_element_type=F32).astype(o_ref.dtype)


def _proj_rest(u, w):
    R, D = u.shape
    N = w.shape[1]
    tn = 2048
    tm = ROW_TILE + ROW_TILE // 2
    if R % tm:
        tm = ROW_TILE
    return pl.pallas_call(
        _matmul_kernel,
        grid=(N // tn, R // tm),
        in_specs=[pl.BlockSpec((tm, D), lambda j, i: (i, 0)),
                  pl.BlockSpec((D, tn), lambda j, i: (0, j))],
        out_specs=pl.BlockSpec((tm, tn), lambda j, i: (i, j)),
        out_shape=jax.ShapeDtypeStruct((R, N), F32),
        compiler_params=_cparams(("parallel", "parallel"), VMEM_LIMIT),
        name="proj_rest",
    )(u, w)


def _attn_core(q_ref, k_groups, v_groups, mask_fn, sink_ref, write):
    nkeys = k_groups[0].shape[0]
    rows = q_ref.shape[0]
    heads_per_group = N_Q_HEADS // N_KV_HEADS
    width = heads_per_group * rows
    lo_q = lax.broadcasted_iota(I32, (rows, LANES), 1) < HEAD_DIM
    ki = lax.broadcasted_iota(I32, (nkeys, width), 0)
    qi = lax.broadcasted_iota(I32, (nkeys, width), 1) % rows
    mask = mask_fn(ki, qi)
    top_half = lax.broadcasted_iota(I32, (LANES, rows), 0) < HEAD_DIM
    for g in range(N_KV_HEADS):
        parts = []
        for j in range(heads_per_group // 2):
            qp = q_ref[:, (heads_per_group // 2 * g + j) * LANES:(heads_per_group // 2 * g + j + 1) * LANES]
            zero_q = jnp.zeros_like(qp)
            parts += [jnp.where(lo_q, qp, zero_q), jnp.where(lo_q, zero_q, qp)]
        q_stack = jnp.concatenate(parts, axis=0)
        s = lax.dot_general(k_groups[g], q_stack, (((1,), (1,)), ((), ())),
                            preferred_element_type=F32)
        s = jnp.where(mask, s, NEG_INF)
        sk = sink_ref[:, g * width:(g + 1) * width]
        m = jnp.maximum(jnp.max(s, axis=0, keepdims=True), sk)
        e = jnp.exp(s - m)
        den = jnp.sum(e, axis=0, keepdims=True) + jnp.exp(sk - m)
        prob = (e * (1.0 / den)).astype(BF16)
        o_t = lax.dot_general(v_groups[g], prob, (((0,), (0,)), ((), ())), preferred_element_type=F32)
        for j in range(heads_per_group // 2):
            even = o_t[:, (2 * j) * rows:(2 * j + 1) * rows]
            odd = o_t[:, (2 * j + 1) * rows:(2 * j + 2) * rows]
            pair_t = jnp.where(top_half, even, odd)
            write(heads_per_group // 2 * g + j, pair_t.T)


def _attn_seq_kernel(sink_ref, q_ref, kc0, kc1, vc0, vc1, kp0, kp1, vp0, vp1,
                     km0, km1, vm0, vm1, o_ref):
    n = pl.program_id(1)
    metas = ((km0, vm0), (km1, vm1))
    curs = ((kc0, vc0), (kc1, vc1))
    prevs = ((kp0, vp0), (kp1, vp1))
    first, second = slice(0, ATTN_BLK), slice(ATTN_BLK, 2 * ATTN_BLK)
    for sub in range(2):
        no_prev = jnp.where(n > 0, 0, 2 * ATTN_BLK) if sub == 0 else 0

        def mask_fn(ki, qi, no_prev=no_prev):
            prev_ok = ((ki >= N_META_TOK) & (ki < N_META_TOK + ATTN_BLK)
                       & (ki - N_META_TOK > qi + no_prev))
            cur_ok = (ki >= N_META_TOK + ATTN_BLK) & (ki - (N_META_TOK + ATTN_BLK) <= qi)
            return (ki < N_META_TOK) | prev_ok | cur_ok

        k_groups, v_groups = [], []
        for g in range(N_KV_HEADS):
            before = ((prevs[g][0][...], prevs[g][1][...]) if sub == 0
                      else (curs[g][0][first, :], curs[g][1][first, :]))
            own = first if sub == 0 else second
            k_groups.append(jnp.concatenate([metas[g][0][...], before[0], curs[g][0][own, :]], axis=0))
            v_groups.append(jnp.concatenate([metas[g][1][...], before[1], curs[g][1][own, :]], axis=0))

        def write(pr, acc, sub=sub):
            o_ref[sub * ATTN_BLK:(sub + 1) * ATTN_BLK, pr * LANES:(pr + 1) * LANES] = acc.astype(o_ref.dtype)

        _attn_core(q_ref.at[pl.ds(sub * ATTN_BLK, ATTN_BLK), :], k_groups, v_groups, mask_fn,
                   sink_ref, write)


def _attn_meta_kernel(sink_ref, q_ref, k0, k1, v0, v1, o_ref):
    nm = q_ref.shape[0]

    def mask_fn(ki, qi):
        return ((qi // N_META_TOK) == (ki // N_META_TOK)) & ((ki % N_META_TOK) <= (qi % N_META_TOK))

    o_ref[...] = jnp.zeros_like(o_ref)

    def write(pr, acc):
        o_ref[0:nm, pr * LANES:(pr + 1) * LANES] = acc.astype(o_ref.dtype)

    _attn_core(q_ref, [k0[...], k1[...]], [v0[...], v1[...]], mask_fn, sink_ref, write)


def _attention(qkv, sinks, B, S):
    R = qkv.shape[0]
    NS = B * S
    NM = B * N_META_TOK
    nb = S // ATTN_BLK
    qc = Q_W // LANES
    sink_seq = jnp.repeat(sinks.astype(F32), ATTN_BLK).reshape(1, N_Q_HEADS * ATTN_BLK)
    sink_meta = jnp.repeat(sinks.astype(F32), NM).reshape(1, N_Q_HEADS * NM)

    pair = 2 * ATTN_BLK
    npair = S // pair

    def kv_spec(col, prev):
        if prev:
            return pl.BlockSpec((ATTN_BLK, LANES), lambda b, n: (b * nb + jnp.maximum(2 * n - 1, 0), col))
        return pl.BlockSpec((pair, LANES), lambda b, n: (b * npair + n, col))

    def meta_spec(col):
        return pl.BlockSpec((N_META_TOK, LANES), lambda b, n: (NS // N_META_TOK + b, col))

    in_specs = [pl.BlockSpec(sink_seq.shape, lambda b, n: (0, 0)),
                pl.BlockSpec((pair, Q_W), lambda b, n: (b * npair + n, 0))]
    in_specs += [kv_spec(qc, False), kv_spec(qc + 1, False), kv_spec(qc + 2, False), kv_spec(qc + 3, False)]
    in_specs += [kv_spec(qc, True), kv_spec(qc + 1, True), kv_spec(qc + 2, True), kv_spec(qc + 3, True)]
    in_specs += [meta_spec(qc), meta_spec(qc + 1), meta_spec(qc + 2), meta_spec(qc + 3)]
    attn = pl.pallas_call(
        _attn_seq_kernel,
        grid=(B, npair),
        in_specs=in_specs,
        out_specs=pl.BlockSpec((pair, Q_W), lambda b, n: (b * npair + n, 0)),
        out_shape=jax.ShapeDtypeStruct((NS, Q_W), BF16),
        compiler_params=_cparams(("parallel", "parallel"), VMEM_LIMIT),
        name="attn_seq",
    )(sink_seq, *([qkv] * 13))

    mb = NS // NM
    tail = R - NS
    blk = lambda col: pl.BlockSpec((NM, LANES), lambda i: (mb, col))
    attn_tail = pl.pallas_call(
        _attn_meta_kernel,
        grid=(1,),
        in_specs=[pl.BlockSpec(sink_meta.shape, lambda i: (0, 0)),
                  pl.BlockSpec((NM, Q_W), lambda i: (mb, 0)),
                  blk(qc), blk(qc + 1), blk(qc + 2), blk(qc + 3)],
        out_specs=pl.BlockSpec((tail, Q_W), lambda i: (0, 0)),
        out_shape=jax.ShapeDtypeStruct((tail, Q_W), BF16),
        compiler_params=_cparams(("arbitrary",), VMEM_LIMIT),
        name="attn_meta",
    )(sink_meta, qkv, qkv, qkv, qkv, qkv)
    return attn, attn_tail


def _hgrn_prep_kernel(rq_ref, rf_ref, ri_ref, lbl_ref, qd_ref, kd_ref, ke_ref, v_ref, eb_ref):
    lbl = lbl_ref[...]
    e = jnp.exp(lbl - jnp.max(lbl, axis=0, keepdims=True))
    lb = e[0:1] / jnp.sum(e, axis=0, keepdims=True)
    f = lb + (1.0 - lb) * jax.nn.sigmoid(rf_ref[...])
    logf = jnp.log(f)
    k = 1.0 - f
    tm = logf.shape[0]
    ri = lax.broadcasted_iota(I32, (tm, tm), 0)
    ci = lax.broadcasted_iota(I32, (tm, tm), 1)
    same_chunk = (ri // REC_CHUNK) == (ci // REC_CHUNK)
    sel = jnp.concatenate([jnp.where(same_chunk & (ci <= ri), 1.0, 0.0).astype(BF16),
                           jnp.where(same_chunk, 1.0, 0.0).astype(BF16)], axis=0)
    hi = logf.astype(BF16)
    rem = logf - hi.astype(F32)
    mid = rem.astype(BF16)
    lo = (rem - mid.astype(F32)).astype(BF16)
    sums = (jnp.dot(sel, hi, preferred_element_type=F32) + jnp.dot(sel, mid, preferred_element_type=F32)
            + jnp.dot(sel, lo, preferred_element_type=F32))
    b = sums[:tm]
    total = sums[tm:]
    eb = jnp.exp(b)
    eb_ref[...] = eb
    qd_ref[...] = (rq_ref[...] * eb).astype(BF16)
    kd_ref[...] = (k * jnp.exp(-b)).astype(BF16)
    ke_ref[...] = (k * jnp.exp(total - b)).astype(BF16)
    v_ref[...] = ri_ref[...].astype(BF16)


def _hgrn_prep(rest, lbl):
    R = rest.shape[0]
    W = REC_W
    col = lambda c: pl.BlockSpec((SEG, W), lambda i: (i, c))
    row_out = pl.BlockSpec((SEG, W), lambda i: (i, 0))
    return pl.pallas_call(
        _hgrn_prep_kernel,
        grid=(R // SEG,),
        in_specs=[col(0), col(1), col(2), pl.BlockSpec(lbl.shape, lambda i: (0, 0))],
        out_specs=[row_out] * 5,
        out_shape=[jax.ShapeDtypeStruct((R, W), BF16)] * 4 + [jax.ShapeDtypeStruct((R, W), F32)],
        compiler_params=_cparams(("parallel",), VMEM_LIMIT),
        name="hgrn_prep",
    )(rest, rest, rest, lbl)


def _hgrn_intra(qd, kd, v):
    n = qd.shape[0]
    ri = lax.broadcasted_iota(I32, (n, n), 0)
    ci = lax.broadcasted_iota(I32, (n, n), 1)
    keep = ((ri // REC_CHUNK) == (ci // REC_CHUNK)) & (ri >= ci)
    sc = lax.dot_general(qd, kd, (((1,), (1,)), ((), ())), preferred_element_type=F32)
    sc = jnp.where(keep, sc, 0.0)
    return jnp.dot(sc.astype(BF16), v, preferred_element_type=F32)


def _hgrn_kv_t(v, ke):
    return lax.dot_general(v, ke, (((0,), (0,)), ((), ())), preferred_element_type=F32)


def _hgrn_finish(o, g, norm_g):
    y = o * lax.rsqrt(jnp.mean(o * o, axis=-1, keepdims=True) + NORM_EPS) * norm_g
    return (y * (g * jax.nn.sigmoid(g))).astype(BF16)


def _hgrn_meta_kernel(qd_ref, kd_ref, ke_ref, v_ref, rg_ref, ng_ref, st_ref, rec_ref, *, nbatch):
    rec_ref[...] = jnp.zeros_like(rec_ref)
    nm = nbatch * REC_CHUNK
    for h in range(N_REC_HEADS):
        cols = slice(h * REC_DIM, (h + 1) * REC_DIM)
        o = _hgrn_intra(qd_ref[0:nm, cols], kd_ref[0:nm, cols], v_ref[0:nm, cols])
        rec_ref[0:nm, cols] = _hgrn_finish(o, rg_ref[0:nm, cols], ng_ref[...])
        for b in range(nbatch):
            rows = slice(b * REC_CHUNK, (b + 1) * REC_CHUNK)
            st_ref[b, h] = _hgrn_kv_t(v_ref[rows, cols], ke_ref[rows, cols])


def _hgrn_scan_kernel(*refs, gb):
    qd, kd, ke, v, eb, rg = (refs[i * gb:(i + 1) * gb] for i in range(6))
    ng_ref, st0_ref, rec_ref, st_scr, o_scr = refs[6 * gb:]
    seg = rec_ref.shape[1]
    heads = [slice(h * REC_DIM, (h + 1) * REC_DIM) for h in range(N_REC_HEADS)]

    @pl.when(pl.program_id(1) == 0)
    def _():
        st_scr[...] = st0_ref[...]

    for i in range(gb):
        for cols in heads:
            o_scr[i, :, cols] = _hgrn_intra(qd[i][:, cols], kd[i][:, cols], v[i][:, cols])

    def chunk(c, carry):
        r0 = pl.multiple_of(c * REC_CHUNK, REC_CHUNK)
        rows = pl.ds(r0, REC_CHUNK)
        for i in range(gb):
            for h, cols in enumerate(heads):
                last8 = eb[i][pl.ds(pl.multiple_of(r0 + REC_CHUNK - 8, 8), 8), cols]
                dec = last8[7:8]
                st = st_scr[i, h]
                o_scr[i, rows, cols] += lax.dot_general(qd[i][rows, cols], st.astype(BF16),
                                                        (((1,), (1,)), ((), ())),
                                                        preferred_element_type=F32)
                st_scr[i, h] = st * dec + _hgrn_kv_t(v[i][rows, cols], ke[i][rows, cols])
        return carry

    lax.fori_loop(0, seg // REC_CHUNK, chunk, 0, unroll=True)
    for i in range(gb):
        for cols in heads:
            rec_ref[i, :, cols] = _hgrn_finish(o_scr[i, :, cols], rg[i][:, cols], ng_ref[...])


def _hgrn(rest, lbl, norm_g, B, S):
    R = rest.shape[0]
    NS = B * S
    W = REC_W
    tail = R - NS
    tb = NS // tail
    qd, kd, ke, v, eb = _hgrn_prep(rest, lbl)

    tail_spec = pl.BlockSpec((tail, W), lambda i: (tb, 0))
    state, rec_tail = pl.pallas_call(
        functools.partial(_hgrn_meta_kernel, nbatch=B),
        grid=(1,),
        in_specs=[tail_spec, tail_spec, tail_spec, tail_spec,
                  pl.BlockSpec((tail, W), lambda i: (tb, 3)),
                  pl.BlockSpec((1, REC_DIM), lambda i: (0, 0))],
        out_specs=[pl.BlockSpec((B, N_REC_HEADS, REC_DIM, REC_DIM), lambda i: (0, 0, 0, 0)),
                   pl.BlockSpec((tail, W), lambda i: (0, 0))],
        out_shape=[jax.ShapeDtypeStruct((B, N_REC_HEADS, REC_DIM, REC_DIM), F32),
                   jax.ShapeDtypeStruct((tail, W), BF16)],
        compiler_params=_cparams(("arbitrary",), VMEM_LIMIT),
        name="hgrn_meta",
    )(qd, kd, ke, v, rest, norm_g)

    gb = SCAN_BATCHES
    ns = S // SCAN_SEG

    def seg_specs(col):
        return [pl.BlockSpec((SCAN_SEG, W), lambda g, s, i=i: ((g * gb + i) * ns + s, col))
                for i in range(gb)]

    rec = pl.pallas_call(
        functools.partial(_hgrn_scan_kernel, gb=gb),
        grid=(B // gb, ns),
        in_specs=seg_specs(0) * 5 + seg_specs(3)
        + [pl.BlockSpec((1, REC_DIM), lambda g, s: (0, 0)),
           pl.BlockSpec((gb, N_REC_HEADS, REC_DIM, REC_DIM), lambda g, s: (g, 0, 0, 0))],
        out_specs=pl.BlockSpec((gb, SCAN_SEG, W), lambda g, s: (g, s, 0)),
        out_shape=jax.ShapeDtypeStruct((B, S, W), BF16),
        scratch_shapes=[pltpu.VMEM((gb, N_REC_HEADS, REC_DIM, REC_DIM), F32),
                        pltpu.VMEM((gb, SCAN_SEG, W), F32)],
        compiler_params=_cparams(("parallel", "arbitrary"), VMEM_LIMIT),
        name="hgrn_scan",
    )(*([qd] * gb + [kd] * gb + [ke] * gb + [v] * gb + [eb] * gb + [rest] * gb), norm_g, state)
    return rec.reshape(NS, W), rec_tail


def _mix_kernel(attn_ref, attn_t_ref, rec_ref, rec_t_ref, ga_ref, gr_ref, x_ref, tail_ref,
                wa_ref, wr_ref, wo_ref, g_ref, wrt_ref, brt_ref, h2_ref, xn_ref, lg_ref, *, n_seq_tiles):
    is_seq = pl.program_id(0) < n_seq_tiles
    attn = jnp.where(is_seq, attn_ref[...], attn_t_ref[...])
    rec = jnp.where(is_seq, rec_ref[...], rec_t_ref[...])
    h = jnp.where(is_seq, x_ref[...], tail_ref[...])
    a = jnp.dot(attn, wa_ref[...], preferred_element_type=F32)
    r = jnp.dot(rec, wr_ref[...], preferred_element_type=F32)
    mixed = jax.nn.sigmoid(ga_ref[...]) * a + jax.nn.sigmoid(gr_ref[...]) * r
    h2 = h + jnp.dot(mixed.astype(BF16), wo_ref[...], preferred_element_type=F32)
    h2_ref[...] = h2
    xn = _rms(h2, g_ref[...])
    xn_ref[...] = xn
    xn_hi = xn.astype(BF16)
    xn_lo = (xn - xn_hi.astype(F32)).astype(BF16)
    w2 = wrt_ref[...]
    p_hi = jnp.dot(xn_hi, w2, preferred_element_type=F32)
    p_lo = jnp.dot(xn_lo, w2[:, :LANES], preferred_element_type=F32)
    lg_ref[...] = p_hi[:, :LANES] + p_hi[:, LANES:] + p_lo + brt_ref[...]


def _mix(attn, attn_tail, rec, rec_tail, rest, x2d, tail_rows, wa, wr, wo, g, w_router, b_router):
    NS, D = x2d.shape
    R = NS + tail_rows.shape[0]
    tm = MIX_TILE
    nst = NS // tm
    const = lambda shape: pl.BlockSpec(shape, lambda i: (0, 0), pipeline_mode=pl.Buffered(1))
    row = lambda w: pl.BlockSpec((tm, w), lambda i: (i, 0))
    seq = lambda w: pl.BlockSpec((tm, w), lambda i: (jnp.minimum(i, nst - 1), 0))
    tl = lambda w: pl.BlockSpec((tm, w), lambda i: (jnp.maximum(i - nst, 0), 0))
    return pl.pallas_call(
        functools.partial(_mix_kernel, n_seq_tiles=nst),
        grid=(R // tm,),
        in_specs=[seq(Q_W), tl(Q_W), seq(REC_W), tl(REC_W),
                  pl.BlockSpec((tm, D), lambda i: (i, 2)), pl.BlockSpec((tm, D), lambda i: (i, 3)),
                  seq(D), tl(D), const(wa.shape), const(wr.shape), const(wo.shape), const(g.shape),
                  const(w_router.shape), const(b_router.shape)],
        out_specs=[row(D), row(D), row(LANES)],
        out_shape=[jax.ShapeDtypeStruct((R, D), F32), jax.ShapeDtypeStruct((R, D), F32),
                   jax.ShapeDtypeStruct((R, LANES), F32)],
        compiler_params=_cparams(("parallel",), VMEM_LIMIT),
        name="mix_outproj",
    )(attn, attn_tail, rec, rec_tail, rest, rest, x2d, tail_rows, wa, wr, wo, g, w_router, b_router)


def _route_kernel(lg_ref, dest_ref, gate_ref, cnt_ref, cnt_scr, carry_scr, start_scr, *, n_tok):
    ph = pl.program_id(0)
    i = pl.program_id(1)
    tm = lg_ref.shape[0]

    @pl.when((ph == 0) & (i == 0))
    def _():
        cnt_scr[...] = jnp.zeros_like(cnt_scr)

    lane = lax.broadcasted_iota(I32, (tm, LANES), 1)
    valid = (i * tm + lax.broadcasted_iota(I32, (tm, LANES), 0)) < n_tok
    work = lg_ref[...]
    onehots, vals = [], []
    for _ in range(TOP_K):
        m = jnp.max(work, axis=-1, keepdims=True)
        idx = jnp.min(jnp.where(work == m, lane, LANES), axis=-1, keepdims=True)
        oh = lane == idx
        onehots.append(oh)
        vals.append(m)
        work = jnp.where(oh, -jnp.inf, work)
    multi = jnp.zeros((tm, LANES), F32)
    for oh in onehots:
        multi = multi + jnp.where(oh & valid, 1.0, 0.0)
    tile_cnt = jnp.sum(multi, axis=0, keepdims=True)

    @pl.when(ph == 0)
    def _():
        cnt_scr[...] += tile_cnt

    @pl.when(ph == 1)
    def _():
        @pl.when(i == 0)
        def _():
            c = cnt_scr[...]
            padded = jnp.ceil(c * (1.0 / MOE_BLK)) * MOE_BLK
            before = (lax.broadcasted_iota(I32, (LANES, LANES), 0)
                      < lax.broadcasted_iota(I32, (LANES, LANES), 1))
            start = jnp.dot(jnp.broadcast_to(padded, (8, LANES)), jnp.where(before, 1.0, 0.0),
                            preferred_element_type=F32, precision=lax.Precision.HIGHEST)
            start_scr[...] = start[0:1]
            carry_scr[...] = jnp.zeros_like(carry_scr)
            cnt_ref[...] = c

        earlier = (lax.broadcasted_iota(I32, (tm, tm), 1) < lax.broadcasted_iota(I32, (tm, tm), 0))
        prefix = jnp.dot(jnp.where(earlier, 1.0, 0.0).astype(BF16), multi.astype(BF16),
                         preferred_element_type=F32)
        base = prefix + carry_scr[...] + start_scr[...]
        den = jnp.zeros_like(vals[0])
        for v in vals:
            den = den + jnp.exp(v - vals[0])
        dest = jnp.zeros((tm, LANES), F32)
        gate = jnp.zeros((tm, LANES), F32)
        for k in range(TOP_K):
            d_k = jnp.sum(jnp.where(onehots[k], base, 0.0), axis=-1, keepdims=True)
            dest = jnp.where(lane == k, d_k, dest)
            gate = jnp.where(lane == k, jnp.exp(vals[k] - vals[0]) / den, gate)
        dest_ref[...] = dest.astype(I32)
        gate_ref[...] = gate
        carry_scr[...] += tile_cnt


def _route(logits, n_tok):
    R = logits.shape[0]
    tm = ROW_TILE
    blk = pl.BlockSpec((tm, LANES), lambda p, i: (i * p, 0))
    return pl.pallas_call(
        functools.partial(_route_kernel, n_tok=n_tok),
        grid=(2, R // tm),
        in_specs=[pl.BlockSpec((tm, LANES), lambda p, i: (i, 0))],
        out_specs=[blk, blk, pl.BlockSpec((1, LANES), lambda p, i: (0, 0))],
        out_shape=[jax.ShapeDtypeStruct((R, LANES), I32), jax.ShapeDtypeStruct((R, LANES), F32),
                   jax.ShapeDtypeStruct((1, LANES), F32)],
        scratch_shapes=[pltpu.VMEM((1, LANES), F32)] * 3,
        compiler_params=_cparams(("arbitrary", "arbitrary"), VMEM_LIMIT),
        name="route",
    )(logits)


def _dispatch_kernel(pad_lo_ref, pad_len_ref, nu_ref, dest_ref, xn_ref, xs_ref, zero_scr, sem, *, n_blk):
    i = pl.program_id(0)
    last = pl.num_programs(0) - 1

    @pl.when(i < last)
    def _():
        def one(r):
            for k in range(TOP_K):
                pltpu.make_async_copy(xn_ref.at[pl.ds(r, 1), :],
                                      xs_ref.at[pl.ds(dest_ref[r * TOP_K + k], 1), :], sem).start()

        _for_rows(TOK_TILE, one, True)
        for _ in range(TOP_K):
            pltpu.make_async_copy(xn_ref, xs_ref.at[pl.ds(0, TOK_TILE), :], sem).wait()

    @pl.when(i == last)
    def _():
        zero_scr[...] = jnp.zeros_like(zero_scr)

        def block_copy(b):
            rows = pl.ds(pl.multiple_of(b * MOE_BLK, MOE_BLK), MOE_BLK)
            return pltpu.make_async_copy(zero_scr, xs_ref.at[rows, :], sem)

        def sweep(act):
            def blocks(b, carry):
                act(block_copy(b))
                return carry

            lax.fori_loop(nu_ref[0], n_blk, blocks, 0)

            def experts(e, carry):
                lo = pad_lo_ref[e]
                n_pad = pad_len_ref[e]
                head = jnp.minimum((-lo) & 7, n_pad)

                def rows(r, c):
                    act(pltpu.make_async_copy(zero_scr.at[pl.ds(0, 1), :],
                                              xs_ref.at[pl.ds(lo + r, 1), :], sem))
                    return c

                lax.fori_loop(0, head, rows, 0)
                off = lo + head
                left = n_pad - head
                size = MOE_BLK // 2
                while size >= 8:
                    @pl.when((left & size) != 0)
                    def _(off=off, size=size):
                        act(pltpu.make_async_copy(zero_scr.at[pl.ds(0, size), :],
                                                  xs_ref.at[pl.ds(pl.multiple_of(off, 8), size), :], sem))

                    off = off + (left & size)
                    size //= 2
                return carry

            lax.fori_loop(0, pad_lo_ref.shape[0], experts, 0)

        sweep(lambda cp: cp.start())
        sweep(lambda cp: cp.wait())


def _dispatch(xn, dest_flat, cnt, n_tok, n_blk):
    D = xn.shape[1]
    n_tiles = n_tok // TOK_TILE
    padded = ((cnt + MOE_BLK - 1) // MOE_BLK) * MOE_BLK
    pad_end = jnp.cumsum(padded)
    pad_lo = (pad_end - padded + cnt).astype(I32)
    pad_len = (padded - cnt).astype(I32)
    n_used = (pad_end[-1:] // MOE_BLK).astype(I32)
    tile = lambda i, *_: (jnp.minimum(i, n_tiles - 1),)
    return pl.pallas_call(
        functools.partial(_dispatch_kernel, n_blk=n_blk),
        grid_spec=pltpu.PrefetchScalarGridSpec(
            num_scalar_prefetch=3,
            grid=(n_tiles + 1,),
            in_specs=[pl.BlockSpec((TOK_TILE * TOP_K,), tile, memory_space=pltpu.SMEM),
                      pl.BlockSpec((TOK_TILE, D), lambda i, *_: (jnp.minimum(i, n_tiles - 1), 0))],
            out_specs=pl.BlockSpec(memory_space=pl.ANY),
            scratch_shapes=[pltpu.VMEM((MOE_BLK, D), F32), pltpu.SemaphoreType.DMA],
        ),
        out_shape=jax.ShapeDtypeStruct((n_blk * MOE_BLK, D), F32),
        compiler_params=_cparams(("arbitrary",), VMEM_LIMIT),
        name="dispatch",
    )(pad_lo, pad_len, n_used, dest_flat, xn)


class _WeightStream:
    def __init__(self, w_refs, stage, wbuf, sems, unit_rows):
        self.w_refs, self.stage, self.wbuf, self.sems = w_refs, stage, wbuf, sems
        self.unit_rows = unit_rows
        self.n_units = w_refs[0].shape[1] // unit_rows
        self.n_stage = stage.shape[0]

    def _copy(self, e, u, m):
        rows = pl.ds(pl.multiple_of(u * self.unit_rows, self.unit_rows), self.unit_rows)
        st = u % self.n_stage
        return pltpu.make_async_copy(self.w_refs[m].at[e, rows, :], self.stage.at[st, m],
                                     self.sems.at[st, m])

    def start(self, e, u):
        for m in range(len(self.w_refs)):
            self._copy(e, u, m).start(priority=1)

    def finish(self, e, u, slot):
        rows = pl.ds(pl.multiple_of(u * self.unit_rows, self.unit_rows), self.unit_rows)
        for m in range(len(self.w_refs)):
            self._copy(e, u, m).wait()
            self.wbuf[slot, m, rows, :] = self.stage[u % self.n_stage, m].astype(BF16)

    def prime(self, e):
        for u in range(self.n_stage):
            self.start(e, u)

    def convert(self, e, slot, lo, hi):
        def body(u, carry):
            self.finish(e, u, slot)

            @pl.when(u + self.n_stage < self.n_units)
            def _():
                self.start(e, u + self.n_stage)

            return carry

        lax.fori_loop(lo, hi, body, 0)


def _moe_step(plan, stream, compute, out_ref):
    be_ref, nxt_ref, slot_ref, first_ref, ulo_ref, uhi_ref, nu_ref = plan
    blk = pl.program_id(0)

    @pl.when(blk >= nu_ref[0])
    def _():
        out_ref[...] = jnp.zeros_like(out_ref)

    @pl.when(blk < nu_ref[0])
    def _():
        e = be_ref[blk]
        ne = nxt_ref[blk]
        slot = slot_ref[blk]

        @pl.when(blk == 0)
        def _():
            stream.prime(e)
            stream.convert(e, slot, 0, stream.n_units)

        @pl.when((first_ref[blk] == 1) & (ne >= 0))
        def _():
            stream.prime(ne)

        compute(slot, e)

        @pl.when(ne >= 0)
        def _():
            stream.convert(ne, 1 - slot, ulo_ref[blk], uhi_ref[blk])


def _row_variants(n_valid, fn):
    half = MOE_BLK // 2

    @pl.when(n_valid > half)
    def _():
        fn(MOE_BLK)

    @pl.when(n_valid <= half)
    def _():
        fn(half)


def _moe_up_kernel(be_ref, nxt_ref, slot_ref, first_ref, ulo_ref, uhi_ref, nu_ref, nval_ref,
                   xs_ref, wg_ref, wu_ref, bg_ref, bu_ref, act_ref, wbuf, stage, sems):
    stream = _WeightStream((wg_ref, wu_ref), stage, wbuf, sems, MOE_UNIT)
    half = act_ref.shape[1] // 2

    def compute(slot, e):
        def rows_fn(rows):
            xb = xs_ref[0:rows, :].astype(BF16)
            for c in range(2):
                cols = slice(c * half, (c + 1) * half)
                gate = jnp.dot(xb, wbuf[slot, 0, :, cols], preferred_element_type=F32) + bg_ref[0][:, cols]
                up = jnp.dot(xb, wbuf[slot, 1, :, cols], preferred_element_type=F32) + bu_ref[0][:, cols]
                gate = jnp.minimum(gate, SWIGLU_LIMIT)
                up = jnp.clip(up, -SWIGLU_LIMIT, SWIGLU_LIMIT)
                act = gate * jax.nn.sigmoid(SWIGLU_ALPHA * gate) * (up + 1.0)
                act_ref[0:rows, cols] = act.astype(act_ref.dtype)
            if rows < MOE_BLK:
                act_ref[rows:, :] = jnp.zeros((MOE_BLK - rows, act_ref.shape[1]), act_ref.dtype)

        _row_variants(nval_ref[pl.program_id(0)], rows_fn)

    _moe_step((be_ref, nxt_ref, slot_ref, first_ref, ulo_ref, uhi_ref, nu_ref), stream, compute, act_ref)


def _moe_down_kernel(be_ref, nxt_ref, slot_ref, first_ref, ulo_ref, uhi_ref, nu_ref, nval_ref,
                     act_ref, wd_ref, bd_ref, y_ref, wbuf, stage, sems):
    stream = _WeightStream((wd_ref,), stage, wbuf, sems, MOE_UNIT)

    def compute(slot, e):
        def rows_fn(rows):
            y_ref[0:rows, :] = (jnp.dot(act_ref[0:rows, :], wbuf[slot, 0], preferred_element_type=F32)
                                + bd_ref[0])
            if rows < MOE_BLK:
                y_ref[rows:, :] = jnp.zeros((MOE_BLK - rows, y_ref.shape[1]), y_ref.dtype)

        _row_variants(nval_ref[pl.program_id(0)], rows_fn)

    _moe_step((be_ref, nxt_ref, slot_ref, first_ref, ulo_ref, uhi_ref, nu_ref), stream, compute, y_ref)


def _moe_plan(cnt, n_blk, n_units):
    E = cnt.shape[0]
    nblk_e = (cnt + MOE_BLK - 1) // MOE_BLK
    blk_end = jnp.cumsum(nblk_e)
    blk_first = blk_end - nblk_e
    j = jnp.arange(n_blk, dtype=I32)
    be = jnp.minimum(jnp.sum((blk_end[None, :] <= j[:, None]).astype(I32), axis=1), E - 1)
    ids = jnp.arange(E, dtype=I32)
    own = be[:, None] == ids[None, :]

    def per_block(table):
        return jnp.sum(jnp.where(own, table[None, :], 0), axis=1)

    k = j - per_block(blk_first)
    nb = jnp.maximum(per_block(nblk_e), 1)
    nonempty = nblk_e > 0
    later = (ids[None, :] > ids[:, None]) & nonempty[None, :]
    nxt_e = jnp.min(jnp.where(later, ids[None, :], E), axis=1)
    nxt_e = jnp.where(nxt_e >= E, -1, nxt_e)
    slot_e = (jnp.cumsum(nonempty.astype(I32)) - 1) % 2
    n_valid = jnp.clip(per_block(cnt) - k * MOE_BLK, 0, MOE_BLK)
    plan = (be, per_block(nxt_e), per_block(slot_e), (k == 0).astype(I32),
            (k * n_units) // nb, ((k + 1) * n_units) // nb, blk_end[-1:].astype(I32), n_valid)
    return tuple(a.astype(I32) for a in plan)


def _moe(xs, cnt, w_gate, b_gate, w_up, b_up, w_down, b_down):
    P, D = xs.shape
    E, _, F = w_gate.shape
    n_blk = P // MOE_BLK
    n_stage = MOE_STAGES
    clamp = lambda blk, nu: jnp.maximum(jnp.minimum(blk, nu[0] - 1), 0)
    row_map = lambda blk, be, nx, sl, fi, lo, hi, nu, nv: (clamp(blk, nu), 0)
    out_map = lambda blk, *plan: (blk, 0)
    hbm = pl.BlockSpec(memory_space=pl.ANY)

    def bias_spec(width):
        return pl.BlockSpec((1, 1, width), lambda blk, *plan: (plan[0][clamp(blk, plan[6])], 0, 0))

    assert D == F, "one streaming plan serves all three expert matrices"
    plan = _moe_plan(cnt, n_blk, D // MOE_UNIT)
    act = pl.pallas_call(
        _moe_up_kernel,
        grid_spec=pltpu.PrefetchScalarGridSpec(
            num_scalar_prefetch=8,
            grid=(n_blk,),
            in_specs=[pl.BlockSpec((MOE_BLK, D), row_map), hbm, hbm, bias_spec(F), bias_spec(F)],
            out_specs=pl.BlockSpec((MOE_BLK, F), out_map),
            scratch_shapes=[pltpu.VMEM((2, 2, D, F), BF16),
                            pltpu.VMEM((n_stage, 2, MOE_UNIT, F), F32),
                            pltpu.SemaphoreType.DMA((n_stage, 2))],
        ),
        out_shape=jax.ShapeDtypeStruct((P, F), BF16),
        compiler_params=_cparams(("arbitrary",), VMEM_LIMIT),
        name="moe_up",
    )(*plan, xs, w_gate, w_up, b_gate.reshape(E, 1, F), b_up.reshape(E, 1, F))

    return pl.pallas_call(
        _moe_down_kernel,
        grid_spec=pltpu.PrefetchScalarGridSpec(
            num_scalar_prefetch=8,
            grid=(n_blk,),
            in_specs=[pl.BlockSpec((MOE_BLK, F), row_map), hbm, bias_spec(D)],
            out_specs=pl.BlockSpec((MOE_BLK, D), out_map),
            scratch_shapes=[pltpu.VMEM((2, 1, F, D), BF16),
                            pltpu.VMEM((n_stage, 1, MOE_UNIT, D), F32),
                            pltpu.SemaphoreType.DMA((n_stage, 1))],
        ),
        out_shape=jax.ShapeDtypeStruct((P, D), F32),
        compiler_params=_cparams(("arbitrary",), VMEM_LIMIT),
        name="moe_down",
    )(*plan, act, w_down, b_down.reshape(E, 1, D))


def _combine_kernel(dest_ref, dnext_ref, h2_ref, gate_ref, g_ref, y_ref, o_ref, ybuf, sem):
    i = pl.program_id(0)
    slot = i % 2

    def issue(dref, s, unrolled):
        def one(r):
            for k in range(TOP_K):
                pltpu.make_async_copy(y_ref.at[pl.ds(dref[r * TOP_K + k], 1), :],
                                      ybuf.at[s, k, pl.ds(r, 1), :], sem.at[s]).start()

        _for_rows(TOK_TILE, one, unrolled)

    @pl.when(i == 0)
    def _():
        issue(dest_ref, 0, False)

    for s in range(2):
        @pl.when((slot == s) & (i + 1 < pl.num_programs(0)))
        def _():
            issue(dnext_ref, 1 - s, True)

    for k in range(TOP_K):
        pltpu.make_async_copy(y_ref.at[pl.ds(0, TOK_TILE), :], ybuf.at[slot, k], sem.at[slot]).wait()
    gates = gate_ref[...]
    acc = h2_ref[...]
    for k in range(TOP_K):
        acc = acc + gates[:, k:k + 1] * ybuf[slot, k]
    o_ref[...] = _rms(acc, g_ref[...])


def _combine(h2, gates, dest_flat, y, g, n_seq):
    D = h2.shape[1]
    n_tiles = n_seq // TOK_TILE
    return pl.pallas_call(
        _combine_kernel,
        grid=(n_tiles,),
        in_specs=[pl.BlockSpec((TOK_TILE * TOP_K,), lambda i: (i,), memory_space=pltpu.SMEM),
                  pl.BlockSpec((TOK_TILE * TOP_K,), lambda i: (jnp.minimum(i + 1, n_tiles - 1),),
                               memory_space=pltpu.SMEM),
                  pl.BlockSpec((TOK_TILE, D), lambda i: (i, 0)),
                  pl.BlockSpec((TOK_TILE, LANES), lambda i: (i, 0)),
                  pl.BlockSpec((1, D), lambda i: (0, 0)),
                  pl.BlockSpec(memory_space=pl.ANY)],
        out_specs=pl.BlockSpec((TOK_TILE, D), lambda i: (i, 0)),
        out_shape=jax.ShapeDtypeStruct((n_seq, D), F32),
        scratch_shapes=[pltpu.VMEM((2, TOP_K, TOK_TILE, D), F32), pltpu.SemaphoreType.DMA((2,))],
        compiler_params=_cparams(("arbitrary",), VMEM_LIMIT),
        name="combine_norm",
    )(dest_flat, dest_flat, h2, gates, g, y)


def _rope_tables(S, n_meta_rows):
    half = HEAD_DIM // 2
    inv_freq = ROPE_THETA ** (-jnp.arange(half, dtype=F32) / half)
    lane = np.arange(LANES)
    fidx = lane % half
    sign = np.where((lane % HEAD_DIM) < half, -1.0, 1.0).astype(np.float32)
    pos_seq = N_META_TOK + jnp.arange(S)
    tail = jnp.arange(ROW_TILE)
    pos_tail = jnp.where(tail < n_meta_rows, tail % N_META_TOK, 0)
    pos = jnp.concatenate([pos_seq, pos_tail]).astype(F32)
    ang = (pos[:, None] * inv_freq[None, :])[:, fidx]
    return jnp.cos(ang), jnp.sin(ang) * sign[None, :]


def kernel(x, meta_tokens, attn_norm_g, w_in, attn_sinks, lower_bound_logits, rec_norm_g,
           w_attn_proj, w_rec_proj, w_out, ffn_norm_g, w_router, b_router, w_gate, b_gate,
           w_up, b_up, w_down, b_down, final_norm_g):
    B, S, D = x.shape
    assert w_in.shape[0] == 1, "single-layer block"
    assert S % ROW_TILE == 0 and (B * S) % ROW_TILE == 0 and S % SEG == 0
    assert S % SCAN_SEG == 0 and B % SCAN_BATCHES == 0
    NS = B * S
    NM = B * N_META_TOK
    NT = NS + NM
    R = -(-NT // ROW_TILE) * ROW_TILE
    assert R - NS == ROW_TILE and NT % TOK_TILE == 0 and NS % TOK_TILE == 0

    x2d = x.reshape(NS, D)
    tail_rows = jnp.concatenate([jnp.tile(meta_tokens.astype(x.dtype), (B, 1)),
                                 jnp.zeros((R - NT, D), x.dtype)], axis=0)

    w0 = w_in[0]
    kw = N_KV_HEADS * HEAD_DIM
    wq, wk, wv = w0[:, :Q_W], w0[:, Q_W:Q_W + kw], w0[:, Q_W + kw:Q_W + 2 * kw]
    dup = lambda w: jnp.concatenate([w[:, :HEAD_DIM]] * 2 + [w[:, HEAD_DIM:]] * 2, axis=1)
    w_qkv = jnp.concatenate([wq, dup(wk), dup(wv)], axis=1).astype(BF16)
    w_rest = w0[:, Q_W + 2 * kw:].astype(BF16)

    cos_t, sin_t = _rope_tables(S, NM)
    u, qkv = _norm_qkv(x2d, tail_rows, attn_norm_g[0].reshape(1, D), w_qkv, cos_t, sin_t,
                       S // ROW_TILE)
    rest = _proj_rest(u, w_rest)

    attn, attn_tail = _attention(qkv, attn_sinks[0], B, S)
    rec, rec_tail = _hgrn(rest, lower_bound_logits, rec_norm_g[0].reshape(1, REC_DIM), B, S)

    wr_hi = w_router[0].astype(BF16)
    wr_lo = (w_router[0] - wr_hi.astype(F32)).astype(BF16)
    lane_pad = ((0, 0), (0, LANES - N_EXPERTS))
    wr_pad = jnp.concatenate([jnp.pad(wr_hi, lane_pad), jnp.pad(wr_lo, lane_pad)], axis=1)
    br_pad = jnp.pad(b_router[0].reshape(1, N_EXPERTS), ((0, 0), (0, LANES - N_EXPERTS)),
                     constant_values=NEG_INF)
    h2, xn, logits = _mix(attn, attn_tail, rec, rec_tail, rest, x2d, tail_rows,
                          w_attn_proj[0].astype(BF16), w_rec_proj[0].astype(BF16),
                          w_out[0].astype(BF16), ffn_norm_g[0].reshape(1, D), wr_pad, br_pad)

    dest, gates, counts = _route(logits, NT)
    cnt = counts[0, :N_EXPERTS].astype(I32)
    n_blk = -(-(NT * TOP_K + N_EXPERTS * (MOE_BLK - 1)) // MOE_BLK)
    dest_flat = dest[:NT, :TOP_K].reshape(NT * TOP_K)

    xs = _dispatch(xn, dest_flat, cnt, NT, n_blk)
    y = _moe(xs, cnt, w_gate[0], b_gate[0], w_up[0], b_up[0], w_down[0], b_down[0])
    out = _combine(h2, gates, dest_flat, y, final_norm_g.reshape(1, D), NS)
    return out.reshape(B, S, D)
```
